```python
import jax
import jax.numpy as jnp
from jax import lax
import numpy as np

D_MODEL = 1024
BATCH = 4
SEQ = 8192
DEPTH = 4

N_EVEN = (DEPTH + 1) // 2
N_ODD = DEPTH // 2
NORM_EPS = 1e-6
F32 = jnp.float32

GLA_HEADS = 4
GLA_DK = D_MODEL // 16
GLA_DV = D_MODEL // 8
GLA_RANK = 16
GLA_TAU = 16.0
GLA_CHUNK = 64
GLA_SIZES = (GLA_HEADS * GLA_DK, GLA_HEADS * GLA_DK, GLA_HEADS * GLA_DV, GLA_HEADS * GLA_DV, GLA_RANK)
GLA_COLS = 2 * GLA_HEADS * GLA_DK + 2 * GLA_HEADS * GLA_DV + GLA_RANK

RWKV_HEADS = 8
RWKV_N = 64
RWKV_WIDTH = RWKV_HEADS * RWKV_N
RWKV_LORA_W = 64
RWKV_LORA_A = 64
RWKV_LORA_G = 128
RWKV_CHUNK = 16
RWKV_DECAY_SCALE = 0.6065306597126334
RWKV_GN_EPS = RWKV_N * 1e-5
RWKV_L2_EPS = 1e-12
RWKV_SIZES = (RWKV_WIDTH, RWKV_LORA_W, RWKV_WIDTH, RWKV_WIDTH, RWKV_LORA_A, RWKV_LORA_G)
RWKV_COLS = 3 * RWKV_WIDTH + RWKV_LORA_W + RWKV_LORA_A + RWKV_LORA_G
AB_COLS = GLA_COLS + RWKV_COLS

LRU_WIDTH = D_MODEL // 2
LRU_BLOCKS = 4
LRU_BLOCK = LRU_WIDTH // LRU_BLOCKS
LRU_C = 8.0
LRU_CONV = 4

MLSTM_HEADS = 4
MLSTM_DH = D_MODEL // 8
MLSTM_WIDTH = MLSTM_HEADS * MLSTM_DH
MLSTM_QKV_BLOCK = 4
MLSTM_NBLK = MLSTM_WIDTH // MLSTM_QKV_BLOCK
MLSTM_CONV = 4
MLSTM_CHUNK = 64
CD_SIZES = (LRU_WIDTH, LRU_WIDTH, MLSTM_WIDTH, MLSTM_WIDTH, 2 * MLSTM_HEADS)
CD_COLS = 2 * LRU_WIDTH + 2 * MLSTM_WIDTH + 2 * MLSTM_HEADS

D_MIX = GLA_HEADS * GLA_DV + RWKV_WIDTH

MEM_LEN = 256
XA_HEADS = 4
XA_DH = D_MODEL // XA_HEADS

D_FF = 2752
FFN_CONV = 3

kernel_name = 'hybrid_gla_rwkv7_rglru_mlstm_trunk'


def rms_norm(x, g):
    xf = x.astype(F32)
    y = xf * lax.rsqrt(jnp.mean(xf * xf, axis=-1, keepdims=True) + NORM_EPS)
    return (y * g.astype(F32)).astype(x.dtype)


def split_cols(x, sizes):
    outs, start = [], 0
    for s in sizes:
        outs.append(x[..., start:start + s])
        start += s
    return outs


def causal_dwconv(x, w, b):
    K, T = w.shape[0], x.shape[1]
    xp = jnp.pad(x, ((0, 0), (K - 1, 0), (0, 0)))
    y = b + xp[:, 0:T, :] * w[0]
    for j in range(1, K):
        y = y + xp[:, j:j + T, :] * w[j]
    return y


def to_chunks(t, chunk):
    B, T, H, d = t.shape
    return t.reshape(B, T // chunk, chunk, H, d).transpose(0, 3, 1, 2, 4)


def from_chunks(t):
    B, H, NC, L, d = t.shape
    return t.transpose(0, 2, 3, 1, 4).reshape(B, NC * L, H, d)


def linear_recurrence(a, u):
    def combine(left, right):
        a_l, u_l = left
        a_r, u_r = right
        return a_l * a_r, a_r * u_l + u_r
    _, h = lax.associative_scan(combine, (a, u), axis=1)
    return h


def gla_chunked(q, k, v, log_alpha):
    L = GLA_CHUNK
    q, k, v, la = (to_chunks(t, L) for t in (q, k, v, log_alpha))
    g = jnp.cumsum(la, axis=3)
    g_last = g[..., -1:, :]
    q_dec = q * jnp.exp(g)
    k_inv = k * jnp.exp(-g)
    k_end = k * jnp.exp(g_last - g)
    causal = jnp.tril(jnp.ones((L, L), dtype=bool))
    scores = jnp.where(causal, jnp.einsum('bhnlk,bhnsk->bhnls', q_dec, k_inv), 0.0)
    o_local = jnp.einsum('bhnls,bhnsv->bhnlv', scores, v)
    state_decay = jnp.exp(g_last[..., 0, :])

    def step(S, inp):
        q_c, ke_c, v_c, sd_c = inp
        o_state = jnp.einsum('bhlk,bhkv->bhlv', q_c, S)
        S = S * sd_c[..., None] + jnp.einsum('bhlk,bhlv->bhkv', ke_c, v_c)
        return S, o_state

    B, H = q.shape[:2]
    S0 = jnp.zeros((B, H, q.shape[-1], v.shape[-1]), F32)
    xs = tuple(jnp.moveaxis(t, 2, 0) for t in (q_dec, k_end, v, state_decay))
    _, o_state = lax.scan(step, S0, xs)
    return from_chunks(o_local + jnp.moveaxis(o_state, 0, 2))


def gla_group(p, w_alpha2, b_alpha, norm_g):
    B, T, _ = p.shape
    q, k, v, g, a_lr = split_cols(p.astype(F32), GLA_SIZES)
    log_alpha = jax.nn.log_sigmoid(a_lr @ w_alpha2 + b_alpha) / GLA_TAU
    hd = lambda t, d: t.reshape(B, T, GLA_HEADS, d)
    o = gla_chunked(hd(q, GLA_DK) * GLA_DK ** -0.5, hd(k, GLA_DK), hd(v, GLA_DV), hd(log_alpha, GLA_DK))
    o = o * lax.rsqrt(jnp.mean(o * o, axis=-1, keepdims=True) + NORM_EPS)
    return o.reshape(B, T, GLA_HEADS * GLA_DV) * norm_g * jax.nn.silu(g)


def rwkv7_chunked(r, log_w, k, v, a_vec, b_vec):
    L = RWKV_CHUNK
    r, log_w, k, v, a_vec, b_vec = (to_chunks(t, L) for t in (r, log_w, k, v, a_vec, b_vec))
    g_inc = jnp.cumsum(log_w, axis=3)
    g_last = g_inc[..., -1:, :]
    a_dec = a_vec * jnp.exp(g_inc - log_w)
    r_dec = r * jnp.exp(g_inc)
    b_inv = b_vec * jnp.exp(-g_inc)
    k_inv = k * jnp.exp(-g_inc)
    b_end = b_vec * jnp.exp(g_last - g_inc)
    k_end = k * jnp.exp(g_last - g_inc)
    strict = jnp.tril(jnp.ones((L, L), dtype=bool), -1)
    incl = jnp.tril(jnp.ones((L, L), dtype=bool))

    def pair(x, y, m):
        return jnp.where(m, jnp.einsum('bhnlk,bhnsk->bhnls', x, y), 0.0)

    A_ab = pair(a_dec, b_inv, strict)
    A_ak = pair(a_dec, k_inv, strict)
    A_rb = pair(r_dec, b_inv, incl)
    A_rk = pair(r_dec, k_inv, incl)
    N = r.shape[-1]
    rhs = jnp.concatenate([a_dec, jnp.einsum('bhnls,bhnsv->bhnlv', A_ak, v)], axis=-1)
    sol = lax.linalg.triangular_solve(jnp.eye(L, dtype=F32) - A_ab, rhs,
                                      left_side=True, lower=True, unit_diagonal=True)
    z_state, z_local = sol[..., :N], sol[..., N:]
    y_local = jnp.einsum('bhnls,bhnsv->bhnlv', A_rk, v)
    state_decay = jnp.exp(g_last[..., 0, :])

    def step(S, inp):
        zs, zl, rd, arb, yl, be, ke, vc, sd = inp
        z = jnp.einsum('bhlk,bhvk->bhlv', zs, S) + zl
        y = jnp.einsum('bhlk,bhvk->bhlv', rd, S) + jnp.einsum('bhls,bhsv->bhlv', arb, z) + yl
        S = (S * sd[:, :, None, :] + jnp.einsum('bhlv,bhlk->bhvk', z, be)
             + jnp.einsum('bhlv,bhlk->bhvk', vc, ke))
        return S, y

    B, H = r.shape[:2]
    S0 = jnp.zeros((B, H, N, N), F32)
    xs = tuple(jnp.moveaxis(t, 2, 0) for t in (z_state, z_local, r_dec, A_rb, y_local, b_end, k_end, v, state_decay))
    _, y = lax.scan(step, S0, xs)
    return from_chunks(jnp.moveaxis(y, 0, 2))


def rwkv7_group(p, mu, w0, w2, a0, a2, g2, k_k, k_a, r_k, ln_g, ln_b):
    B, T, _ = p.shape
    pf = p.astype(F32)
    prev = jnp.pad(pf, ((0, 0), (1, 0), (0, 0)))[:, :-1]
    pf = pf + (prev - pf) * mu
    r, w_lr, k, v, a_lr, g_lr = split_cols(pf, RWKV_SIZES)
    log_w = -RWKV_DECAY_SCALE * jax.nn.sigmoid(w0 + jnp.tanh(w_lr) @ w2)
    a = jax.nn.sigmoid(a0 + a_lr @ a2)
    g = jax.nn.sigmoid(g_lr) @ g2
    hd = lambda t: t.reshape(B, T, RWKV_HEADS, RWKV_N)
    kk = hd(k * k_k)
    kk = kk * lax.rsqrt(jnp.sum(kk * kk, axis=-1, keepdims=True) + RWKV_L2_EPS)
    k = k * (1.0 + (a - 1.0) * k_a)
    rh, kh, vh = hd(r), hd(k), hd(v)
    y = rwkv7_chunked(rh, hd(log_w), kh, vh, -kk, kk * hd(a))
    mu_y = jnp.mean(y, axis=-1, keepdims=True)
    var = jnp.mean(jnp.square(y - mu_y), axis=-1, keepdims=True)
    y = ((y - mu_y) * lax.rsqrt(var + RWKV_GN_EPS)).reshape(B, T, RWKV_WIDTH) * ln_g + ln_b
    bonus = jnp.sum(rh * kh * r_k.reshape(RWKV_HEADS, RWKV_N), axis=-1, keepdims=True) * vh
    y = y + bonus.reshape(B, T, RWKV_WIDTH)
    return y * g


def rglru_group(xb, gate, conv_w, conv_b, gate_w, gate_b, lam):
    B, T, _ = xb.shape
    xc = causal_dwconv(xb.astype(F32), conv_w, conv_b)
    xblk = xc.reshape(B, T, LRU_BLOCKS, LRU_BLOCK)
    gates = jnp.einsum('btnd,gnde->gbtne', xblk, gate_w).reshape(2, B, T, LRU_WIDTH) + gate_b[:, None, None, :]
    r_gate = jax.nn.sigmoid(gates[0])
    i_gate = jax.nn.sigmoid(gates[1])
    log_a = -LRU_C * r_gate * jax.nn.softplus(-lam)
    u = jnp.sqrt(-jnp.expm1(2.0 * log_a)) * (i_gate * xc)
    h = linear_recurrence(jnp.exp(log_a), u)
    return h * jax.nn.gelu(gate.astype(F32))


def mlstm_chunked(q, k, v, i_pre, log_f):
    L = MLSTM_CHUNK
    q, k, v = (to_chunks(t, L) for t in (q, k, v))
    i_pre, log_f = (to_chunks(t[..., None], L)[..., 0] for t in (i_pre, log_f))
    b = jnp.cumsum(log_f, axis=-1)
    b_last = b[..., -1]
    incl = jnp.tril(jnp.ones((L, L), dtype=bool))
    log_d = jnp.where(incl, b[..., :, None] - b[..., None, :] + i_pre[..., None, :], -jnp.inf)
    m_loc = jnp.max(log_d, axis=-1)
    w_loc = jnp.exp(log_d - m_loc[..., None]) * jnp.einsum('bhnld,bhnsd->bhnls', q, k)
    num_loc = jnp.einsum('bhnls,bhnsd->bhnld', w_loc, v)
    den_loc = jnp.sum(w_loc, axis=-1)
    log_e = b_last[..., None] - b + i_pre
    m_end = jnp.max(log_e, axis=-1)
    w_end = jnp.exp(log_e - m_end[..., None])

    def step(carry, inp):
        C, n, m = carry
        qc, kc, vc, bc, mlc, numc, denc, blc, mec, wec = inp
        m_t = jnp.maximum(bc + m[..., None], mlc)
        s_state = jnp.exp(bc + m[..., None] - m_t)
        s_loc = jnp.exp(mlc - m_t)
        num = s_state[..., None] * jnp.einsum('bhld,bhde->bhle', qc, C) + s_loc[..., None] * numc
        den = s_state * jnp.einsum('bhld,bhd->bhl', qc, n) + s_loc * denc
        h = num / jnp.maximum(jnp.abs(den), jnp.exp(-m_t))[..., None]
        m_new = jnp.maximum(blc + m, mec)
        c_state = jnp.exp(blc + m - m_new)
        c_in = jnp.exp(mec - m_new)[..., None] * wec
        C = c_state[..., None, None] * C + jnp.einsum('bhl,bhld,bhle->bhde', c_in, kc, vc)
        n = c_state[..., None] * n + jnp.einsum('bhl,bhld->bhd', c_in, kc)
        return (C, n, m_new), h

    B, H, _, _, DH = q.shape
    init = (jnp.zeros((B, H, DH, DH), F32), jnp.zeros((B, H, DH), F32), jnp.zeros((B, H), F32))
    xs = tuple(jnp.moveaxis(t, 2, 0) for t in (q, k, v, b, m_loc, num_loc, den_loc, b_last, m_end, w_end))
    _, h = lax.scan(step, init, xs)
    return from_chunks(jnp.moveaxis(h, 0, 2))


def mlstm_group(xm, o_pre, if_pre, conv_w, conv_b, qkv_w, b_if, norm_g):
    B, T, _ = xm.shape
    xmf = xm.astype(F32)
    xc = jax.nn.silu(causal_dwconv(xmf, conv_w, conv_b))

    def headwise(t, w):
        y = jnp.einsum('btnd,nde->btne', t.reshape(B, T, MLSTM_NBLK, MLSTM_QKV_BLOCK), w)
        return y.reshape(B, T, MLSTM_HEADS, MLSTM_DH)

    q = headwise(xc, qkv_w[0])
    k = headwise(xc, qkv_w[1]) * MLSTM_DH ** -0.5
    v = headwise(xmf, qkv_w[2])
    gates = if_pre.astype(F32) + b_if
    i_pre = gates[..., :MLSTM_HEADS]
    log_f = jax.nn.log_sigmoid(gates[..., MLSTM_HEADS:])
    h = mlstm_chunked(q, k, v, i_pre, log_f)
    h = h * lax.rsqrt(jnp.mean(h * h, axis=-1, keepdims=True) + NORM_EPS)
    return jax.nn.sigmoid(o_pre.astype(F32)) * h.reshape(B, T, MLSTM_WIDTH) * norm_g


def memory_cross_attention(xn, mem_n, wq, wkv, wo):
    B, T, _ = xn.shape
    M = mem_n.shape[1]
    q = (xn @ wq).reshape(B, T, XA_HEADS, XA_DH)
    kv = (mem_n @ wkv).reshape(B, M, 2, XA_HEADS, XA_DH)
    k, v = kv[:, :, 0], kv[:, :, 1]
    s = jnp.einsum('bthd,bmhd->bhtm', q, k).astype(F32) * XA_DH ** -0.5
    p = jax.nn.softmax(s, axis=-1).astype(v.dtype)
    o = jnp.einsum('bhtm,bmhd->bthd', p, v).reshape(B, T, XA_HEADS * XA_DH)
    return o @ wo


def conv_glu_ffn(xn, w_up, conv_w, conv_b, w_down):
    u, gt = split_cols(xn @ w_up, (D_FF, D_FF))
    u = causal_dwconv(u, conv_w, conv_b)
    return (jax.nn.silu(u) * gt) @ w_down


def setup_inputs(seed: int = 0) -> dict:
    key = jax.random.key(seed)
    ks = jax.random.split(key, 64)
    counter = [0]

    def nxt():
        counter[0] += 1
        return ks[counter[0] - 1]

    def nrm(shape, scale):
        return jax.random.normal(nxt(), shape, F32) * scale

    def gain(shape):
        return 1.0 + nrm(shape, 0.02)

    lam_s = jax.random.uniform(nxt(), (N_ODD, LRU_WIDTH), F32, 0.9, 0.999) ** (1.0 / LRU_C)
    lru_lambda = jnp.log(lam_s) - jnp.log1p(-lam_s)
    f_bias = jnp.linspace(3.0, 6.0, MLSTM_HEADS, dtype=F32)[None, :] + nrm((N_ODD, MLSTM_HEADS), 0.1)
    mlstm_b_if = jnp.concatenate([nrm((N_ODD, MLSTM_HEADS), 0.1), f_bias], axis=-1)
    return {
        'x': nrm((BATCH, SEQ, D_MODEL), 1.0),
        'mem': nrm((BATCH, MEM_LEN, D_MODEL), 1.0),
        'mem_norm_g': gain((D_MODEL,)),
        'norm_mix_g': gain((DEPTH, D_MODEL)),
        'ab_w_in': nrm((N_EVEN, D_MODEL, AB_COLS), D_MODEL ** -0.5),
        'gla_w_alpha2': nrm((N_EVEN, GLA_RANK, GLA_HEADS * GLA_DK), GLA_RANK ** -0.5),
        'gla_b_alpha': nrm((N_EVEN, GLA_HEADS * GLA_DK), 0.1),
        'gla_norm_g': gain((N_EVEN, GLA_HEADS * GLA_DV)),
        'rwkv_mu': jax.random.uniform(nxt(), (N_EVEN, RWKV_COLS), F32),
        'rwkv_w0': nrm((N_EVEN, RWKV_WIDTH), 0.5),
        'rwkv_w2': nrm((N_EVEN, RWKV_LORA_W, RWKV_WIDTH), RWKV_LORA_W ** -0.5),
        'rwkv_a0': nrm((N_EVEN, RWKV_WIDTH), 0.1),
        'rwkv_a2': nrm((N_EVEN, RWKV_LORA_A, RWKV_WIDTH), RWKV_LORA_A ** -0.5),
        'rwkv_g2': nrm((N_EVEN, RWKV_LORA_G, RWKV_WIDTH), RWKV_LORA_G ** -0.5),
        'rwkv_k_k': 0.85 + nrm((N_EVEN, RWKV_WIDTH), 0.02),
        'rwkv_k_a': gain((N_EVEN, RWKV_WIDTH)),
        'rwkv_r_k': nrm((N_EVEN, RWKV_WIDTH), 0.1),
        'rwkv_ln_g': gain((N_EVEN, RWKV_WIDTH)),
        'rwkv_ln_b': nrm((N_EVEN, RWKV_WIDTH), 0.02),
        'cd_w_in': nrm((N_ODD, D_MODEL, CD_COLS), D_MODEL ** -0.5),
        'lru_conv_w': nrm((N_ODD, LRU_CONV, LRU_WIDTH), LRU_CONV ** -0.5),
        'lru_conv_b': nrm((N_ODD, LRU_WIDTH), 0.02),
        'lru_gate_w': nrm((N_ODD, 2, LRU_BLOCKS, LRU_BLOCK, LRU_BLOCK), LRU_BLOCK ** -0.5),
        'lru_gate_b': nrm((N_ODD, 2, LRU_WIDTH), 0.02),
        'lru_lambda': lru_lambda,
        'mlstm_conv_w': nrm((N_ODD, MLSTM_CONV, MLSTM_WIDTH), MLSTM_CONV ** -0.5),
        'mlstm_conv_b': nrm((N_ODD, MLSTM_WIDTH), 0.02),
        'mlstm_qkv_w': nrm((N_ODD, 3, MLSTM_NBLK, MLSTM_QKV_BLOCK, MLSTM_QKV_BLOCK), MLSTM_QKV_BLOCK ** -0.5),
        'mlstm_b_if': mlstm_b_if,
        'mlstm_norm_g': gain((N_ODD, MLSTM_WIDTH)),
        'w_mix_out': nrm((DEPTH, D_MIX, D_MODEL), D_MIX ** -0.5),
        'norm_xattn_g': gain((DEPTH, D_MODEL)),
        'xattn_wq': nrm((DEPTH, D_MODEL, XA_HEADS * XA_DH), D_MODEL ** -0.5),
        'xattn_wkv': nrm((DEPTH, D_MODEL, 2 * XA_HEADS * XA_DH), D_MODEL ** -0.5),
        'xattn_wo': nrm((DEPTH, XA_HEADS * XA_DH, D_MODEL), D_MODEL ** -0.5),
        'norm_ffn_g': gain((DEPTH, D_MODEL)),
        'ffn_w_up': nrm((DEPTH, D_MODEL, 2 * D_FF), D_MODEL ** -0.5),
        'ffn_conv_w': nrm((DEPTH, FFN_CONV, D_FF), FFN_CONV ** -0.5),
        'ffn_conv_b': nrm((DEPTH, D_FF), 0.02),
        'ffn_w_down': nrm((DEPTH, D_FF, D_MODEL), D_FF ** -0.5),
        'final_norm_g': gain((D_MODEL,)),
    }


def reference(x, mem, mem_norm_g, norm_mix_g, ab_w_in, gla_w_alpha2, gla_b_alpha, gla_norm_g,
              rwkv_mu, rwkv_w0, rwkv_w2, rwkv_a0, rwkv_a2, rwkv_g2, rwkv_k_k, rwkv_k_a, rwkv_r_k,
              rwkv_ln_g, rwkv_ln_b, cd_w_in, lru_conv_w, lru_conv_b, lru_gate_w, lru_gate_b, lru_lambda,
              mlstm_conv_w, mlstm_conv_b, mlstm_qkv_w, mlstm_b_if, mlstm_norm_g, w_mix_out,
              norm_xattn_g, xattn_wq, xattn_wkv, xattn_wo, norm_ffn_g, ffn_w_up, ffn_conv_w, ffn_conv_b,
              ffn_w_down, final_norm_g):
    mem_n = rms_norm(mem, mem_norm_g)
    h = x
    for layer in range(DEPTH):
        j = layer // 2
        xn = rms_norm(h, norm_mix_g[layer])
        if layer % 2 == 0:
            proj = xn @ ab_w_in[j]
            y_a = gla_group(proj[..., :GLA_COLS], gla_w_alpha2[j], gla_b_alpha[j], gla_norm_g[j])
            y_b = rwkv7_group(proj[..., GLA_COLS:], rwkv_mu[j], rwkv_w0[j], rwkv_w2[j], rwkv_a0[j],
                              rwkv_a2[j], rwkv_g2[j], rwkv_k_k[j], rwkv_k_a[j], rwkv_r_k[j],
                              rwkv_ln_g[j], rwkv_ln_b[j])
        else:
            proj = xn @ cd_w_in[j]
            lru_x, lru_g, m_x, m_o, m_if = split_cols(proj, CD_SIZES)
            y_a = rglru_group(lru_x, lru_g, lru_conv_w[j], lru_conv_b[j], lru_gate_w[j], lru_gate_b[j],
                              lru_lambda[j])
            y_b = mlstm_group(m_x, m_o, m_if, mlstm_conv_w[j], mlstm_conv_b[j], mlstm_qkv_w[j],
                              mlstm_b_if[j], mlstm_norm_g[j])
        mixed = jnp.concatenate([y_a, y_b], axis=-1).astype(h.dtype)
        h = h + mixed @ w_mix_out[layer]
        h = h + memory_cross_attention(rms_norm(h, norm_xattn_g[layer]), mem_n,
                                       xattn_wq[layer], xattn_wkv[layer], xattn_wo[layer])
        h = h + conv_glu_ffn(rms_norm(h, norm_ffn_g[layer]), ffn_w_up[layer], ffn_conv_w[layer],
                             ffn_conv_b[layer], ffn_w_down[layer])
    return rms_norm(h, final_norm_g)
```

```python
import functools

import jax
import jax.numpy as jnp
from jax import lax
from jax.experimental import pallas as pl
from jax.experimental.pallas import tpu as pltpu

F32 = jnp.float32
BF16 = jnp.bfloat16

D_MODEL = 1024
NORM_EPS = 1e-6
LANES = 128
SUBLANES = 8
VMEM_LIMIT_BYTES = 56 * 1024 * 1024

GLA_HEADS, GLA_DK, GLA_DV, GLA_RANK, GLA_TAU, GLA_CHUNK = 4, 64, 128, 16, 16.0, 64
GLA_COLS = 2 * GLA_HEADS * GLA_DK + 2 * GLA_HEADS * GLA_DV + GLA_RANK
GLA_COLS_PAD = 13 * LANES

RWKV_HEADS, RWKV_N, RWKV_WIDTH = 8, 64, 512
RWKV_LORA_W, RWKV_LORA_A, RWKV_LORA_G = 64, 64, 128
RWKV_CHUNK = 64
RWKV_DECAY_SCALE = 0.6065306597126334
RWKV_GN_EPS = RWKV_N * 1e-5
RWKV_L2_EPS = 1e-12
RWKV_COLS = 3 * RWKV_WIDTH + RWKV_LORA_W + RWKV_LORA_A + RWKV_LORA_G

LRU_WIDTH, LRU_BLOCKS, LRU_BLOCK, LRU_C, LRU_CONV = 512, 4, 128, 8.0, 4
MLSTM_HEADS, MLSTM_DH, MLSTM_WIDTH, MLSTM_CONV, MLSTM_CHUNK = 4, 128, 512, 4, 64
MLSTM_COLS_PAD = 2 * MLSTM_WIDTH + LANES

XA_HEADS, XA_DH, MEM_LEN = 4, 256, 256
D_FF, FFN_CONV = 2752, 3
D_FF_PAD = 22 * LANES

MIX_TILE = 256
ROW_TILE = 512


def _mm(a, b):
    return jnp.dot(a.astype(BF16), b.astype(BF16), preferred_element_type=F32)


def _mm_nt(a, b):
    return lax.dot_general(a.astype(BF16), b.astype(BF16), (((1,), (1,)), ((), ())), preferred_element_type=F32)


def _mm_tn(a, b):
    return lax.dot_general(a.astype(BF16), b.astype(BF16), (((0,), (0,)), ((), ())), preferred_element_type=F32)


def _split3(x):
    x1 = x.astype(BF16)
    r1 = x - x1.astype(F32)
    x2 = r1.astype(BF16)
    x3 = (r1 - x2.astype(F32)).astype(BF16)
    return x1, x2, x3


def _mm_exact_lhs01(m01, x):
    x1, x2, x3 = _split3(x)
    d = lambda y: jnp.dot(m01, y, preferred_element_type=F32)
    return d(x1) + d(x2) + d(x3)


def _mm_exact_rhs01(x, m01):
    x1, x2, x3 = _split3(x)
    d = lambda y: jnp.dot(y, m01, preferred_element_type=F32)
    return d(x1) + d(x2) + d(x3)


def _rmsnorm(x, g):
    return x * lax.rsqrt(jnp.mean(x * x, axis=-1, keepdims=True) + NORM_EPS) * g


def _log_sigmoid(x):
    return jnp.minimum(x, 0.0) - jnp.log1p(jnp.exp(-jnp.abs(x)))


def _softplus(x):
    return jnp.maximum(x, 0.0) + jnp.log1p(jnp.exp(-jnp.abs(x)))


def _silu(x):
    return x * jax.nn.sigmoid(x)


def _tri_masks(n):
    r = lax.broadcasted_iota(jnp.int32, (n, n), 0)
    c = lax.broadcasted_iota(jnp.int32, (n, n), 1)
    return r >= c, r > c


def _shift_rows(x, s, fill):
    if s == 0:
        return x
    rolled = pltpu.roll(x, s, 0)
    row = lax.broadcasted_iota(jnp.int32, x.shape, 0)
    return jnp.where(row >= s, rolled, fill)


def _causal_conv(x, halo, w_ref, b_ref, width):
    t = x.shape[0]
    xx = jnp.concatenate([halo, x], axis=0)
    y = b_ref[...] + w_ref[width - 1:width, :] * x
    for j in range(width - 1):
        s = width - 1 - j
        y = y + w_ref[j:j + 1, :] * pltpu.roll(xx, s, 0)[SUBLANES:SUBLANES + t, :]
    return y


def _const_spec(shape):
    nd = len(shape)
    return pl.BlockSpec(shape, lambda *_: (0,) * nd, pipeline_mode=pl.Buffered(1))


def _params(sem):
    return pltpu.CompilerParams(dimension_semantics=sem, vmem_limit_bytes=VMEM_LIMIT_BYTES)


def _norm_matmul_body(n_out, x_ref, g_ref, *refs):
    xn = _rmsnorm(x_ref[...], g_ref[...]).astype(BF16)
    for w_ref, o_ref in zip(refs[:n_out], refs[n_out:]):
        o_ref[...] = jnp.dot(xn, w_ref[...], preferred_element_type=F32).astype(o_ref.dtype)


def norm_matmul(x2d, g, ws, out_dtype=F32, name="norm_matmul"):
    n, d = x2d.shape
    tm = min(ROW_TILE, n)
    assert n % tm == 0
    return pl.pallas_call(
        functools.partial(_norm_matmul_body, len(ws)),
        out_shape=[jax.ShapeDtypeStruct((n, w.shape[1]), out_dtype) for w in ws],
        grid=(n // tm,),
        in_specs=[pl.BlockSpec((tm, d), lambda i: (i, 0)), _const_spec((1, d))]
        + [_const_spec(w.shape) for w in ws],
        out_specs=[pl.BlockSpec((tm, w.shape[1]), lambda i: (i, 0)) for w in ws],
        compiler_params=_params(("parallel",)),
        name=name,
    )(x2d, g.reshape(1, d), *ws)


def _gla_body(p_ref, wa_ref, ba_ref, ng_ref, o_ref, st_ref):
    L = GLA_CHUNK
    n_chunks = p_ref.shape[1] // L
    hk = GLA_HEADS * GLA_DK
    hv = GLA_HEADS * GLA_DV

    @pl.when(pl.program_id(1) == 0)
    def _():
        st_ref[...] = jnp.zeros_like(st_ref)

    def chunk(c, carry):
        incl, _ = _tri_masks(L)
        tri01 = jnp.where(incl, 1.0, 0.0).astype(BF16)
        t0 = pl.multiple_of(c * L, L)
        rows = pl.ds(t0, L)
        q = p_ref[0, rows, 0:hk]
        k = p_ref[0, rows, hk:2 * hk]
        v = p_ref[0, rows, 2 * hk:2 * hk + hv]
        gt = p_ref[0, rows, 2 * hk + hv:2 * hk + 2 * hv]
        a_lr = p_ref[0, rows, 2 * hk + 2 * hv:GLA_COLS_PAD]
        la = _log_sigmoid(_mm(a_lr, wa_ref[...]) + ba_ref[...]) * (1.0 / GLA_TAU)
        g = _mm_exact_lhs01(tri01, la)
        g_last = g[L - 1:L, :]
        q_dec = (q * GLA_DK ** -0.5) * jnp.exp(g)
        k_inv = k * jnp.exp(-g)
        k_end = k * jnp.exp(g_last - g)
        sd = jnp.exp(g_last)
        for h in range(GLA_HEADS):
            ks = slice(h * GLA_DK, (h + 1) * GLA_DK)
            vs = slice(h * GLA_DV, (h + 1) * GLA_DV)
            qd, vh = q_dec[:, ks], v[:, vs]
            sc = jnp.where(incl, _mm_nt(qd, k_inv[:, ks]), 0.0)
            st = st_ref[h]
            o = _mm(sc, vh) + _mm_nt(qd, st)
            st_ref[h] = st * sd[:, ks] + _mm_tn(vh, k_end[:, ks])
            o = o * lax.rsqrt(jnp.mean(o * o, axis=-1, keepdims=True) + NORM_EPS)
            o_ref[0, rows, vs] = o * ng_ref[:, vs] * _silu(gt[:, vs])
        return carry

    lax.fori_loop(0, n_chunks, chunk, 0)


def gla_mixer(pa, w_alpha2, b_alpha, norm_g):
    b, t, _ = pa.shape
    tt = min(MIX_TILE, t)
    hk, hv = GLA_HEADS * GLA_DK, GLA_HEADS * GLA_DV
    wa = jnp.zeros((LANES, hk), BF16).at[:GLA_RANK].set(w_alpha2.astype(BF16))
    return pl.pallas_call(
        _gla_body,
        out_shape=jax.ShapeDtypeStruct((b, t, hv), F32),
        grid=(b, t // tt),
        in_specs=[pl.BlockSpec((1, tt, GLA_COLS_PAD), lambda i, j: (i, j, 0)),
                  _const_spec((LANES, hk)), _const_spec((1, hk)), _const_spec((1, hv))],
        out_specs=pl.BlockSpec((1, tt, hv), lambda i, j: (i, j, 0)),
        scratch_shapes=[pltpu.VMEM((GLA_HEADS, GLA_DV, GLA_DK), F32)],
        compiler_params=_params(("parallel", "arbitrary")),
        name="gla_mixer",
    )(pa, wa, b_alpha.reshape(1, hk), norm_g.reshape(1, hv))


def _rwkv_body(p_ref, mu_ref, w0_ref, w2_ref, a0_ref, a2_ref, g2_ref, kk_ref, ka_ref, rk_ref, lng_ref, lnb_ref,
               seg_ref, o_ref, s_ref, prev_ref):
    L = RWKV_CHUNK
    W = RWKV_WIDTH
    N = RWKV_N
    n_chunks = p_ref.shape[1] // L

    @pl.when(pl.program_id(1) == 0)
    def _():
        s_ref[...] = jnp.zeros_like(s_ref)
        prev_ref[...] = jnp.zeros_like(prev_ref)

    def seg_sum(x):
        return _mm_exact_rhs01(x, seg_ref[...])

    def chunk(c, carry):
        incl, strict = _tri_masks(L)
        tri01 = jnp.where(incl, 1.0, 0.0).astype(BF16)
        t0 = pl.multiple_of(c * L, L)
        rows = pl.ds(t0, L)
        p = p_ref[0, rows, :]
        row = lax.broadcasted_iota(jnp.int32, p.shape, 0)
        sh = jnp.where(row == 0, prev_ref[0:1, :], pltpu.roll(p, 1, 0))
        prev_ref[0:1, :] = p[L - 1:L, :]
        pf = p + (sh - p) * mu_ref[...]
        r = pf[:, 0:W]
        k = pf[:, W:2 * W]
        v = pf[:, 2 * W:3 * W]
        wa = pf[:, 3 * W:3 * W + LANES]
        g_lr = pf[:, 3 * W + LANES:3 * W + 2 * LANES]
        log_w = -RWKV_DECAY_SCALE * jax.nn.sigmoid(w0_ref[...] + _mm(jnp.tanh(wa), w2_ref[...]))
        a = jax.nn.sigmoid(a0_ref[...] + _mm(wa, a2_ref[...]))
        gate = _mm(jax.nn.sigmoid(g_lr), g2_ref[...])
        kk = k * kk_ref[...]
        kk = kk * lax.rsqrt(seg_sum(kk * kk) + RWKV_L2_EPS)
        k = k * (1.0 + (a - 1.0) * ka_ref[...])
        b_vec = kk * a
        g = _mm_exact_lhs01(tri01, log_w)
        g_last = g[L - 1:L, :]
        e_neg = jnp.exp(-g)
        e_end = jnp.exp(g_last - g)
        a_dec = -kk * jnp.exp(g - log_w)
        r_dec = r * jnp.exp(g)
        b_inv = b_vec * e_neg
        k_inv = k * e_neg
        b_end = b_vec * e_end
        k_end = k * e_end
        sd = jnp.exp(g_last)
        ys = []
        for h in range(RWKV_HEADS):
            hs = slice(h * N, (h + 1) * N)
            ad, rd, bi, ki, be, ke, vh = (x[:, hs] for x in (a_dec, r_dec, b_inv, k_inv, b_end, k_end, v))
            a_ab = jnp.where(strict, _mm_nt(ad, bi), 0.0)
            a_ak = jnp.where(strict, _mm_nt(ad, ki), 0.0)
            a_rb = jnp.where(incl, _mm_nt(rd, bi), 0.0)
            a_rk = jnp.where(incl, _mm_nt(rd, ki), 0.0)
            x = jnp.concatenate([ad, _mm(a_ak, vh)], axis=1)
            pw = a_ab
            for i in range(6):
                x = x + _mm(pw, x)
                if i < 5:
                    pw = _mm(pw, pw)
            ry = _mm(a_rb, x)
            rm = rd + ry[:, 0:N]
            y0 = ry[:, N:2 * N] + _mm(a_rk, vh)
            mc = _mm_tn(x, be)
            cc = mc[N:2 * N, :] + _mm_tn(vh, ke)
            s = s_ref[h]
            ys.append(_mm_nt(rm, s) + y0)
            s_ref[h] = s * sd[:, hs] + _mm(s, mc[0:N, :]) + cc
        y = jnp.concatenate(ys, axis=1)
        mean = seg_sum(y) * (1.0 / N)
        dy = y - mean
        var = seg_sum(dy * dy) * (1.0 / N)
        y = dy * lax.rsqrt(var + RWKV_GN_EPS) * lng_ref[...] + lnb_ref[...]
        y = y + seg_sum(r * k * rk_ref[...]) * v
        o_ref[0, rows, :] = y * gate
        return carry

    lax.fori_loop(0, n_chunks, chunk, 0)


def rwkv_mixer(pb, mu, w0, w2, a0, a2, g2, k_k, k_a, r_k, ln_g, ln_b):
    b, t, cols = pb.shape
    tt = min(MIX_TILE, t)
    W = RWKV_WIDTH
    row = lambda x: x.reshape(1, -1)
    w2p = jnp.zeros((LANES, W), BF16).at[:RWKV_LORA_W].set(w2.astype(BF16))
    a2p = jnp.zeros((LANES, W), BF16).at[RWKV_LORA_W:].set(a2.astype(BF16))
    head_of = jnp.arange(W) // RWKV_N
    seg = (head_of[:, None] == head_of[None, :]).astype(BF16)
    consts = [row(mu), row(w0), w2p, row(a0), a2p, g2.astype(BF16), row(k_k), row(k_a), row(r_k), row(ln_g),
              row(ln_b), seg]
    return pl.pallas_call(
        _rwkv_body,
        out_shape=jax.ShapeDtypeStruct((b, t, W), F32),
        grid=(b, t // tt),
        in_specs=[pl.BlockSpec((1, tt, cols), lambda i, j: (i, j, 0))] + [_const_spec(x.shape) for x in consts],
        out_specs=pl.BlockSpec((1, tt, W), lambda i, j: (i, j, 0)),
        scratch_shapes=[pltpu.VMEM((RWKV_HEADS, RWKV_N, RWKV_N), F32), pltpu.VMEM((SUBLANES, cols), F32)],
        compiler_params=_params(("parallel", "arbitrary")),
        name="rwkv_mixer",
    )(pb, *consts)


def _rglru_body(p_ref, cw_ref, cb_ref, gw_ref, gb_ref, lam_ref, o_ref, xprev_ref, hprev_ref):
    tt = p_ref.shape[1]
    W = LRU_WIDTH

    @pl.when(pl.program_id(1) == 0)
    def _():
        xprev_ref[...] = jnp.zeros_like(xprev_ref)
        hprev_ref[...] = jnp.zeros_like(hprev_ref)

    x = p_ref[0, :, 0:W]
    gate = p_ref[0, :, W:2 * W]
    xc = _causal_conv(x, xprev_ref[...], cw_ref, cb_ref, LRU_CONV)
    xprev_ref[...] = x[tt - SUBLANES:tt, :]
    xcb = xc.astype(BF16)
    pre = []
    for gi in range(2):
        pre.append(jnp.concatenate(
            [jnp.dot(xcb[:, n * LRU_BLOCK:(n + 1) * LRU_BLOCK], gw_ref[gi, n], preferred_element_type=F32)
             for n in range(LRU_BLOCKS)], axis=1) + gb_ref[gi:gi + 1, :])
    r_gate = jax.nn.sigmoid(pre[0])
    i_gate = jax.nn.sigmoid(pre[1])
    log_a = -LRU_C * r_gate * _softplus(-lam_ref[...])
    a = jnp.exp(log_a)
    u = jnp.sqrt(-jnp.tanh(log_a) * (a * a + 1.0)) * (i_gate * xc)
    d = 1
    while d < tt:
        u = u + a * _shift_rows(u, d, 0.0)
        a = a * _shift_rows(a, d, 1.0)
        d *= 2
    h = u + a * hprev_ref[0:1, :]
    hprev_ref[0:1, :] = h[tt - 1:tt, :]
    o_ref[0] = h * jax.nn.gelu(gate)


def rglru_mixer(pc, conv_w, conv_b, gate_w, gate_b, lam):
    b, t, cols = pc.shape
    tt = min(MIX_TILE, t)
    W = LRU_WIDTH
    cw = jnp.zeros((SUBLANES, W), F32).at[:LRU_CONV].set(conv_w)
    consts = [cw, conv_b.reshape(1, W), gate_w.astype(BF16), gate_b, lam.reshape(1, W)]
    return pl.pallas_call(
        _rglru_body,
        out_shape=jax.ShapeDtypeStruct((b, t, W), F32),
        grid=(b, t // tt),
        in_specs=[pl.BlockSpec((1, tt, cols), lambda i, j: (i, j, 0))] + [_const_spec(x.shape) for x in consts],
        out_specs=pl.BlockSpec((1, tt, W), lambda i, j: (i, j, 0)),
        scratch_shapes=[pltpu.VMEM((SUBLANES, W), F32), pltpu.VMEM((SUBLANES, W), F32)],
        compiler_params=_params(("parallel", "arbitrary")),
        name="rglru_mixer",
    )(pc, *consts)


def _mlstm_body(p_ref, cw_ref, cb_ref, wq_ref, wk_ref, wv_ref, bif_ref, ng_ref, o_ref,
                xprev_ref, c_ref, n_ref, m_ref, q_s, k_s, v_s):
    tt = p_ref.shape[1]
    L = MLSTM_CHUNK
    W = MLSTM_WIDTH
    DH = MLSTM_DH
    H = MLSTM_HEADS
    n_chunks = tt // L

    @pl.when(pl.program_id(1) == 0)
    def _():
        xprev_ref[...] = jnp.zeros_like(xprev_ref)
        c_ref[...] = jnp.zeros_like(c_ref)
        n_ref[...] = jnp.zeros_like(n_ref)
        m_ref[...] = jnp.zeros_like(m_ref)

    x = p_ref[0, :, 0:W]
    xc = _silu(_causal_conv(x, xprev_ref[...], cw_ref, cb_ref, MLSTM_CONV))
    xprev_ref[...] = x[tt - SUBLANES:tt, :]
    xcb = xc.astype(BF16)
    q_s[...] = jnp.dot(xcb, wq_ref[...], preferred_element_type=F32)
    k_s[...] = jnp.dot(xcb, wk_ref[...], preferred_element_type=F32) * DH ** -0.5
    v_s[...] = jnp.dot(x.astype(BF16), wv_ref[...], preferred_element_type=F32)

    def chunk(c, carry):
        incl, _ = _tri_masks(L)
        tri01 = jnp.where(incl, 1.0, 0.0).astype(BF16)
        eye = jnp.where(lax.broadcasted_iota(jnp.int32, (L, L), 0) == lax.broadcasted_iota(jnp.int32, (L, L), 1),
                        1.0, 0.0)

        def to_row(col):
            return jnp.sum(col * eye, axis=0, keepdims=True)

        t0 = pl.multiple_of(c * L, L)
        rows = pl.ds(t0, L)
        gates = p_ref[0, rows, 2 * W:2 * W + LANES] + bif_ref[...]
        bcum = _mm_exact_lhs01(tri01, _log_sigmoid(gates))
        for h in range(H):
            hs = slice(h * DH, (h + 1) * DH)
            qh, kh, vh = q_s[rows, hs], k_s[rows, hs], v_s[rows, hs]
            i_col = gates[:, h:h + 1]
            b_col = bcum[:, H + h:H + h + 1]
            b_last = b_col[L - 1:L, :]
            i_row, b_row = to_row(i_col), to_row(b_col)
            log_d = jnp.where(incl, b_col - b_row + i_row, -jnp.inf)
            m_loc = jnp.max(log_d, axis=1, keepdims=True)
            w_loc = jnp.exp(log_d - m_loc) * _mm_nt(qh, kh)
            num_loc = _mm(w_loc, vh)
            den_loc = jnp.sum(w_loc, axis=1, keepdims=True)
            log_e = b_last - b_col + i_col
            m_end = jnp.max(log_e, axis=0, keepdims=True)
            w_end = jnp.exp(log_e - m_end)
            cm = c_ref[h]
            nv = n_ref[h, 0:1, :]
            m = m_ref[h, 0:1, 0:1]
            m_t = jnp.maximum(b_col + m, m_loc)
            s_state = jnp.exp(b_col + m - m_t)
            s_loc = jnp.exp(m_loc - m_t)
            num = s_state * _mm(qh, cm) + s_loc * num_loc
            den = s_state * jnp.sum(qh * nv, axis=1, keepdims=True) + s_loc * den_loc
            hv = num / jnp.maximum(jnp.abs(den), jnp.exp(-m_t))
            m_new = jnp.maximum(b_last + m, m_end)
            c_state = jnp.exp(b_last + m - m_new)
            kin = kh * (jnp.exp(m_end - m_new) * w_end)
            c_ref[h] = c_state * cm + _mm_tn(kin, vh)
            n_ref[h, 0:1, :] = c_state * nv + jnp.sum(kin, axis=0, keepdims=True)
            m_ref[h] = jnp.broadcast_to(m_new, (SUBLANES, LANES))
            hv = hv * lax.rsqrt(jnp.mean(hv * hv, axis=-1, keepdims=True) + NORM_EPS)
            o_ref[0, rows, hs] = jax.nn.sigmoid(p_ref[0, rows, W + h * DH:W + (h + 1) * DH]) * hv * ng_ref[:, hs]
        return carry

    lax.fori_loop(0, n_chunks, chunk, 0)


def _block_diag(w):
    nb, d, e = w.shape
    eye = jnp.eye(nb, dtype=w.dtype)
    return (eye[:, None, :, None] * w[:, :, None, :]).reshape(nb * d, nb * e)


def mlstm_mixer(pd, conv_w, conv_b, qkv_w, b_if, norm_g):
    b, t, cols = pd.shape
    tt = min(MIX_TILE, t)
    W = MLSTM_WIDTH
    cw = jnp.zeros((SUBLANES, W), F32).at[:MLSTM_CONV].set(conv_w)
    bif = jnp.zeros((1, LANES), F32).at[0, :2 * MLSTM_HEADS].set(b_if)
    consts = [cw, conv_b.reshape(1, W)] + [_block_diag(qkv_w[i]).astype(BF16) for i in range(3)] + [
        bif, norm_g.reshape(1, W)]
    return pl.pallas_call(
        _mlstm_body,
        out_shape=jax.ShapeDtypeStruct((b, t, W), F32),
        grid=(b, t // tt),
        in_specs=[pl.BlockSpec((1, tt, cols), lambda i, j: (i, j, 0))] + [_const_spec(x.shape) for x in consts],
        out_specs=pl.BlockSpec((1, tt, W), lambda i, j: (i, j, 0)),
        scratch_shapes=[pltpu.VMEM((SUBLANES, W), F32),
                        pltpu.VMEM((MLSTM_HEADS, MLSTM_DH, MLSTM_DH), F32),
                        pltpu.VMEM((MLSTM_HEADS, SUBLANES, MLSTM_DH), F32),
                        pltpu.VMEM((MLSTM_HEADS, SUBLANES, LANES), F32),
                        pltpu.VMEM((tt, W), F32), pltpu.VMEM((tt, W), F32), pltpu.VMEM((tt, W), F32)],
        compiler_params=_params(("parallel", "arbitrary")),
        name="mlstm_mixer",
    )(pd, *consts)


def _mix_xattn_body(h_ref, ya_ref, yb_ref, wm_ref, g_ref, wq_ref, k_ref, v_ref, wo_ref, o_ref):
    mixed = jnp.concatenate([ya_ref[0], yb_ref[0]], axis=1).astype(BF16)
    h = h_ref[0] + jnp.dot(mixed, wm_ref[...], preferred_element_type=F32)
    xn = _rmsnorm(h, g_ref[...]).astype(BF16)
    q = jnp.dot(xn, wq_ref[...], preferred_element_type=F32)
    outs = []
    for hd in range(XA_HEADS):
        hs = slice(hd * XA_DH, (hd + 1) * XA_DH)
        s = _mm_nt(q[:, hs], k_ref[0, :, hs]) * XA_DH ** -0.5
        e = jnp.exp(s - jnp.max(s, axis=-1, keepdims=True))
        p = e / jnp.sum(e, axis=-1, keepdims=True)
        outs.append(_mm(p, v_ref[0, :, hs]))
    o = jnp.concatenate(outs, axis=1).astype(BF16)
    o_ref[0] = h + jnp.dot(o, wo_ref[...], preferred_element_type=F32)


def mix_xattn(h, ya, yb, w_mix, g, wq, kv, wo):
    b, t, d = h.shape
    tm = min(ROW_TILE, t)
    tok = lambda w: pl.BlockSpec((1, tm, w), lambda i, j: (i, j, 0))
    mem_k = pl.BlockSpec((1, MEM_LEN, d), lambda i, j: (i, 0, 0))
    mem_v = pl.BlockSpec((1, MEM_LEN, d), lambda i, j: (i, 0, 1))
    return pl.pallas_call(
        _mix_xattn_body,
        out_shape=jax.ShapeDtypeStruct((b, t, d), F32),
        grid=(b, t // tm),
        in_specs=[tok(d), tok(ya.shape[-1]), tok(yb.shape[-1]), _const_spec(w_mix.shape), _const_spec((1, d)),
                  _const_spec(wq.shape), mem_k, mem_v, _const_spec(wo.shape)],
        out_specs=tok(d),
        compiler_params=_params(("parallel", "parallel")),
        name="mix_xattn",
    )(h, ya, yb, w_mix, g.reshape(1, d), wq, kv, kv, wo)


def _ffn_body(final_norm, n_split, h_ref, g_ref, wu_ref, wg_ref, cw_ref, cb_ref, wd_ref, fg_ref, o_ref, uprev_ref):
    tm = h_ref.shape[1]
    fc = D_FF_PAD // n_split

    @pl.when(pl.program_id(1) == 0)
    def _():
        uprev_ref[...] = jnp.zeros_like(uprev_ref)

    h = h_ref[0]
    xn = _rmsnorm(h, g_ref[...]).astype(BF16)
    acc = h
    for s in range(n_split):
        cs = slice(s * fc, (s + 1) * fc)
        u = jnp.dot(xn, wu_ref[:, cs], preferred_element_type=F32)
        gt = jnp.dot(xn, wg_ref[:, cs], preferred_element_type=F32)
        uu = jnp.concatenate([uprev_ref[:, cs], u], axis=0)
        uprev_ref[:, cs] = u[tm - SUBLANES:tm, :]
        c = cb_ref[:, cs] + cw_ref[2:3, cs] * u
        c = c + cw_ref[1:2, cs] * pltpu.roll(uu, 1, 0)[SUBLANES:SUBLANES + tm, :]
        c = c + cw_ref[0:1, cs] * pltpu.roll(uu, 2, 0)[SUBLANES:SUBLANES + tm, :]
        act = (_silu(c) * gt).astype(BF16)
        acc = acc + jnp.dot(act, wd_ref[cs, :], preferred_element_type=F32)
    if final_norm:
        acc = _rmsnorm(acc, fg_ref[...])
    o_ref[0] = acc


def ffn(h, g, wu, wg, conv_w, conv_b, wd, final_g, final_norm):
    b, t, d = h.shape
    tm = min(ROW_TILE, t)
    tok = pl.BlockSpec((1, tm, d), lambda i, j: (i, j, 0))
    consts = [g.reshape(1, d), wu, wg, conv_w, conv_b, wd, final_g.reshape(1, d)]
    return pl.pallas_call(
        functools.partial(_ffn_body, final_norm, 2),
        out_shape=jax.ShapeDtypeStruct((b, t, d), F32),
        grid=(b, t // tm),
        in_specs=[tok] + [_const_spec(x.shape) for x in consts],
        out_specs=tok,
        scratch_shapes=[pltpu.VMEM((SUBLANES, D_FF_PAD), F32)],
        compiler_params=_params(("parallel", "arbitrary")),
        name="ffn",
    )(h, *consts)


def _pad_cols(w, n):
    return jnp.pad(w, ((0, 0), (0, n - w.shape[1])))


def _rwkv_perm(x):
    W = RWKV_WIDTH
    o_w, o_k, o_v = W, W + RWKV_LORA_W, 2 * W + RWKV_LORA_W
    o_a = 3 * W + RWKV_LORA_W
    o_g = o_a + RWKV_LORA_A
    return jnp.concatenate([x[..., :W], x[..., o_k:o_k + W], x[..., o_v:o_v + W], x[..., o_w:o_w + RWKV_LORA_W],
                            x[..., o_a:o_a + RWKV_LORA_A], x[..., o_g:]], axis=-1)


def kernel(x, mem, mem_norm_g, norm_mix_g, ab_w_in, gla_w_alpha2, gla_b_alpha, gla_norm_g, rwkv_mu, rwkv_w0, rwkv_w2, rwkv_a0, rwkv_a2, rwkv_g2, rwkv_k_k, rwkv_k_a, rwkv_r_k, rwkv_ln_g, rwkv_ln_b, cd_w_in, lru_conv_w, lru_conv_b, lru_gate_w, lru_gate_b, lru_lambda, mlstm_conv_w, mlstm_conv_b, mlstm_qkv_w, mlstm_b_if, mlstm_norm_g, w_mix_out, norm_xattn_g, xattn_wq, xattn_wkv, xattn_wo, norm_ffn_g, ffn_w_up, ffn_conv_w, ffn_conv_b, ffn_w_down, final_norm_g):
    b, t, d = x.shape
    depth = norm_mix_g.shape[0]
    n = b * t
    h = x
    mem2d = mem.reshape(b * MEM_LEN, d)
    for layer in range(depth):
        j = layer // 2
        h2d = h.reshape(n, d)
        if layer % 2 == 0:
            w = ab_w_in[j]
            wa = _pad_cols(w[:, :GLA_COLS], GLA_COLS_PAD).astype(BF16)
            wb = _rwkv_perm(w[:, GLA_COLS:]).astype(BF16)
            pa, pb = norm_matmul(h2d, norm_mix_g[layer], [wa, wb], name="proj_ab")
            ya = gla_mixer(pa.reshape(b, t, -1), gla_w_alpha2[j], gla_b_alpha[j], gla_norm_g[j])
            yb = rwkv_mixer(pb.reshape(b, t, -1), _rwkv_perm(rwkv_mu[j]), rwkv_w0[j], rwkv_w2[j], rwkv_a0[j],
                            rwkv_a2[j], rwkv_g2[j], rwkv_k_k[j], rwkv_k_a[j], rwkv_r_k[j], rwkv_ln_g[j],
                            rwkv_ln_b[j])
        else:
            w = cd_w_in[j]
            wc = w[:, :2 * LRU_WIDTH].astype(BF16)
            wd_ = _pad_cols(w[:, 2 * LRU_WIDTH:], MLSTM_COLS_PAD).astype(BF16)
            pc, pd = norm_matmul(h2d, norm_mix_g[layer], [wc, wd_], name="proj_cd")
            ya = rglru_mixer(pc.reshape(b, t, -1), lru_conv_w[j], lru_conv_b[j], lru_gate_w[j], lru_gate_b[j],
                             lru_lambda[j])
            yb = mlstm_mixer(pd.reshape(b, t, -1), mlstm_conv_w[j], mlstm_conv_b[j], mlstm_qkv_w[j],
                             mlstm_b_if[j], mlstm_norm_g[j])
        (kv,) = norm_matmul(mem2d, mem_norm_g, [xattn_wkv[layer].astype(BF16)], out_dtype=BF16, name="proj_kv")
        kv = kv.reshape(b, MEM_LEN, 2 * d)
        h = mix_xattn(h, ya, yb, w_mix_out[layer].astype(BF16), norm_xattn_g[layer], xattn_wq[layer].astype(BF16),
                      kv, xattn_wo[layer].astype(BF16))
        wup = ffn_w_up[layer]
        wu = _pad_cols(wup[:, :D_FF], D_FF_PAD).astype(BF16)
        wg = _pad_cols(wup[:, D_FF:], D_FF_PAD).astype(BF16)
        cw = _pad_cols(jnp.pad(ffn_conv_w[layer], ((0, SUBLANES - FFN_CONV), (0, 0))), D_FF_PAD)
        cb = _pad_cols(ffn_conv_b[layer].reshape(1, D_FF), D_FF_PAD)
        wdn = jnp.pad(ffn_w_down[layer], ((0, D_FF_PAD - D_FF), (0, 0))).astype(BF16)
        h = ffn(h, norm_ffn_g[layer], wu, wg, cw, cb, wdn, final_norm_g, layer == depth - 1)
    return h
```

```python
import functools

import jax
import jax.numpy as jnp
from jax import lax
from jax.experimental import pallas as pl
from jax.experimental.pallas import tpu as pltpu

F32 = jnp.float32
BF16 = jnp.bfloat16

D_MODEL = 1024
NORM_EPS = 1e-6
LANES = 128
SUBLANES = 8
VMEM_LIMIT_BYTES = 56 * 1024 * 1024

GLA_HEADS, GLA_DK, GLA_DV, GLA_RANK, GLA_TAU, GLA_CHUNK = 4, 64, 128, 16, 16.0, 64
GLA_COLS = 2 * GLA_HEADS * GLA_DK + 2 * GLA_HEADS * GLA_DV + GLA_RANK
GLA_COLS_PAD = 13 * LANES

RWKV_HEADS, RWKV_N, RWKV_WIDTH = 8, 64, 512
RWKV_LORA_W, RWKV_LORA_A, RWKV_LORA_G = 64, 64, 128
RWKV_CHUNK = 64
RWKV_DECAY_SCALE = 0.6065306597126334
RWKV_GN_EPS = RWKV_N * 1e-5
RWKV_L2_EPS = 1e-12
RWKV_COLS = 3 * RWKV_WIDTH + RWKV_LORA_W + RWKV_LORA_A + RWKV_LORA_G

LRU_WIDTH, LRU_BLOCKS, LRU_BLOCK, LRU_C, LRU_CONV = 512, 4, 128, 8.0, 4
MLSTM_HEADS, MLSTM_DH, MLSTM_WIDTH, MLSTM_CONV, MLSTM_CHUNK = 4, 128, 512, 4, 64
MLSTM_COLS_PAD = 2 * MLSTM_WIDTH + LANES

XA_HEADS, XA_DH, MEM_LEN = 4, 256, 256
D_FF, FFN_CONV = 2752, 3
D_FF_PAD = 22 * LANES

MIX_TILE = 256
ROW_TILE = 512


def _mm(a, b):
    return jnp.dot(a.astype(BF16), b.astype(BF16), preferred_element_type=F32)


def _mm_nt(a, b):
    return lax.dot_general(a.astype(BF16), b.astype(BF16), (((1,), (1,)), ((), ())), preferred_element_type=F32)


def _mm_tn(a, b):
    return lax.dot_general(a.astype(BF16), b.astype(BF16), (((0,), (0,)), ((), ())), preferred_element_type=F32)


def _split3(x):
    x1 = x.astype(BF16)
    r1 = x - x1.astype(F32)
    x2 = r1.astype(BF16)
    x3 = (r1 - x2.astype(F32)).astype(BF16)
    return x1, x2, x3


def _mm_exact_lhs01(m01, x):
    x1, x2, x3 = _split3(x)
    d = lambda y: jnp.dot(m01, y, preferred_element_type=F32)
    return d(x1) + d(x2) + d(x3)


def _mm_exact_rhs01(x, m01):
    x1, x2, x3 = _split3(x)
    d = lambda y: jnp.dot(y, m01, preferred_element_type=F32)
    return d(x1) + d(x2) + d(x3)


def _rmsnorm(x, g):
    return x * lax.rsqrt(jnp.mean(x * x, axis=-1, keepdims=True) + NORM_EPS) * g


def _log_sigmoid(x):
    return jnp.minimum(x, 0.0) - jnp.log1p(jnp.exp(-jnp.abs(x)))


def _softplus(x):
    return jnp.maximum(x, 0.0) + jnp.log1p(jnp.exp(-jnp.abs(x)))


def _silu(x):
    return x * jax.nn.sigmoid(x)


def _tri_masks(n):
    r = lax.broadcasted_iota(jnp.int32, (n, n), 0)
    c = lax.broadcasted_iota(jnp.int32, (n, n), 1)
    return r >= c, r > c


def _shift_rows(x, s, fill):
    if s == 0:
        return x
    rolled = pltpu.roll(x, s, 0)
    row = lax.broadcasted_iota(jnp.int32, x.shape, 0)
    return jnp.where(row >= s, rolled, fill)


def _causal_conv(x, halo, w_ref, b_ref, width):
    t = x.shape[0]
    xx = jnp.concatenate([halo, x], axis=0)
    y = b_ref[...] + w_ref[width - 1:width, :] * x
    for j in range(width - 1):
        s = width - 1 - j
        y = y + w_ref[j:j + 1, :] * pltpu.roll(xx, s, 0)[SUBLANES:SUBLANES + t, :]
    return y


def _const_spec(shape):
    nd = len(shape)
    return pl.BlockSpec(shape, lambda *_: (0,) * nd, pipeline_mode=pl.Buffered(1))


def _params(sem):
    return pltpu.CompilerParams(dimension_semantics=sem, vmem_limit_bytes=VMEM_LIMIT_BYTES)


def _norm_matmul_body(n_out, x_ref, g_ref, *refs):
    xn = _rmsnorm(x_ref[...], g_ref[...]).astype(BF16)
    for w_ref, o_ref in zip(refs[:n_out], refs[n_out:]):
        o_ref[...] = jnp.dot(xn, w_ref[...], preferred_element_type=F32).astype(o_ref.dtype)


def norm_matmul(x2d, g, ws, out_dtype=F32, name="norm_matmul"):
    n, d = x2d.shape
    tm = min(ROW_TILE, n)
    assert n % tm == 0
    return pl.pallas_call(
        functools.partial(_norm_matmul_body, len(ws)),
        out_shape=[jax.ShapeDtypeStruct((n, w.shape[1]), out_dtype) for w in ws],
        grid=(n // tm,),
        in_specs=[pl.BlockSpec((tm, d), lambda i: (i, 0)), _const_spec((1, d))]
        + [_const_spec(w.shape) for w in ws],
        out_specs=[pl.BlockSpec((tm, w.shape[1]), lambda i: (i, 0)) for w in ws],
        compiler_params=_params(("parallel",)),
        name=name,
    )(x2d, g.reshape(1, d), *ws)


def _gla_body(p_ref, wa_ref, ba_ref, ng_ref, o_ref, st_ref):
    L = GLA_CHUNK
    n_chunks = p_ref.shape[1] // L
    hk = GLA_HEADS * GLA_DK
    hv = GLA_HEADS * GLA_DV

    @pl.when(pl.program_id(1) == 0)
    def _():
        st_ref[...] = jnp.zeros_like(st_ref)

    def chunk(c, carry):
        incl, _ = _tri_masks(L)
        tri01 = jnp.where(incl, 1.0, 0.0).astype(BF16)
        t0 = pl.multiple_of(c * L, L)
        rows = pl.ds(t0, L)
        q = p_ref[0, rows, 0:hk]
        k = p_ref[0, rows, hk:2 * hk]
        v = p_ref[0, rows, 2 * hk:2 * hk + hv]
        gt = p_ref[0, rows, 2 * hk + hv:2 * hk + 2 * hv]
        a_lr = p_ref[0, rows, 2 * hk + 2 * hv:GLA_COLS_PAD]
        la = _log_sigmoid(_mm(a_lr, wa_ref[...]) + ba_ref[...]) * (1.0 / GLA_TAU)
        g = _mm_exact_lhs01(tri01, la)
        g_last = g[L - 1:L, :]
        q_dec = (q * GLA_DK ** -0.5) * jnp.exp(g)
        k_inv = k * jnp.exp(-g)
        k_end = k * jnp.exp(g_last - g)
        sd = jnp.exp(g_last)
        H = range(GLA_HEADS)
        ks = [slice(h * GLA_DK, (h + 1) * GLA_DK) for h in H]
        vs = [slice(h * GLA_DV, (h + 1) * GLA_DV) for h in H]
        qd = [q_dec[:, ks[h]] for h in H]
        vh = [v[:, vs[h]] for h in H]
        st = [st_ref[h] for h in H]
        sc = [jnp.where(incl, _mm_nt(qd[h], k_inv[:, ks[h]]), 0.0) for h in H]
        o_state = [_mm_nt(qd[h], st[h]) for h in H]
        upd = [_mm_tn(vh[h], k_end[:, ks[h]]) for h in H]
        o_loc = [_mm(sc[h], vh[h]) for h in H]
        for h in H:
            st_ref[h] = st[h] * sd[:, ks[h]] + upd[h]
            o = o_loc[h] + o_state[h]
            o = o * lax.rsqrt(jnp.mean(o * o, axis=-1, keepdims=True) + NORM_EPS)
            o_ref[0, rows, vs[h]] = o * ng_ref[:, vs[h]] * _silu(gt[:, vs[h]])
        return carry

    lax.fori_loop(0, n_chunks, chunk, 0)


def gla_mixer(pa, w_alpha2, b_alpha, norm_g):
    b, t, _ = pa.shape
    tt = min(MIX_TILE, t)
    hk, hv = GLA_HEADS * GLA_DK, GLA_HEADS * GLA_DV
    wa = jnp.zeros((LANES, hk), BF16).at[:GLA_RANK].set(w_alpha2.astype(BF16))
    return pl.pallas_call(
        _gla_body,
        out_shape=jax.ShapeDtypeStruct((b, t, hv), F32),
        grid=(b, t // tt),
        in_specs=[pl.BlockSpec((1, tt, GLA_COLS_PAD), lambda i, j: (i, j, 0)),
                  _const_spec((LANES, hk)), _const_spec((1, hk)), _const_spec((1, hv))],
        out_specs=pl.BlockSpec((1, tt, hv), lambda i, j: (i, j, 0)),
        scratch_shapes=[pltpu.VMEM((GLA_HEADS, GLA_DV, GLA_DK), F32)],
        compiler_params=_params(("parallel", "arbitrary")),
        name="gla_mixer",
    )(pa, wa, b_alpha.reshape(1, hk), norm_g.reshape(1, hv))


def _rwkv_body(p_ref, mu_ref, w0_ref, w2_ref, a0_ref, a2_ref, g2_ref, kk_ref, ka_ref, rk_ref, lng_ref, lnb_ref,
               seg_ref, o_ref, s_ref, prev_ref):
    L = RWKV_CHUNK
    W = RWKV_WIDTH
    N = RWKV_N
    n_chunks = p_ref.shape[1] // L

    @pl.when(pl.program_id(1) == 0)
    def _():
        s_ref[...] = jnp.zeros_like(s_ref)
        prev_ref[...] = jnp.zeros_like(prev_ref)

    def seg_sum(x):
        return _mm_exact_rhs01(x, seg_ref[...])

    def chunk(c, carry):
        incl, strict = _tri_masks(L)
        tri01 = jnp.where(incl, 1.0, 0.0).astype(BF16)
        t0 = pl.multiple_of(c * L, L)
        rows = pl.ds(t0, L)
        p = p_ref[0, rows, :]
        row = lax.broadcasted_iota(jnp.int32, p.shape, 0)
        sh = jnp.where(row == 0, prev_ref[0:1, :], pltpu.roll(p, 1, 0))
        prev_ref[0:1, :] = p[L - 1:L, :]
        pf = p + (sh - p) * mu_ref[...]
        r = pf[:, 0:W]
        k = pf[:, W:2 * W]
        v = pf[:, 2 * W:3 * W]
        wa = pf[:, 3 * W:3 * W + LANES]
        g_lr = pf[:, 3 * W + LANES:3 * W + 2 * LANES]
        log_w = -RWKV_DECAY_SCALE * jax.nn.sigmoid(w0_ref[...] + _mm(jnp.tanh(wa), w2_ref[...]))
        a = jax.nn.sigmoid(a0_ref[...] + _mm(wa, a2_ref[...]))
        gate = _mm(jax.nn.sigmoid(g_lr), g2_ref[...])
        kk = k * kk_ref[...]
        kk = kk * lax.rsqrt(seg_sum(kk * kk) + RWKV_L2_EPS)
        k = k * (1.0 + (a - 1.0) * ka_ref[...])
        b_vec = kk * a
        g = _mm_exact_lhs01(tri01, log_w)
        g_last = g[L - 1:L, :]
        e_neg = jnp.exp(-g)
        e_end = jnp.exp(g_last - g)
        a_dec = -kk * jnp.exp(g - log_w)
        r_dec = r * jnp.exp(g)
        b_inv = b_vec * e_neg
        k_inv = k * e_neg
        b_end = b_vec * e_end
        k_end = k * e_end
        sd = jnp.exp(g_last)
        H = range(RWKV_HEADS)
        hsl = [slice(h * N, (h + 1) * N) for h in H]
        ad, rd, bi, ki, be, ke, vh = ([x[:, hs] for hs in hsl] for x in (a_dec, r_dec, b_inv, k_inv, b_end, k_end, v))
        a_ab = [jnp.where(strict, _mm_nt(ad[h], bi[h]), 0.0) for h in H]
        a_ak = [jnp.where(strict, _mm_nt(ad[h], ki[h]), 0.0) for h in H]
        a_rb = [jnp.where(incl, _mm_nt(rd[h], bi[h]), 0.0) for h in H]
        a_rk = [jnp.where(incl, _mm_nt(rd[h], ki[h]), 0.0) for h in H]
        x = [jnp.concatenate([ad[h], _mm(a_ak[h], vh[h])], axis=1) for h in H]
        pw = a_ab
        for i in range(6):
            x = [x[h] + _mm(pw[h], x[h]) for h in H]
            if i < 5:
                pw = [_mm(pw[h], pw[h]) for h in H]
        ry = [_mm(a_rb[h], x[h]) for h in H]
        rm = [rd[h] + ry[h][:, 0:N] for h in H]
        y0 = [ry[h][:, N:2 * N] + _mm(a_rk[h], vh[h]) for h in H]
        mc = [_mm_tn(x[h], be[h]) for h in H]
        cc = [mc[h][N:2 * N, :] + _mm_tn(vh[h], ke[h]) for h in H]
        s = [s_ref[h] for h in H]
        ys = [_mm_nt(rm[h], s[h]) + y0[h] for h in H]
        for h in H:
            s_ref[h] = s[h] * sd[:, hsl[h]] + _mm(s[h], mc[h][0:N, :]) + cc[h]
        y = jnp.concatenate(ys, axis=1)
        mean = seg_sum(y) * (1.0 / N)
        dy = y - mean
        var = seg_sum(dy * dy) * (1.0 / N)
        y = dy * lax.rsqrt(var + RWKV_GN_EPS) * lng_ref[...] + lnb_ref[...]
        y = y + seg_sum(r * k * rk_ref[...]) * v
        o_ref[0, rows, :] = y * gate
        return carry

    lax.fori_loop(0, n_chunks, chunk, 0)


def rwkv_mixer(pb, mu, w0, w2, a0, a2, g2, k_k, k_a, r_k, ln_g, ln_b):
    b, t, cols = pb.shape
    tt = min(MIX_TILE, t)
    W = RWKV_WIDTH
    row = lambda x: x.reshape(1, -1)
    w2p = jnp.zeros((LANES, W), BF16).at[:RWKV_LORA_W].set(w2.astype(BF16))
    a2p = jnp.zeros((LANES, W), BF16).at[RWKV_LORA_W:].set(a2.astype(BF16))
    head_of = jnp.arange(W) // RWKV_N
    seg = (head_of[:, None] == head_of[None, :]).astype(BF16)
    consts = [row(mu), row(w0), w2p, row(a0), a2p, g2.astype(BF16), row(k_k), row(k_a), row(r_k), row(ln_g),
              row(ln_b), seg]
    return pl.pallas_call(
        _rwkv_body,
        out_shape=jax.ShapeDtypeStruct((b, t, W), F32),
        grid=(b, t // tt),
        in_specs=[pl.BlockSpec((1, tt, cols), lambda i, j: (i, j, 0))] + [_const_spec(x.shape) for x in consts],
        out_specs=pl.BlockSpec((1, tt, W), lambda i, j: (i, j, 0)),
        scratch_shapes=[pltpu.VMEM((RWKV_HEADS, RWKV_N, RWKV_N), F32), pltpu.VMEM((SUBLANES, cols), F32)],
        compiler_params=_params(("parallel", "arbitrary")),
        name="rwkv_mixer",
    )(pb, *consts)


def _rglru_body(p_ref, cw_ref, cb_ref, gw_ref, gb_ref, lam_ref, o_ref, xprev_ref, hprev_ref):
    tt = p_ref.shape[1]
    W = LRU_WIDTH

    @pl.when(pl.program_id(1) == 0)
    def _():
        xprev_ref[...] = jnp.zeros_like(xprev_ref)
        hprev_ref[...] = jnp.zeros_like(hprev_ref)

    x = p_ref[0, :, 0:W]
    gate = p_ref[0, :, W:2 * W]
    xc = _causal_conv(x, xprev_ref[...], cw_ref, cb_ref, LRU_CONV)
    xprev_ref[...] = x[tt - SUBLANES:tt, :]
    xcb = xc.astype(BF16)
    pre = []
    for gi in range(2):
        pre.append(jnp.concatenate(
            [jnp.dot(xcb[:, n * LRU_BLOCK:(n + 1) * LRU_BLOCK], gw_ref[gi, n], preferred_element_type=F32)
             for n in range(LRU_BLOCKS)], axis=1) + gb_ref[gi:gi + 1, :])
    r_gate = jax.nn.sigmoid(pre[0])
    i_gate = jax.nn.sigmoid(pre[1])
    log_a = -LRU_C * r_gate * _softplus(-lam_ref[...])
    a = jnp.exp(log_a)
    u = jnp.sqrt(-jnp.tanh(log_a) * (a * a + 1.0)) * (i_gate * xc)
    d = 1
    while d < tt:
        u = u + a * _shift_rows(u, d, 0.0)
        a = a * _shift_rows(a, d, 1.0)
        d *= 2
    h = u + a * hprev_ref[0:1, :]
    hprev_ref[0:1, :] = h[tt - 1:tt, :]
    o_ref[0] = h * jax.nn.gelu(gate)


def rglru_mixer(pc, conv_w, conv_b, gate_w, gate_b, lam):
    b, t, cols = pc.shape
    tt = min(MIX_TILE, t)
    W = LRU_WIDTH
    cw = jnp.zeros((SUBLANES, W), F32).at[:LRU_CONV].set(conv_w)
    consts = [cw, conv_b.reshape(1, W), gate_w.astype(BF16), gate_b, lam.reshape(1, W)]
    return pl.pallas_call(
        _rglru_body,
        out_shape=jax.ShapeDtypeStruct((b, t, W), F32),
        grid=(b, t // tt),
        in_specs=[pl.BlockSpec((1, tt, cols), lambda i, j: (i, j, 0))] + [_const_spec(x.shape) for x in consts],
        out_specs=pl.BlockSpec((1, tt, W), lambda i, j: (i, j, 0)),
        scratch_shapes=[pltpu.VMEM((SUBLANES, W), F32), pltpu.VMEM((SUBLANES, W), F32)],
        compiler_params=_params(("parallel", "arbitrary")),
        name="rglru_mixer",
    )(pc, *consts)


def _mlstm_body(p_ref, cw_ref, cb_ref, wq_ref, wk_ref, wv_ref, bif_ref, ng_ref, o_ref,
                xprev_ref, c_ref, n_ref, m_ref, q_s, k_s, v_s):
    tt = p_ref.shape[1]
    L = MLSTM_CHUNK
    W = MLSTM_WIDTH
    DH = MLSTM_DH
    H = MLSTM_HEADS
    n_chunks = tt // L

    @pl.when(pl.program_id(1) == 0)
    def _():
        xprev_ref[...] = jnp.zeros_like(xprev_ref)
        c_ref[...] = jnp.zeros_like(c_ref)
        n_ref[...] = jnp.zeros_like(n_ref)
        m_ref[...] = jnp.zeros_like(m_ref)

    x = p_ref[0, :, 0:W]
    xc = _silu(_causal_conv(x, xprev_ref[...], cw_ref, cb_ref, MLSTM_CONV))
    xprev_ref[...] = x[tt - SUBLANES:tt, :]
    xcb = xc.astype(BF16)
    q_s[...] = jnp.dot(xcb, wq_ref[...], preferred_element_type=F32)
    k_s[...] = jnp.dot(xcb, wk_ref[...], preferred_element_type=F32) * DH ** -0.5
    v_s[...] = jnp.dot(x.astype(BF16), wv_ref[...], preferred_element_type=F32)

    def chunk(c, carry):
        incl, _ = _tri_masks(L)
        tri01 = jnp.where(incl, 1.0, 0.0).astype(BF16)
        eye = jnp.where(lax.broadcasted_iota(jnp.int32, (L, L), 0) == lax.broadcasted_iota(jnp.int32, (L, L), 1),
                        1.0, 0.0)

        def to_row(col):
            return jnp.sum(col * eye, axis=0, keepdims=True)

        t0 = pl.multiple_of(c * L, L)
        rows = pl.ds(t0, L)
        gates = p_ref[0, rows, 2 * W:2 * W + LANES] + bif_ref[...]
        bcum = _mm_exact_lhs01(tri01, _log_sigmoid(gates))
        HS = range(H)
        hs = [slice(h * DH, (h + 1) * DH) for h in HS]
        qh = [q_s[rows, hs[h]] for h in HS]
        kh = [k_s[rows, hs[h]] for h in HS]
        vh = [v_s[rows, hs[h]] for h in HS]
        cm = [c_ref[h] for h in HS]
        nv = [n_ref[h, 0:1, :] for h in HS]
        m = [m_ref[h, 0:1, 0:1] for h in HS]
        i_col = [gates[:, h:h + 1] for h in HS]
        b_col = [bcum[:, H + h:H + h + 1] for h in HS]
        b_last = [b_col[h][L - 1:L, :] for h in HS]
        qk = [_mm_nt(qh[h], kh[h]) for h in HS]
        q_c = [_mm(qh[h], cm[h]) for h in HS]
        log_e = [b_last[h] - b_col[h] + i_col[h] for h in HS]
        m_end = [jnp.max(log_e[h], axis=0, keepdims=True) for h in HS]
        m_new = [jnp.maximum(b_last[h] + m[h], m_end[h]) for h in HS]
        kin = [kh[h] * (jnp.exp(m_end[h] - m_new[h]) * jnp.exp(log_e[h] - m_end[h])) for h in HS]
        c_upd = [_mm_tn(kin[h], vh[h]) for h in HS]
        log_d = [jnp.where(incl, b_col[h] - to_row(b_col[h]) + to_row(i_col[h]), -jnp.inf) for h in HS]
        m_loc = [jnp.max(log_d[h], axis=1, keepdims=True) for h in HS]
        w_loc = [jnp.exp(log_d[h] - m_loc[h]) * qk[h] for h in HS]
        num_loc = [_mm(w_loc[h], vh[h]) for h in HS]
        for h in HS:
            c_state = jnp.exp(b_last[h] + m[h] - m_new[h])
            c_ref[h] = c_state * cm[h] + c_upd[h]
            n_ref[h, 0:1, :] = c_state * nv[h] + jnp.sum(kin[h], axis=0, keepdims=True)
            m_ref[h] = jnp.broadcast_to(m_new[h], (SUBLANES, LANES))
            den_loc = jnp.sum(w_loc[h], axis=1, keepdims=True)
            m_t = jnp.maximum(b_col[h] + m[h], m_loc[h])
            s_state = jnp.exp(b_col[h] + m[h] - m_t)
            s_loc = jnp.exp(m_loc[h] - m_t)
            num = s_state * q_c[h] + s_loc * num_loc[h]
            den = s_state * jnp.sum(qh[h] * nv[h], axis=1, keepdims=True) + s_loc * den_loc
            hv = num / jnp.maximum(jnp.abs(den), jnp.exp(-m_t))
            hv = hv * lax.rsqrt(jnp.mean(hv * hv, axis=-1, keepdims=True) + NORM_EPS)
            o_ref[0, rows, hs[h]] = (jax.nn.sigmoid(p_ref[0, rows, W + h * DH:W + (h + 1) * DH]) * hv
                                     * ng_ref[:, hs[h]])
        return carry

    lax.fori_loop(0, n_chunks, chunk, 0)


def _block_diag(w):
    nb, d, e = w.shape
    eye = jnp.eye(nb, dtype=w.dtype)
    return (eye[:, None, :, None] * w[:, :, None, :]).reshape(nb * d, nb * e)


def mlstm_mixer(pd, conv_w, conv_b, qkv_w, b_if, norm_g):
    b, t, cols = pd.shape
    tt = min(MIX_TILE, t)
    W = MLSTM_WIDTH
    cw = jnp.zeros((SUBLANES, W), F32).at[:MLSTM_CONV].set(conv_w)
    bif = jnp.zeros((1, LANES), F32).at[0, :2 * MLSTM_HEADS].set(b_if)
    consts = [cw, conv_b.reshape(1, W)] + [_block_diag(qkv_w[i]).astype(BF16) for i in range(3)] + [
        bif, norm_g.reshape(1, W)]
    return pl.pallas_call(
        _mlstm_body,
        out_shape=jax.ShapeDtypeStruct((b, t, W), F32),
        grid=(b, t // tt),
        in_specs=[pl.BlockSpec((1, tt, cols), lambda i, j: (i, j, 0))] + [_const_spec(x.shape) for x in consts],
        out_specs=pl.BlockSpec((1, tt, W), lambda i, j: (i, j, 0)),
        scratch_shapes=[pltpu.VMEM((SUBLANES, W), F32),
                        pltpu.VMEM((MLSTM_HEADS, MLSTM_DH, MLSTM_DH), F32),
                        pltpu.VMEM((MLSTM_HEADS, SUBLANES, MLSTM_DH), F32),
                        pltpu.VMEM((MLSTM_HEADS, SUBLANES, LANES), F32),
                        pltpu.VMEM((tt, W), F32), pltpu.VMEM((tt, W), F32), pltpu.VMEM((tt, W), F32)],
        compiler_params=_params(("parallel", "arbitrary")),
        name="mlstm_mixer",
    )(pd, *consts)


def _mix_xattn_body(h_ref, ya_ref, yb_ref, wm_ref, g_ref, wq_ref, k_ref, v_ref, wo_ref, o_ref):
    mixed = jnp.concatenate([ya_ref[0], yb_ref[0]], axis=1).astype(BF16)
    h = h_ref[0] + jnp.dot(mixed, wm_ref[...], preferred_element_type=F32)
    xn = _rmsnorm(h, g_ref[...]).astype(BF16)
    q = jnp.dot(xn, wq_ref[...], preferred_element_type=F32)
    outs = []
    for hd in range(XA_HEADS):
        hs = slice(hd * XA_DH, (hd + 1) * XA_DH)
        s = _mm_nt(q[:, hs], k_ref[0, :, hs]) * XA_DH ** -0.5
        e = jnp.exp(s - jnp.max(s, axis=-1, keepdims=True))
        p = e / jnp.sum(e, axis=-1, keepdims=True)
        outs.append(_mm(p, v_ref[0, :, hs]))
    o = jnp.concatenate(outs, axis=1).astype(BF16)
    o_ref[0] = h + jnp.dot(o, wo_ref[...], preferred_element_type=F32)


def mix_xattn(h, ya, yb, w_mix, g, wq, kv, wo):
    b, t, d = h.shape
    tm = min(ROW_TILE, t)
    tok = lambda w: pl.BlockSpec((1, tm, w), lambda i, j: (i, j, 0))
    mem_k = pl.BlockSpec((1, MEM_LEN, d), lambda i, j: (i, 0, 0))
    mem_v = pl.BlockSpec((1, MEM_LEN, d), lambda i, j: (i, 0, 1))
    return pl.pallas_call(
        _mix_xattn_body,
        out_shape=jax.ShapeDtypeStruct((b, t, d), F32),
        grid=(b, t // tm),
        in_specs=[tok(d), tok(ya.shape[-1]), tok(yb.shape[-1]), _const_spec(w_mix.shape), _const_spec((1, d)),
                  _const_spec(wq.shape), mem_k, mem_v, _const_spec(wo.shape)],
        out_specs=tok(d),
        compiler_params=_params(("parallel", "parallel")),
        name="mix_xattn",
    )(h, ya, yb, w_mix, g.reshape(1, d), wq, kv, kv, wo)


def _ffn_body(final_norm, n_split, h_ref, g_ref, wu_ref, wg_ref, cw_ref, cb_ref, wd_ref, fg_ref, o_ref, uprev_ref):
    tm = h_ref.shape[1]
    fc = D_FF_PAD // n_split

    @pl.when(pl.program_id(1) == 0)
    def _():
        uprev_ref[...] = jnp.zeros_like(uprev_ref)

    h = h_ref[0]
    xn = _rmsnorm(h, g_ref[...]).astype(BF16)
    acc = h
    for s in range(n_split):
        cs = slice(s * fc, (s + 1) * fc)
        u = jnp.dot(xn, wu_ref[:, cs], preferred_element_type=F32)
        gt = jnp.dot(xn, wg_ref[:, cs], preferred_element_type=F32)
        uu = jnp.concatenate([uprev_ref[:, cs], u], axis=0)
        uprev_ref[:, cs] = u[tm - SUBLANES:tm, :]
        c = cb_ref[:, cs] + cw_ref[2:3, cs] * u
        c = c + cw_ref[1:2, cs] * pltpu.roll(uu, 1, 0)[SUBLANES:SUBLANES + tm, :]
        c = c + cw_ref[0:1, cs] * pltpu.roll(uu, 2, 0)[SUBLANES:SUBLANES + tm, :]
        act = (_silu(c) * gt).astype(BF16)
        acc = acc + jnp.dot(act, wd_ref[cs, :], preferred_element_type=F32)
    if final_norm:
        acc = _rmsnorm(acc, fg_ref[...])
    o_ref[0] = acc


def ffn(h, g, wu, wg, conv_w, conv_b, wd, final_g, final_norm):
    b, t, d = h.shape
    tm = min(ROW_TILE, t)
    tok = pl.BlockSpec((1, tm, d), lambda i, j: (i, j, 0))
    consts = [g.reshape(1, d), wu, wg, conv_w, conv_b, wd, final_g.reshape(1, d)]
    return pl.pallas_call(
        functools.partial(_ffn_body, final_norm, 2),
        out_shape=jax.ShapeDtypeStruct((b, t, d), F32),
        grid=(b, t // tm),
        in_specs=[tok] + [_const_spec(x.shape) for x in consts],
        out_specs=tok,
        scratch_shapes=[pltpu.VMEM((SUBLANES, D_FF_PAD), F32)],
        compiler_params=_params(("parallel", "arbitrary")),
        name="ffn",
    )(h, *consts)


def _pad_cols(w, n):
    return jnp.pad(w, ((0, 0), (0, n - w.shape[1])))


def _rwkv_perm(x):
    W = RWKV_WIDTH
    o_w, o_k, o_v = W, W + RWKV_LORA_W, 2 * W + RWKV_LORA_W
    o_a = 3 * W + RWKV_LORA_W
    o_g = o_a + RWKV_LORA_A
    return jnp.concatenate([x[..., :W], x[..., o_k:o_k + W], x[..., o_v:o_v + W], x[..., o_w:o_w + RWKV_LORA_W],
                            x[..., o_a:o_a + RWKV_LORA_A], x[..., o_g:]], axis=-1)


def kernel(x, mem, mem_norm_g, norm_mix_g, ab_w_in, gla_w_alpha2, gla_b_alpha, gla_norm_g, rwkv_mu, rwkv_w0, rwkv_w2, rwkv_a0, rwkv_a2, rwkv_g2, rwkv_k_k, rwkv_k_a, rwkv_r_k, rwkv_ln_g, rwkv_ln_b, cd_w_in, lru_conv_w, lru_conv_b, lru_gate_w, lru_gate_b, lru_lambda, mlstm_conv_w, mlstm_conv_b, mlstm_qkv_w, mlstm_b_if, mlstm_norm_g, w_mix_out, norm_xattn_g, xattn_wq, xattn_wkv, xattn_wo, norm_ffn_g, ffn_w_up, ffn_conv_w, ffn_conv_b, ffn_w_down, final_norm_g):
    b, t, d = x.shape
    depth = norm_mix_g.shape[0]
    n = b * t
    h = x
    mem2d = mem.reshape(b * MEM_LEN, d)
    for layer in range(depth):
        j = layer // 2
        h2d = h.reshape(n, d)
        if layer % 2 == 0:
            w = ab_w_in[j]
            wa = _pad_cols(w[:, :GLA_COLS], GLA_COLS_PAD).astype(BF16)
            wb = _rwkv_perm(w[:, GLA_COLS:]).astype(BF16)
            pa, pb = norm_matmul(h2d, norm_mix_g[layer], [wa, wb], name="proj_ab")
            ya = gla_mixer(pa.reshape(b, t, -1), gla_w_alpha2[j], gla_b_alpha[j], gla_norm_g[j])
            yb = rwkv_mixer(pb.reshape(b, t, -1), _rwkv_perm(rwkv_mu[j]), rwkv_w0[j], rwkv_w2[j], rwkv_a0[j],
                            rwkv_a2[j], rwkv_g2[j], rwkv_k_k[j], rwkv_k_a[j], rwkv_r_k[j], rwkv_ln_g[j],
                            rwkv_ln_b[j])
        else:
            w = cd_w_in[j]
            wc = w[:, :2 * LRU_WIDTH].astype(BF16)
            wd_ = _pad_cols(w[:, 2 * LRU_WIDTH:], MLSTM_COLS_PAD).astype(BF16)
            pc, pd = norm_matmul(h2d, norm_mix_g[layer], [wc, wd_], name="proj_cd")
            ya = rglru_mixer(pc.reshape(b, t, -1), lru_conv_w[j], lru_conv_b[j], lru_gate_w[j], lru_gate_b[j],
                             lru_lambda[j])
            yb = mlstm_mixer(pd.reshape(b, t, -1), mlstm_conv_w[j], mlstm_conv_b[j], mlstm_qkv_w[j],
                             mlstm_b_if[j], mlstm_norm_g[j])
        (kv,) = norm_matmul(mem2d, mem_norm_g, [xattn_wkv[layer].astype(BF16)], out_dtype=BF16, name="proj_kv")
        kv = kv.reshape(b, MEM_LEN, 2 * d)
        h = mix_xattn(h, ya, yb, w_mix_out[layer].astype(BF16), norm_xattn_g[layer], xattn_wq[layer].astype(BF16),
                      kv, xattn_wo[layer].astype(BF16))
        wup = ffn_w_up[layer]
        wu = _pad_cols(wup[:, :D_FF], D_FF_PAD).astype(BF16)
        wg = _pad_cols(wup[:, D_FF:], D_FF_PAD).astype(BF16)
        cw = _pad_cols(jnp.pad(ffn_conv_w[layer], ((0, SUBLANES - FFN_CONV), (0, 0))), D_FF_PAD)
        cb = _pad_cols(ffn_conv_b[layer].reshape(1, D_FF), D_FF_PAD)
        wdn = jnp.pad(ffn_w_down[layer], ((0, D_FF_PAD - D_FF), (0, 0))).astype(BF16)
        h = ffn(h, norm_ffn_g[layer], wu, wg, cw, cb, wdn, final_norm_g, layer == depth - 1)
    return h
```

```python
import functools

import jax
import jax.numpy as jnp
from jax import lax
from jax.experimental import pallas as pl
from jax.experimental.pallas import tpu as pltpu

F32 = jnp.float32
BF16 = jnp.bfloat16

D_MODEL = 1024
NORM_EPS = 1e-6
LANES = 128
SUBLANES = 8
VMEM_LIMIT_BYTES = 56 * 1024 * 1024

GLA_HEADS, GLA_DK, GLA_DV, GLA_RANK, GLA_TAU, GLA_CHUNK = 4, 64, 128, 16, 16.0, 64
GLA_COLS = 2 * GLA_HEADS * GLA_DK + 2 * GLA_HEADS * GLA_DV + GLA_RANK
GLA_COLS_PAD = 13 * LANES

RWKV_HEADS, RWKV_N, RWKV_WIDTH = 8, 64, 512
RWKV_LORA_W, RWKV_LORA_A, RWKV_LORA_G = 64, 64, 128
RWKV_CHUNK = 64
RWKV_DECAY_SCALE = 0.6065306597126334
RWKV_GN_EPS = RWKV_N * 1e-5
RWKV_L2_EPS = 1e-12
RWKV_COLS = 3 * RWKV_WIDTH + RWKV_LORA_W + RWKV_LORA_A + RWKV_LORA_G

LRU_WIDTH, LRU_BLOCKS, LRU_BLOCK, LRU_C, LRU_CONV = 512, 4, 128, 8.0, 4
MLSTM_HEADS, MLSTM_DH, MLSTM_WIDTH, MLSTM_CONV, MLSTM_CHUNK = 4, 128, 512, 4, 64
MLSTM_COLS_PAD = 2 * MLSTM_WIDTH + LANES

XA_HEADS, XA_DH, MEM_LEN = 4, 256, 256
D_FF, FFN_CONV = 2752, 3
D_FF_PAD = 22 * LANES

MIX_TILE = 256
RWKV_TILE = 512
RWKV_BATCH_PER_STEP = 2
ROW_TILE = 512


def _mm(a, b):
    return jnp.dot(a.astype(BF16), b.astype(BF16), preferred_element_type=F32)


def _mm_nt(a, b):
    return lax.dot_general(a.astype(BF16), b.astype(BF16), (((1,), (1,)), ((), ())), preferred_element_type=F32)


def _mm_tn(a, b):
    return lax.dot_general(a.astype(BF16), b.astype(BF16), (((0,), (0,)), ((), ())), preferred_element_type=F32)


def _split3(x):
    x1 = x.astype(BF16)
    r1 = x - x1.astype(F32)
    x2 = r1.astype(BF16)
    x3 = (r1 - x2.astype(F32)).astype(BF16)
    return x1, x2, x3


def _mm_exact_lhs01(m01, x):
    x1, x2, x3 = _split3(x)
    d = lambda y: jnp.dot(m01, y, preferred_element_type=F32)
    return d(x1) + d(x2) + d(x3)


def _mm_exact_rhs01(x, m01):
    x1, x2, x3 = _split3(x)
    d = lambda y: jnp.dot(y, m01, preferred_element_type=F32)
    return d(x1) + d(x2) + d(x3)


def _rmsnorm(x, g):
    return x * lax.rsqrt(jnp.mean(x * x, axis=-1, keepdims=True) + NORM_EPS) * g


def _log_sigmoid(x):
    return jnp.minimum(x, 0.0) - jnp.log1p(jnp.exp(-jnp.abs(x)))


def _softplus(x):
    return jnp.maximum(x, 0.0) + jnp.log1p(jnp.exp(-jnp.abs(x)))


def _silu(x):
    return x * jax.nn.sigmoid(x)


def _tri_masks(n):
    r = lax.broadcasted_iota(jnp.int32, (n, n), 0)
    c = lax.broadcasted_iota(jnp.int32, (n, n), 1)
    return r >= c, r > c


def _shift_rows(x, s, fill):
    if s == 0:
        return x
    rolled = pltpu.roll(x, s, 0)
    row = lax.broadcasted_iota(jnp.int32, x.shape, 0)
    return jnp.where(row >= s, rolled, fill)


def _causal_conv(x, halo, w_ref, b_ref, width):
    t = x.shape[0]
    xx = jnp.concatenate([halo, x], axis=0)
    y = b_ref[...] + w_ref[width - 1:width, :] * x
    for j in range(width - 1):
        s = width - 1 - j
        y = y + w_ref[j:j + 1, :] * pltpu.roll(xx, s, 0)[SUBLANES:SUBLANES + t, :]
    return y


def _const_spec(shape):
    nd = len(shape)
    return pl.BlockSpec(shape, lambda *_: (0,) * nd, pipeline_mode=pl.Buffered(1))


def _params(sem):
    return pltpu.CompilerParams(dimension_semantics=sem, vmem_limit_bytes=VMEM_LIMIT_BYTES)


def _norm_matmul_body(n_out, x_ref, g_ref, *refs):
    xn = _rmsnorm(x_ref[...], g_ref[...]).astype(BF16)
    for w_ref, o_ref in zip(refs[:n_out], refs[n_out:]):
        o_ref[...] = jnp.dot(xn, w_ref[...], preferred_element_type=F32).astype(o_ref.dtype)


def norm_matmul(x2d, g, ws, out_dtype=F32, name="norm_matmul"):
    n, d = x2d.shape
    tm = min(ROW_TILE, n)
    assert n % tm == 0
    return pl.pallas_call(
        functools.partial(_norm_matmul_body, len(ws)),
        out_shape=[jax.ShapeDtypeStruct((n, w.shape[1]), out_dtype) for w in ws],
        grid=(n // tm,),
        in_specs=[pl.BlockSpec((tm, d), lambda i: (i, 0)), _const_spec((1, d))]
        + [_const_spec(w.shape) for w in ws],
        out_specs=[pl.BlockSpec((tm, w.shape[1]), lambda i: (i, 0)) for w in ws],
        compiler_params=_params(("parallel",)),
        name=name,
    )(x2d, g.reshape(1, d), *ws)


def _gla_body(p_ref, wa_ref, ba_ref, ng_ref, o_ref, st_ref):
    L = GLA_CHUNK
    n_chunks = p_ref.shape[1] // L
    hk = GLA_HEADS * GLA_DK
    hv = GLA_HEADS * GLA_DV

    @pl.when(pl.program_id(1) == 0)
    def _():
        st_ref[...] = jnp.zeros_like(st_ref)

    def chunk(c, carry):
        incl, _ = _tri_masks(L)
        tri01 = jnp.where(incl, 1.0, 0.0).astype(BF16)
        t0 = pl.multiple_of(c * L, L)
        rows = pl.ds(t0, L)
        q = p_ref[0, rows, 0:hk]
        k = p_ref[0, rows, hk:2 * hk]
        v = p_ref[0, rows, 2 * hk:2 * hk + hv]
        gt = p_ref[0, rows, 2 * hk + hv:2 * hk + 2 * hv]
        a_lr = p_ref[0, rows, 2 * hk + 2 * hv:GLA_COLS_PAD]
        la = _log_sigmoid(_mm(a_lr, wa_ref[...]) + ba_ref[...]) * (1.0 / GLA_TAU)
        g = _mm_exact_lhs01(tri01, la)
        g_last = g[L - 1:L, :]
        q_dec = (q * GLA_DK ** -0.5) * jnp.exp(g)
        k_inv = k * jnp.exp(-g)
        k_end = k * jnp.exp(g_last - g)
        sd = jnp.exp(g_last)
        H = range(GLA_HEADS)
        ks = [slice(h * GLA_DK, (h + 1) * GLA_DK) for h in H]
        vs = [slice(h * GLA_DV, (h + 1) * GLA_DV) for h in H]
        qd = [q_dec[:, ks[h]] for h in H]
        vh = [v[:, vs[h]] for h in H]
        st = [st_ref[h] for h in H]
        sc = [jnp.where(incl, _mm_nt(qd[h], k_inv[:, ks[h]]), 0.0) for h in H]
        o_state = [_mm_nt(qd[h], st[h]) for h in H]
        upd = [_mm_tn(vh[h], k_end[:, ks[h]]) for h in H]
        o_loc = [_mm(sc[h], vh[h]) for h in H]
        for h in H:
            st_ref[h] = st[h] * sd[:, ks[h]] + upd[h]
            o = o_loc[h] + o_state[h]
            o = o * lax.rsqrt(jnp.mean(o * o, axis=-1, keepdims=True) + NORM_EPS)
            o_ref[0, rows, vs[h]] = o * ng_ref[:, vs[h]] * _silu(gt[:, vs[h]])
        return carry

    lax.fori_loop(0, n_chunks, chunk, 0)


def gla_mixer(pa, w_alpha2, b_alpha, norm_g):
    b, t, _ = pa.shape
    tt = min(MIX_TILE, t)
    hk, hv = GLA_HEADS * GLA_DK, GLA_HEADS * GLA_DV
    wa = jnp.zeros((LANES, hk), BF16).at[:GLA_RANK].set(w_alpha2.astype(BF16))
    return pl.pallas_call(
        _gla_body,
        out_shape=jax.ShapeDtypeStruct((b, t, hv), F32),
        grid=(b, t // tt),
        in_specs=[pl.BlockSpec((1, tt, GLA_COLS_PAD), lambda i, j: (i, j, 0)),
                  _const_spec((LANES, hk)), _const_spec((1, hk)), _const_spec((1, hv))],
        out_specs=pl.BlockSpec((1, tt, hv), lambda i, j: (i, j, 0)),
        scratch_shapes=[pltpu.VMEM((GLA_HEADS, GLA_DV, GLA_DK), F32)],
        compiler_params=_params(("parallel", "arbitrary")),
        name="gla_mixer",
    )(pa, wa, b_alpha.reshape(1, hk), norm_g.reshape(1, hv))


_RWKV_STAGED = ("a_dec", "r_dec", "b_inv", "k_inv", "b_end", "k_end", "v", "gate", "bonus")


def _rwkv_body(p_ref, mu_ref, w0_ref, w2_ref, a0_ref, a2_ref, g2_ref, kk_ref, ka_ref, rk_ref, lng_ref, lnb_ref,
               seg_ref, o_ref, sp_ref, prev_ref, st_ref, gl_ref):
    L = RWKV_CHUNK
    W = RWKV_WIDTH
    N = RWKV_N
    nb = p_ref.shape[0]
    n_chunks = p_ref.shape[1] // L
    n_pairs = RWKV_HEADS // 2

    @pl.when(pl.program_id(1) == 0)
    def _():
        sp_ref[...] = jnp.zeros_like(sp_ref)
        prev_ref[...] = jnp.zeros_like(prev_ref)

    M = nb * L
    HW = W // 2

    def seg_sum(x):
        parts = [piece[:, s * HW:(s + 1) * HW] for piece in _split3(x) for s in range(2)]
        out = jnp.dot(jnp.concatenate(parts, axis=0), seg_ref[...], preferred_element_type=F32)
        m = x.shape[0]
        halves = [out[s * m:(s + 1) * m] + out[(2 + s) * m:(3 + s) * m] + out[(4 + s) * m:(5 + s) * m]
                  for s in range(2)]
        return jnp.concatenate(halves, axis=1)

    def prepare(c, slot):
        rows = pl.ds(pl.multiple_of(c * L, L), L)
        p = jnp.concatenate([p_ref[b, rows, :] for b in range(nb)], axis=0)
        rolled = pltpu.roll(p, 1, 0)
        first = lax.broadcasted_iota(jnp.int32, (SUBLANES, p.shape[1]), 0) == 0
        pieces = []
        for b in range(nb):
            pieces.append(jnp.where(first, prev_ref[b, 0:1, :], rolled[b * L:b * L + SUBLANES, :]))
            pieces.append(rolled[b * L + SUBLANES:(b + 1) * L, :])
            prev_ref[b, 0:1, :] = p[(b + 1) * L - 1:(b + 1) * L, :]
        sh = jnp.concatenate(pieces, axis=0)
        pf = p + (sh - p) * mu_ref[...]
        r = pf[:, 0:W]
        k = pf[:, W:2 * W]
        v = pf[:, 2 * W:3 * W]
        wa = pf[:, 3 * W:3 * W + LANES]
        g_lr = pf[:, 3 * W + LANES:3 * W + 2 * LANES]
        log_w = -RWKV_DECAY_SCALE * jax.nn.sigmoid(w0_ref[...] + _mm(jnp.tanh(wa), w2_ref[...]))
        a = jax.nn.sigmoid(a0_ref[...] + _mm(wa, a2_ref[...]))
        gate = _mm(jax.nn.sigmoid(g_lr), g2_ref[...])
        yield
        kk = k * kk_ref[...]
        kk = kk * lax.rsqrt(seg_sum(kk * kk) + RWKV_L2_EPS)
        yield
        k = k * (1.0 + (a - 1.0) * ka_ref[...])
        b_vec = kk * a
        ri = lax.broadcasted_iota(jnp.int32, (M, M), 0)
        ci = lax.broadcasted_iota(jnp.int32, (M, M), 1)
        same_seq = jnp.bitwise_and(ri, -L) == jnp.bitwise_and(ci, -L)
        tri01 = jnp.where(same_seq & (ri >= ci), 1.0, 0.0).astype(BF16)
        g = _mm_exact_lhs01(tri01, log_w)
        yield
        g_last = jnp.concatenate([jnp.broadcast_to(g[(b + 1) * L - 1:(b + 1) * L, :], (L, W)) for b in range(nb)],
                                 axis=0)
        e_neg = jnp.exp(-g)
        e_end = jnp.exp(g_last - g)
        bonus = seg_sum(r * k * rk_ref[...]) * v
        yield
        vals = dict(a_dec=-kk * jnp.exp(g - log_w), r_dec=r * jnp.exp(g), b_inv=b_vec * e_neg, k_inv=k * e_neg,
                    b_end=b_vec * e_end, k_end=k * e_end, v=v, gate=gate, bonus=bonus)
        for j, name in enumerate(_RWKV_STAGED):
            st_ref[slot, j] = vals[name]
        for b in range(nb):
            gl_ref[slot, b] = g[(b + 1) * L - SUBLANES:(b + 1) * L, :]
        yield

    def recur(c, slot, filler):
        def tick():
            if filler is not None:
                next(filler, None)

        lane = lax.broadcasted_iota(jnp.int32, (L, LANES), 1)
        rowi = lax.broadcasted_iota(jnp.int32, (L, LANES), 0)
        lo = lane < N
        hi = lane >= N
        colm = jnp.bitwise_and(lane, N - 1)
        strict2 = rowi > colm
        incl2 = rowi >= colm
        r2 = lax.broadcasted_iota(jnp.int32, (LANES, LANES), 0) < N
        c2 = lax.broadcasted_iota(jnp.int32, (LANES, LANES), 1) < N
        diag = r2 == c2
        nat = (lo, hi)
        oth = (hi, lo)
        sel = lambda m, x: jnp.where(m, x, 0.0)
        swap = lambda x: pltpu.roll(x, N, 1)
        cat0 = lambda xs: jnp.concatenate(xs, axis=0)
        dot = lambda x, y: jnp.dot(x, y, preferred_element_type=F32)
        zeros_b = jnp.zeros((L, LANES), BF16)
        rows = pl.ds(pl.multiple_of(c * L, L), L)
        U = [(b, p) for b in range(nb) for p in range(n_pairs)]
        HH = range(2)
        idx = {name: j for j, name in enumerate(_RWKV_STAGED)}
        blk = lambda name, u: st_ref[slot, idx[name], u[0] * L:(u[0] + 1) * L, u[1] * LANES:(u[1] + 1) * LANES]
        ad = [blk("a_dec", u) for u in U]
        rd = [blk("r_dec", u) for u in U]
        ad_sw = [swap(x) for x in ad]
        rd_sw = [swap(x) for x in rd]
        be_sw = [swap(blk("b_end", u)) for u in U]
        ke_sw = [swap(blk("k_end", u)) for u in U]
        sd_sw = [jnp.exp(swap(gl_ref[slot, u[0], :, u[1] * LANES:(u[1] + 1) * LANES])[SUBLANES - 1:SUBLANES, :])
                 for u in U]
        vn = [[sel(nat[h], blk("v", u)).astype(BF16) for h in HH] for u in U]
        nu = range(len(U))

        b_inv = [blk("b_inv", u) for u in U]
        k_inv = [blk("k_inv", u) for u in U]
        pe = [_mm_nt(cat0([sel(lo, ad[i]), sel(lo, rd[i])]), cat0([b_inv[i], k_inv[i]])) for i in nu]
        po = [_mm_nt(cat0([sel(hi, ad[i]), sel(hi, rd[i])]), cat0([k_inv[i], b_inv[i]])) for i in nu]
        tick()
        aa = [[sel(strict2, pe[i][0:L, :]), sel(strict2, po[i][0:L, :])] for i in nu]
        ar = [[sel(incl2, pe[i][L:2 * L, :]).astype(BF16), sel(incl2, po[i][L:2 * L, :]).astype(BF16)] for i in nu]
        t0 = [dot(jnp.where(lo, aa[i][1], aa[i][0]).astype(BF16),
                  cat0([jnp.concatenate([vn[i][1], zeros_b], axis=1), jnp.concatenate([zeros_b, vn[i][0]], axis=1)]))
              for i in nu]
        eye2 = jnp.where(rowi == colm, 1.0, 0.0)
        q0 = [jnp.where(lo, aa[i][0], eye2) for i in nu]
        q1 = [jnp.where(lo, eye2, aa[i][1]) for i in nu]
        for _ in range(6):
            out = [dot(jnp.where(lo, q0[i], q1[i]).astype(BF16),
                       cat0([jnp.concatenate([q0[i].astype(BF16), zeros_b], axis=1),
                             jnp.concatenate([zeros_b, q1[i].astype(BF16)], axis=1)])) for i in nu]
            tick()
            q0 = [out[i][:, 0:LANES] + sel(hi, q0[i]) for i in nu]
            q1 = [out[i][:, LANES:2 * LANES] + sel(lo, q1[i]) for i in nu]
        x0 = [[jnp.where(hi, ad_sw[i], t0[i][:, LANES:2 * LANES]).astype(BF16),
               jnp.where(lo, ad_sw[i], t0[i][:, 0:LANES]).astype(BF16)] for i in nu]
        tx = [dot(jnp.where(lo, q1[i], q0[i]).astype(BF16),
                  cat0([jnp.concatenate([x0[i][1], zeros_b], axis=1), jnp.concatenate([zeros_b, x0[i][0]], axis=1)]))
              for i in nu]
        tick()
        xb = [[tx[i][:, LANES:2 * LANES].astype(BF16), tx[i][:, 0:LANES].astype(BF16)] for i in nu]
        rmy = [[sel(hi, rd_sw[i]) + dot(ar[i][0], cat0([xb[i][0], vn[i][0]])),
                sel(lo, rd_sw[i]) + dot(ar[i][1], cat0([vn[i][1], xb[i][1]]))] for i in nu]
        gg = [_mm_tn(cat0([xb[i][0], xb[i][1], vn[i][0], vn[i][1]]),
                     cat0([sel(hi, be_sw[i]), sel(lo, be_sw[i]), sel(hi, ke_sw[i]), sel(lo, ke_sw[i])]))
              for i in nu]
        tick()
        sp = [sp_ref[i] for i in nu]
        ys = [_mm_nt(jnp.where(lo, rmy[i][1], rmy[i][0]), sp[i]) + jnp.where(lo, rmy[i][0], rmy[i][1]) for i in nu]
        for i in nu:
            sp_ref[i] = sp[i] * sd_sw[i] + _mm(sp[i], sel(diag, gg[i])) + jnp.where(diag, 0.0, gg[i])
        if filler is not None:
            for _ in filler:
                pass
        y = jnp.concatenate([jnp.concatenate(ys[b * n_pairs:(b + 1) * n_pairs], axis=1) for b in range(nb)], axis=0)
        mean = seg_sum(y) * (1.0 / N)
        dy = y - mean
        var = seg_sum(dy * dy) * (1.0 / N)
        y = dy * lax.rsqrt(var + RWKV_GN_EPS) * lng_ref[...] + lnb_ref[...]
        y = (y + st_ref[slot, idx["bonus"]]) * st_ref[slot, idx["gate"]]
        for b in range(nb):
            o_ref[b, rows, :] = y[b * L:(b + 1) * L, :]

    for _ in prepare(0, 0):
        pass

    def body(c, carry):
        slot = jnp.bitwise_and(c, 1)
        recur(c, slot, prepare(c + 1, 1 - slot))
        return carry

    lax.fori_loop(0, n_chunks - 1, body, 0)
    recur(n_chunks - 1, (n_chunks - 1) % 2, None)


def rwkv_mixer(pb, mu, w0, w2, a0, a2, g2, k_k, k_a, r_k, ln_g, ln_b):
    b, t, cols = pb.shape
    tt = min(RWKV_TILE, t)
    nb = min(RWKV_BATCH_PER_STEP, b)
    W = RWKV_WIDTH
    row = lambda x: x.reshape(1, -1)
    w2p = jnp.zeros((LANES, W), BF16).at[:RWKV_LORA_W].set(w2.astype(BF16))
    a2p = jnp.zeros((LANES, W), BF16).at[RWKV_LORA_W:].set(a2.astype(BF16))
    head_of = jnp.arange(W // 2) // RWKV_N
    seg = (head_of[:, None] == head_of[None, :]).astype(BF16)
    consts = [row(mu), row(w0), w2p, row(a0), a2p, g2.astype(BF16), row(k_k), row(k_a), row(r_k), row(ln_g),
              row(ln_b), seg]
    return pl.pallas_call(
        _rwkv_body,
        out_shape=jax.ShapeDtypeStruct((b, t, W), F32),
        grid=(b // nb, t // tt),
        in_specs=[pl.BlockSpec((nb, tt, cols), lambda i, j: (i, j, 0))] + [_const_spec(x.shape) for x in consts],
        out_specs=pl.BlockSpec((nb, tt, W), lambda i, j: (i, j, 0)),
        scratch_shapes=[pltpu.VMEM((nb * RWKV_HEADS // 2, LANES, LANES), F32),
                        pltpu.VMEM((nb, SUBLANES, cols), F32),
                        pltpu.VMEM((2, len(_RWKV_STAGED), nb * RWKV_CHUNK, W), F32),
                        pltpu.VMEM((2, nb, SUBLANES, W), F32)],
        compiler_params=_params(("parallel", "arbitrary")),
        name="rwkv_mixer",
    )(pb, *consts)


def _rglru_body(p_ref, cw_ref, cb_ref, gw_ref, gb_ref, lam_ref, o_ref, xprev_ref, hprev_ref):
    tt = p_ref.shape[1]
    W = LRU_WIDTH

    @pl.when(pl.program_id(1) == 0)
    def _():
        xprev_ref[...] = jnp.zeros_like(xprev_ref)
        hprev_ref[...] = jnp.zeros_like(hprev_ref)

    x = p_ref[0, :, 0:W]
    gate = p_ref[0, :, W:2 * W]
    xc = _causal_conv(x, xprev_ref[...], cw_ref, cb_ref, LRU_CONV)
    xprev_ref[...] = x[tt - SUBLANES:tt, :]
    xcb = xc.astype(BF16)
    pre = []
    for gi in range(2):
        pre.append(jnp.concatenate(
            [jnp.dot(xcb[:, n * LRU_BLOCK:(n + 1) * LRU_BLOCK], gw_ref[gi, n], preferred_element_type=F32)
             for n in range(LRU_BLOCKS)], axis=1) + gb_ref[gi:gi + 1, :])
    r_gate = jax.nn.sigmoid(pre[0])
    i_gate = jax.nn.sigmoid(pre[1])
    log_a = -LRU_C * r_gate * _softplus(-lam_ref[...])
    a = jnp.exp(log_a)
    u = jnp.sqrt(-jnp.tanh(log_a) * (a * a + 1.0)) * (i_gate * xc)
    d = 1
    while d < tt:
        u = u + a * _shift_rows(u, d, 0.0)
        a = a * _shift_rows(a, d, 1.0)
        d *= 2
    h = u + a * hprev_ref[0:1, :]
    hprev_ref[0:1, :] = h[tt - 1:tt, :]
    o_ref[0] = h * jax.nn.gelu(gate)


def rglru_mixer(pc, conv_w, conv_b, gate_w, gate_b, lam):
    b, t, cols = pc.shape
    tt = min(MIX_TILE, t)
    W = LRU_WIDTH
    cw = jnp.zeros((SUBLANES, W), F32).at[:LRU_CONV].set(conv_w)
    consts = [cw, conv_b.reshape(1, W), gate_w.astype(BF16), gate_b, lam.reshape(1, W)]
    return pl.pallas_call(
        _rglru_body,
        out_shape=jax.ShapeDtypeStruct((b, t, W), F32),
        grid=(b, t // tt),
        in_specs=[pl.BlockSpec((1, tt, cols), lambda i, j: (i, j, 0))] + [_const_spec(x.shape) for x in consts],
        out_specs=pl.BlockSpec((1, tt, W), lambda i, j: (i, j, 0)),
        scratch_shapes=[pltpu.VMEM((SUBLANES, W), F32), pltpu.VMEM((SUBLANES, W), F32)],
        compiler_params=_params(("parallel", "arbitrary")),
        name="rglru_mixer",
    )(pc, *consts)


def _mlstm_body(p_ref, cw_ref, cb_ref, wq_ref, wk_ref, wv_ref, bif_ref, ng_ref, o_ref,
                xprev_ref, c_ref, n_ref, m_ref, q_s, k_s, v_s):
    tt = p_ref.shape[1]
    L = MLSTM_CHUNK
    W = MLSTM_WIDTH
    DH = MLSTM_DH
    H = MLSTM_HEADS
    n_chunks = tt // L

    @pl.when(pl.program_id(1) == 0)
    def _():
        xprev_ref[...] = jnp.zeros_like(xprev_ref)
        c_ref[...] = jnp.zeros_like(c_ref)
        n_ref[...] = jnp.zeros_like(n_ref)
        m_ref[...] = jnp.zeros_like(m_ref)

    x = p_ref[0, :, 0:W]
    xc = _silu(_causal_conv(x, xprev_ref[...], cw_ref, cb_ref, MLSTM_CONV))
    xprev_ref[...] = x[tt - SUBLANES:tt, :]
    xcb = xc.astype(BF16)
    q_s[...] = jnp.dot(xcb, wq_ref[...], preferred_element_type=F32)
    k_s[...] = jnp.dot(xcb, wk_ref[...], preferred_element_type=F32) * DH ** -0.5
    v_s[...] = jnp.dot(x.astype(BF16), wv_ref[...], preferred_element_type=F32)

    def chunk(c, carry):
        incl, _ = _tri_masks(L)
        tri01 = jnp.where(incl, 1.0, 0.0).astype(BF16)
        eye = jnp.where(lax.broadcasted_iota(jnp.int32, (L, L), 0) == lax.broadcasted_iota(jnp.int32, (L, L), 1),
                        1.0, 0.0)

        def to_row(col):
            return jnp.sum(col * eye, axis=0, keepdims=True)

        t0 = pl.multiple_of(c * L, L)
        rows = pl.ds(t0, L)
        gates = p_ref[0, rows, 2 * W:2 * W + LANES] + bif_ref[...]
        bcum = _mm_exact_lhs01(tri01, _log_sigmoid(gates))
        HS = range(H)
        hs = [slice(h * DH, (h + 1) * DH) for h in HS]
        qh = [q_s[rows, hs[h]] for h in HS]
        kh = [k_s[rows, hs[h]] for h in HS]
        vh = [v_s[rows, hs[h]] for h in HS]
        cm = [c_ref[h] for h in HS]
        nv = [n_ref[h, 0:1, :] for h in HS]
        m = [m_ref[h, 0:1, 0:1] for h in HS]
        i_col = [gates[:, h:h + 1] for h in HS]
        b_col = [bcum[:, H + h:H + h + 1] for h in HS]
        b_last = [b_col[h][L - 1:L, :] for h in HS]
        qk = [_mm_nt(qh[h], kh[h]) for h in HS]
        q_c = [_mm(qh[h], cm[h]) for h in HS]
        log_e = [b_last[h] - b_col[h] + i_col[h] for h in HS]
        m_end = [jnp.max(log_e[h], axis=0, keepdims=True) for h in HS]
        m_new = [jnp.maximum(b_last[h] + m[h], m_end[h]) for h in HS]
        kin = [kh[h] * (jnp.exp(m_end[h] - m_new[h]) * jnp.exp(log_e[h] - m_end[h])) for h in HS]
        c_upd = [_mm_tn(kin[h], vh[h]) for h in HS]
        log_d = [jnp.where(incl, b_col[h] - to_row(b_col[h]) + to_row(i_col[h]), -jnp.inf) for h in HS]
        m_loc = [jnp.max(log_d[h], axis=1, keepdims=True) for h in HS]
        w_loc = [jnp.exp(log_d[h] - m_loc[h]) * qk[h] for h in HS]
        num_loc = [_mm(w_loc[h], vh[h]) for h in HS]
        for h in HS:
            c_state = jnp.exp(b_last[h] + m[h] - m_new[h])
            c_ref[h] = c_state * cm[h] + c_upd[h]
            n_ref[h, 0:1, :] = c_state * nv[h] + jnp.sum(kin[h], axis=0, keepdims=True)
            m_ref[h] = jnp.broadcast_to(m_new[h], (SUBLANES, LANES))
            den_loc = jnp.sum(w_loc[h], axis=1, keepdims=True)
            m_t = jnp.maximum(b_col[h] + m[h], m_loc[h])
            s_state = jnp.exp(b_col[h] + m[h] - m_t)
            s_loc = jnp.exp(m_loc[h] - m_t)
            num = s_state * q_c[h] + s_loc * num_loc[h]
            den = s_state * jnp.sum(qh[h] * nv[h], axis=1, keepdims=True) + s_loc * den_loc
            hv = num / jnp.maximum(jnp.abs(den), jnp.exp(-m_t))
            hv = hv * lax.rsqrt(jnp.mean(hv * hv, axis=-1, keepdims=True) + NORM_EPS)
            o_ref[0, rows, hs[h]] = (jax.nn.sigmoid(p_ref[0, rows, W + h * DH:W + (h + 1) * DH]) * hv
                                     * ng_ref[:, hs[h]])
        return carry

    lax.fori_loop(0, n_chunks, chunk, 0)


def _block_diag(w):
    nb, d, e = w.shape
    eye = jnp.eye(nb, dtype=w.dtype)
    return (eye[:, None, :, None] * w[:, :, None, :]).reshape(nb * d, nb * e)


def mlstm_mixer(pd, conv_w, conv_b, qkv_w, b_if, norm_g):
    b, t, cols = pd.shape
    tt = min(MIX_TILE, t)
    W = MLSTM_WIDTH
    cw = jnp.zeros((SUBLANES, W), F32).at[:MLSTM_CONV].set(conv_w)
    bif = jnp.zeros((1, LANES), F32).at[0, :2 * MLSTM_HEADS].set(b_if)
    consts = [cw, conv_b.reshape(1, W)] + [_block_diag(qkv_w[i]).astype(BF16) for i in range(3)] + [
        bif, norm_g.reshape(1, W)]
    return pl.pallas_call(
        _mlstm_body,
        out_shape=jax.ShapeDtypeStruct((b, t, W), F32),
        grid=(b, t // tt),
        in_specs=[pl.BlockSpec((1, tt, cols), lambda i, j: (i, j, 0))] + [_const_spec(x.shape) for x in consts],
        out_specs=pl.BlockSpec((1, tt, W), lambda i, j: (i, j, 0)),
        scratch_shapes=[pltpu.VMEM((SUBLANES, W), F32),
                        pltpu.VMEM((MLSTM_HEADS, MLSTM_DH, MLSTM_DH), F32),
                        pltpu.VMEM((MLSTM_HEADS, SUBLANES, MLSTM_DH), F32),
                        pltpu.VMEM((MLSTM_HEADS, SUBLANES, LANES), F32),
                        pltpu.VMEM((tt, W), F32), pltpu.VMEM((tt, W), F32), pltpu.VMEM((tt, W), F32)],
        compiler_params=_params(("parallel", "arbitrary")),
        name="mlstm_mixer",
    )(pd, *consts)


def _mix_xattn_body(h_ref, ya_ref, yb_ref, wm_ref, g_ref, wq_ref, k_ref, v_ref, wo_ref, o_ref):
    mixed = jnp.concatenate([ya_ref[0], yb_ref[0]], axis=1).astype(BF16)
    h = h_ref[0] + jnp.dot(mixed, wm_ref[...], preferred_element_type=F32)
    xn = _rmsnorm(h, g_ref[...]).astype(BF16)
    q = jnp.dot(xn, wq_ref[...], preferred_element_type=F32)
    outs = []
    for hd in range(XA_HEADS):
        hs = slice(hd * XA_DH, (hd + 1) * XA_DH)
        s = _mm_nt(q[:, hs], k_ref[0, :, hs]) * XA_DH ** -0.5
        e = jnp.exp(s - jnp.max(s, axis=-1, keepdims=True))
        p = e / jnp.sum(e, axis=-1, keepdims=True)
        outs.append(_mm(p, v_ref[0, :, hs]))
    o = jnp.concatenate(outs, axis=1).astype(BF16)
    o_ref[0] = h + jnp.dot(o, wo_ref[...], preferred_element_type=F32)


def mix_xattn(h, ya, yb, w_mix, g, wq, kv, wo):
    b, t, d = h.shape
    tm = min(ROW_TILE, t)
    tok = lambda w: pl.BlockSpec((1, tm, w), lambda i, j: (i, j, 0))
    mem_k = pl.BlockSpec((1, MEM_LEN, d), lambda i, j: (i, 0, 0))
    mem_v = pl.BlockSpec((1, MEM_LEN, d), lambda i, j: (i, 0, 1))
    return pl.pallas_call(
        _mix_xattn_body,
        out_shape=jax.ShapeDtypeStruct((b, t, d), F32),
        grid=(b, t // tm),
        in_specs=[tok(d), tok(ya.shape[-1]), tok(yb.shape[-1]), _const_spec(w_mix.shape), _const_spec((1, d)),
                  _const_spec(wq.shape), mem_k, mem_v, _const_spec(wo.shape)],
        out_specs=tok(d),
        compiler_params=_params(("parallel", "parallel")),
        name="mix_xattn",
    )(h, ya, yb, w_mix, g.reshape(1, d), wq, kv, kv, wo)


def _ffn_body(final_norm, n_split, h_ref, g_ref, wu_ref, wg_ref, cw_ref, cb_ref, wd_ref, fg_ref, o_ref, uprev_ref):
    tm = h_ref.shape[1]
    fc = D_FF_PAD // n_split

    @pl.when(pl.program_id(1) == 0)
    def _():
        uprev_ref[...] = jnp.zeros_like(uprev_ref)

    h = h_ref[0]
    xn = _rmsnorm(h, g_ref[...]).astype(BF16)
    acc = h
    for s in range(n_split):
        cs = slice(s * fc, (s + 1) * fc)
        u = jnp.dot(xn, wu_ref[:, cs], preferred_element_type=F32)
        gt = jnp.dot(xn, wg_ref[:, cs], preferred_element_type=F32)
        uu = jnp.concatenate([uprev_ref[:, cs], u], axis=0)
        uprev_ref[:, cs] = u[tm - SUBLANES:tm, :]
        c = cb_ref[:, cs] + cw_ref[2:3, cs] * u
        c = c + cw_ref[1:2, cs] * pltpu.roll(uu, 1, 0)[SUBLANES:SUBLANES + tm, :]
        c = c + cw_ref[0:1, cs] * pltpu.roll(uu, 2, 0)[SUBLANES:SUBLANES + tm, :]
        act = (_silu(c) * gt).astype(BF16)
        acc = acc + jnp.dot(act, wd_ref[cs, :], preferred_element_type=F32)
    if final_norm:
        acc = _rmsnorm(acc, fg_ref[...])
    o_ref[0] = acc


def ffn(h, g, wu, wg, conv_w, conv_b, wd, final_g, final_norm):
    b, t, d = h.shape
    tm = min(ROW_TILE, t)
    tok = pl.BlockSpec((1, tm, d), lambda i, j: (i, j, 0))
    consts = [g.reshape(1, d), wu, wg, conv_w, conv_b, wd, final_g.reshape(1, d)]
    return pl.pallas_call(
        functools.partial(_ffn_body, final_norm, 2),
        out_shape=jax.ShapeDtypeStruct((b, t, d), F32),
        grid=(b, t // tm),
        in_specs=[tok] + [_const_spec(x.shape) for x in consts],
        out_specs=tok,
        scratch_shapes=[pltpu.VMEM((SUBLANES, D_FF_PAD), F32)],
        compiler_params=_params(("parallel", "arbitrary")),
        name="ffn",
    )(h, *consts)


def _pad_cols(w, n):
    return jnp.pad(w, ((0, 0), (0, n - w.shape[1])))


def _rwkv_perm(x):
    W = RWKV_WIDTH
    o_w, o_k, o_v = W, W + RWKV_LORA_W, 2 * W + RWKV_LORA_W
    o_a = 3 * W + RWKV_LORA_W
    o_g = o_a + RWKV_LORA_A
    return jnp.concatenate([x[..., :W], x[..., o_k:o_k + W], x[..., o_v:o_v + W], x[..., o_w:o_w + RWKV_LORA_W],
                            x[..., o_a:o_a + RWKV_LORA_A], x[..., o_g:]], axis=-1)


def kernel(x, mem, mem_norm_g, norm_mix_g, ab_w_in, gla_w_alpha2, gla_b_alpha, gla_norm_g, rwkv_mu, rwkv_w0, rwkv_w2, rwkv_a0, rwkv_a2, rwkv_g2, rwkv_k_k, rwkv_k_a, rwkv_r_k, rwkv_ln_g, rwkv_ln_b, cd_w_in, lru_conv_w, lru_conv_b, lru_gate_w, lru_gate_b, lru_lambda, mlstm_conv_w, mlstm_conv_b, mlstm_qkv_w, mlstm_b_if, mlstm_norm_g, w_mix_out, norm_xattn_g, xattn_wq, xattn_wkv, xattn_wo, norm_ffn_g, ffn_w_up, ffn_conv_w, ffn_conv_b, ffn_w_down, final_norm_g):
    b, t, d = x.shape
    depth = norm_mix_g.shape[0]
    n = b * t
    h = x
    mem2d = mem.reshape(b * MEM_LEN, d)
    for layer in range(depth):
        j = layer // 2
        h2d = h.reshape(n, d)
        if layer % 2 == 0:
            w = ab_w_in[j]
            wa = _pad_cols(w[:, :GLA_COLS], GLA_COLS_PAD).astype(BF16)
            wb = _rwkv_perm(w[:, GLA_COLS:]).astype(BF16)
            pa, pb = norm_matmul(h2d, norm_mix_g[layer], [wa, wb], name="proj_ab")
            ya = gla_mixer(pa.reshape(b, t, -1), gla_w_alpha2[j], gla_b_alpha[j], gla_norm_g[j])
            yb = rwkv_mixer(pb.reshape(b, t, -1), _rwkv_perm(rwkv_mu[j]), rwkv_w0[j], rwkv_w2[j], rwkv_a0[j],
                            rwkv_a2[j], rwkv_g2[j], rwkv_k_k[j], rwkv_k_a[j], rwkv_r_k[j], rwkv_ln_g[j],
                            rwkv_ln_b[j])
        else:
            w = cd_w_in[j]
            wc = w[:, :2 * LRU_WIDTH].astype(BF16)
            wd_ = _pad_cols(w[:, 2 * LRU_WIDTH:], MLSTM_COLS_PAD).astype(BF16)
            pc, pd = norm_matmul(h2d, norm_mix_g[layer], [wc, wd_], name="proj_cd")
            ya = rglru_mixer(pc.reshape(b, t, -1), lru_conv_w[j], lru_conv_b[j], lru_gate_w[j], lru_gate_b[j],
                             lru_lambda[j])
            yb = mlstm_mixer(pd.reshape(b, t, -1), mlstm_conv_w[j], mlstm_conv_b[j], mlstm_qkv_w[j],
                             mlstm_b_if[j], mlstm_norm_g[j])
        (kv,) = norm_matmul(mem2d, mem_norm_g, [xattn_wkv[layer].astype(BF16)], out_dtype=BF16, name="proj_kv")
        kv = kv.reshape(b, MEM_LEN, 2 * d)
        h = mix_xattn(h, ya, yb, w_mix_out[layer].astype(BF16), norm_xattn_g[layer], xattn_wq[layer].astype(BF16),
                      kv, xattn_wo[layer].astype(BF16))
        wup = ffn_w_up[layer]
        wu = _pad_cols(wup[:, :D_FF], D_FF_PAD).astype(BF16)
        wg = _pad_cols(wup[:, D_FF:], D_FF_PAD).astype(BF16)
        cw = _pad_cols(jnp.pad(ffn_conv_w[layer], ((0, SUBLANES - FFN_CONV), (0, 0))), D_FF_PAD)
        cb = _pad_cols(ffn_conv_b[layer].reshape(1, D_FF), D_FF_PAD)
        wdn = jnp.pad(ffn_w_down[layer], ((0, D_FF_PAD - D_FF), (0, 0))).astype(BF16)
        h = ffn(h, norm_ffn_g[layer], wu, wg, cw, cb, wdn, final_norm_g, layer == depth - 1)
    return h
```

```python
import functools

import jax
import jax.numpy as jnp
from jax import lax
from jax.experimental import pallas as pl
from jax.experimental.pallas import tpu as pltpu

F32 = jnp.float32
BF16 = jnp.bfloat16

D_MODEL = 1024
NORM_EPS = 1e-6
LANES = 128
SUBLANES = 8
VMEM_LIMIT_BYTES = 56 * 1024 * 1024

GLA_HEADS, GLA_DK, GLA_DV, GLA_RANK, GLA_TAU, GLA_CHUNK = 4, 64, 128, 16, 16.0, 64
GLA_COLS = 2 * GLA_HEADS * GLA_DK + 2 * GLA_HEADS * GLA_DV + GLA_RANK
GLA_COLS_PAD = 13 * LANES

RWKV_HEADS, RWKV_N, RWKV_WIDTH = 8, 64, 512
RWKV_LORA_W, RWKV_LORA_A, RWKV_LORA_G = 64, 64, 128
RWKV_CHUNK = 64
RWKV_DECAY_SCALE = 0.6065306597126334
RWKV_GN_EPS = RWKV_N * 1e-5
RWKV_L2_EPS = 1e-12
RWKV_COLS = 3 * RWKV_WIDTH + RWKV_LORA_W + RWKV_LORA_A + RWKV_LORA_G

LRU_WIDTH, LRU_BLOCKS, LRU_BLOCK, LRU_C, LRU_CONV = 512, 4, 128, 8.0, 4
MLSTM_HEADS, MLSTM_DH, MLSTM_WIDTH, MLSTM_CONV, MLSTM_CHUNK = 4, 128, 512, 4, 64
MLSTM_COLS_PAD = 2 * MLSTM_WIDTH + LANES

XA_HEADS, XA_DH, MEM_LEN = 4, 256, 256
D_FF, FFN_CONV = 2752, 3
D_FF_PAD = 22 * LANES
MXU_K_TILE = 256
FFN_COL_GROUPS = (6 * MXU_K_TILE, 5 * MXU_K_TILE)

MIX_TILE = 256
RWKV_TILE = 512
GLA_BATCH_PER_STEP = 4
MLSTM_BATCH_PER_STEP = 2
RWKV_BATCH_PER_STEP = 2
ROW_TILE = 512


def _mm(a, b):
    return jnp.dot(a.astype(BF16), b.astype(BF16), preferred_element_type=F32)


def _mm_nt(a, b):
    return lax.dot_general(a.astype(BF16), b.astype(BF16), (((1,), (1,)), ((), ())), preferred_element_type=F32)


def _mm_tn(a, b):
    return lax.dot_general(a.astype(BF16), b.astype(BF16), (((0,), (0,)), ((), ())), preferred_element_type=F32)


def _split3(x):
    x1 = x.astype(BF16)
    r1 = x - x1.astype(F32)
    x2 = r1.astype(BF16)
    x3 = (r1 - x2.astype(F32)).astype(BF16)
    return x1, x2, x3


def _mm_exact_lhs01(m01, x):
    x1, x2, x3 = _split3(x)
    d = lambda y: jnp.dot(m01, y, preferred_element_type=F32)
    return d(x1) + d(x2) + d(x3)


def _mm_exact_rhs01(x, m01):
    x1, x2, x3 = _split3(x)
    d = lambda y: jnp.dot(y, m01, preferred_element_type=F32)
    return d(x1) + d(x2) + d(x3)


def _rmsnorm(x, g):
    return x * lax.rsqrt(jnp.mean(x * x, axis=-1, keepdims=True) + NORM_EPS) * g


def _log_sigmoid(x):
    return jnp.minimum(x, 0.0) - jnp.log1p(jnp.exp(-jnp.abs(x)))


def _softplus(x):
    return jnp.maximum(x, 0.0) + jnp.log1p(jnp.exp(-jnp.abs(x)))


def _silu(x):
    return x * jax.nn.sigmoid(x)


def _tri_masks(n):
    r = lax.broadcasted_iota(jnp.int32, (n, n), 0)
    c = lax.broadcasted_iota(jnp.int32, (n, n), 1)
    return r >= c, r > c


def _shift_rows(x, s, fill):
    if s == 0:
        return x
    rolled = pltpu.roll(x, s, 0)
    row = lax.broadcasted_iota(jnp.int32, x.shape, 0)
    return jnp.where(row >= s, rolled, fill)


def _causal_conv(x, buf_ref, w, b, width):
    t = x.shape[0]
    buf_ref[SUBLANES:SUBLANES + t, :] = x
    y = b + w[width - 1:width, :] * x
    for j in range(width - 1):
        s = width - 1 - j
        y = y + w[j:j + 1, :] * buf_ref[SUBLANES - s:SUBLANES - s + t, :]
    buf_ref[0:SUBLANES, :] = x[t - SUBLANES:t, :]
    return y


def _const_spec(shape):
    nd = len(shape)
    return pl.BlockSpec(shape, lambda *_: (0,) * nd, pipeline_mode=pl.Buffered(1))


def _params(sem):
    return pltpu.CompilerParams(dimension_semantics=sem, vmem_limit_bytes=VMEM_LIMIT_BYTES)


def _norm_matmul_body(n_out, x_ref, g_ref, *refs):
    xn = _rmsnorm(x_ref[...], g_ref[...]).astype(BF16)
    for w_ref, o_ref in zip(refs[:n_out], refs[n_out:]):
        o_ref[...] = jnp.dot(xn, w_ref[...], preferred_element_type=F32).astype(o_ref.dtype)


def norm_matmul(x2d, g, ws, out_dtype=F32, name="norm_matmul"):
    n, d = x2d.shape
    tm = min(ROW_TILE, n)
    assert n % tm == 0
    return pl.pallas_call(
        functools.partial(_norm_matmul_body, len(ws)),
        out_shape=[jax.ShapeDtypeStruct((n, w.shape[1]), out_dtype) for w in ws],
        grid=(n // tm,),
        in_specs=[pl.BlockSpec((tm, d), lambda i: (i, 0)), _const_spec((1, d))]
        + [_const_spec(w.shape) for w in ws],
        out_specs=[pl.BlockSpec((tm, w.shape[1]), lambda i: (i, 0)) for w in ws],
        compiler_params=_params(("parallel",)),
        name=name,
    )(x2d, g.reshape(1, d), *ws)


def _gla_body(p_ref, wa_ref, ba_ref, ng_ref, o_ref, st_ref):
    L = GLA_CHUNK
    nb = p_ref.shape[0]
    n_chunks = p_ref.shape[1] // L
    M = nb * L
    hk = GLA_HEADS * GLA_DK
    hv = GLA_HEADS * GLA_DV

    @pl.when(pl.program_id(1) == 0)
    def _():
        st_ref[...] = jnp.zeros_like(st_ref)

    def chunk(c, carry):
        incl, _ = _tri_masks(L)
        ri = lax.broadcasted_iota(jnp.int32, (M, M), 0)
        ci = lax.broadcasted_iota(jnp.int32, (M, M), 1)
        same_seq = jnp.bitwise_and(ri, -L) == jnp.bitwise_and(ci, -L)
        tri01 = jnp.where(same_seq & (ri >= ci), 1.0, 0.0).astype(BF16)
        t0 = pl.multiple_of(c * L, L)
        rows = pl.ds(t0, L)
        cols = lambda lo, hi: jnp.concatenate([p_ref[b, rows, lo:hi] for b in range(nb)], axis=0)
        q = cols(0, hk)
        k = cols(hk, 2 * hk)
        v = cols(2 * hk, 2 * hk + hv)
        gt = cols(2 * hk + hv, 2 * hk + 2 * hv)
        a_lr = cols(2 * hk + 2 * hv, GLA_COLS_PAD)
        la = _log_sigmoid(_mm(a_lr, wa_ref[...]) + ba_ref[...]) * (1.0 / GLA_TAU)
        g = _mm_exact_lhs01(tri01, la)
        g_last = jnp.concatenate([jnp.broadcast_to(g[(b + 1) * L - 1:(b + 1) * L, :], (L, hk)) for b in range(nb)],
                                 axis=0)
        q_dec = (q * GLA_DK ** -0.5) * jnp.exp(g)
        k_inv = k * jnp.exp(-g)
        k_end = k * jnp.exp(g_last - g)
        sd = [jnp.exp(g[(b + 1) * L - 1:(b + 1) * L, :]) for b in range(nb)]
        U = [(b, h) for b in range(nb) for h in range(GLA_HEADS)]
        nu = range(len(U))
        rb = [slice(b * L, (b + 1) * L) for b in range(nb)]
        ks = [slice(h * GLA_DK, (h + 1) * GLA_DK) for h in range(GLA_HEADS)]
        vs = [slice(h * GLA_DV, (h + 1) * GLA_DV) for h in range(GLA_HEADS)]
        qd = [q_dec[rb[b], ks[h]] for b, h in U]
        vh = [v[rb[b], vs[h]] for b, h in U]
        st = [st_ref[i] for i in nu]
        sc = [jnp.where(incl, _mm_nt(qd[i], k_inv[rb[b], ks[h]]), 0.0) for i, (b, h) in enumerate(U)]
        o_state = [_mm_nt(qd[i], st[i]) for i in nu]
        upd = [_mm_tn(vh[i], k_end[rb[b], ks[h]]) for i, (b, h) in enumerate(U)]
        o_loc = [_mm(sc[i], vh[i]) for i in nu]
        for i, (b, h) in enumerate(U):
            st_ref[i] = st[i] * sd[b][:, ks[h]] + upd[i]
            o = o_loc[i] + o_state[i]
            o = o * lax.rsqrt(jnp.mean(o * o, axis=-1, keepdims=True) + NORM_EPS)
            o_ref[b, rows, vs[h]] = o * ng_ref[:, vs[h]] * _silu(gt[rb[b], vs[h]])
        return carry

    lax.fori_loop(0, n_chunks, chunk, 0)


def gla_mixer(pa, w_alpha2, b_alpha, norm_g):
    b, t, _ = pa.shape
    tt = min(MIX_TILE, t)
    nb = min(GLA_BATCH_PER_STEP, b)
    hk, hv = GLA_HEADS * GLA_DK, GLA_HEADS * GLA_DV
    wa = jnp.zeros((LANES, hk), BF16).at[:GLA_RANK].set(w_alpha2.astype(BF16))
    return pl.pallas_call(
        _gla_body,
        out_shape=jax.ShapeDtypeStruct((b, t, hv), F32),
        grid=(b // nb, t // tt),
        in_specs=[pl.BlockSpec((nb, tt, GLA_COLS_PAD), lambda i, j: (i, j, 0)),
                  _const_spec((LANES, hk)), _const_spec((1, hk)), _const_spec((1, hv))],
        out_specs=pl.BlockSpec((nb, tt, hv), lambda i, j: (i, j, 0)),
        scratch_shapes=[pltpu.VMEM((nb * GLA_HEADS, GLA_DV, GLA_DK), F32)],
        compiler_params=_params(("parallel", "arbitrary")),
        name="gla_mixer",
    )(pa, wa, b_alpha.reshape(1, hk), norm_g.reshape(1, hv))


_RWKV_STAGED = ("a_dec", "r_dec", "b_inv", "k_inv", "b_end", "k_end", "v", "gate", "bonus")


def _rwkv_body(p_ref, mu_ref, w0_ref, w2_ref, a0_ref, a2_ref, g2_ref, kk_ref, ka_ref, rk_ref, lng_ref, lnb_ref,
               seg_ref, o_ref, sp_ref, prev_ref, st_ref, gl_ref):
    L = RWKV_CHUNK
    W = RWKV_WIDTH
    N = RWKV_N
    nb = p_ref.shape[0]
    n_chunks = p_ref.shape[1] // L
    n_pairs = RWKV_HEADS // 2

    @pl.when(pl.program_id(1) == 0)
    def _():
        sp_ref[...] = jnp.zeros_like(sp_ref)
        prev_ref[...] = jnp.zeros_like(prev_ref)

    M = nb * L
    HW = W // 2

    def seg_sum(x):
        parts = [piece[:, s * HW:(s + 1) * HW] for piece in _split3(x) for s in range(2)]
        out = jnp.dot(jnp.concatenate(parts, axis=0), seg_ref[...], preferred_element_type=F32)
        m = x.shape[0]
        halves = [out[s * m:(s + 1) * m] + out[(2 + s) * m:(3 + s) * m] + out[(4 + s) * m:(5 + s) * m]
                  for s in range(2)]
        return jnp.concatenate(halves, axis=1)

    def prepare(c, slot):
        rows = pl.ds(pl.multiple_of(c * L, L), L)
        p = jnp.concatenate([p_ref[b, rows, :] for b in range(nb)], axis=0)
        rolled = pltpu.roll(p, 1, 0)
        first = lax.broadcasted_iota(jnp.int32, (SUBLANES, p.shape[1]), 0) == 0
        pieces = []
        for b in range(nb):
            pieces.append(jnp.where(first, prev_ref[b, 0:1, :], rolled[b * L:b * L + SUBLANES, :]))
            pieces.append(rolled[b * L + SUBLANES:(b + 1) * L, :])
            prev_ref[b, 0:1, :] = p[(b + 1) * L - 1:(b + 1) * L, :]
        sh = jnp.concatenate(pieces, axis=0)
        pf = p + (sh - p) * mu_ref[...]
        r = pf[:, 0:W]
        k = pf[:, W:2 * W]
        v = pf[:, 2 * W:3 * W]
        wa = pf[:, 3 * W:3 * W + LANES]
        g_lr = pf[:, 3 * W + LANES:3 * W + 2 * LANES]
        log_w = -RWKV_DECAY_SCALE * jax.nn.sigmoid(w0_ref[...] + _mm(jnp.tanh(wa), w2_ref[...]))
        a = jax.nn.sigmoid(a0_ref[...] + _mm(wa, a2_ref[...]))
        gate = _mm(jax.nn.sigmoid(g_lr), g2_ref[...])
        yield
        kk = k * kk_ref[...]
        kk = kk * lax.rsqrt(seg_sum(kk * kk) + RWKV_L2_EPS)
        yield
        k = k * (1.0 + (a - 1.0) * ka_ref[...])
        b_vec = kk * a
        ri = lax.broadcasted_iota(jnp.int32, (M, M), 0)
        ci = lax.broadcasted_iota(jnp.int32, (M, M), 1)
        same_seq = jnp.bitwise_and(ri, -L) == jnp.bitwise_and(ci, -L)
        tri01 = jnp.where(same_seq & (ri >= ci), 1.0, 0.0).astype(BF16)
        g = _mm_exact_lhs01(tri01, log_w)
        yield
        g_last = jnp.concatenate([jnp.broadcast_to(g[(b + 1) * L - 1:(b + 1) * L, :], (L, W)) for b in range(nb)],
                                 axis=0)
        e_neg = jnp.exp(-g)
        e_end = jnp.exp(g_last - g)
        bonus = seg_sum(r * k * rk_ref[...]) * v
        yield
        vals = dict(a_dec=-kk * jnp.exp(g - log_w), r_dec=r * jnp.exp(g), b_inv=b_vec * e_neg, k_inv=k * e_neg,
                    b_end=b_vec * e_end, k_end=k * e_end, v=v, gate=gate, bonus=bonus)
        for j, name in enumerate(_RWKV_STAGED):
            st_ref[slot, j] = vals[name]
        for b in range(nb):
            gl_ref[slot, b] = g[(b + 1) * L - SUBLANES:(b + 1) * L, :]
        yield

    def recur(c, slot, filler):
        def tick():
            if filler is not None:
                next(filler, None)

        lane = lax.broadcasted_iota(jnp.int32, (L, LANES), 1)
        rowi = lax.broadcasted_iota(jnp.int32, (L, LANES), 0)
        lo = lane < N
        hi = lane >= N
        colm = jnp.bitwise_and(lane, N - 1)
        strict2 = rowi > colm
        incl2 = rowi >= colm
        r2 = lax.broadcasted_iota(jnp.int32, (LANES, LANES), 0) < N
        c2 = lax.broadcasted_iota(jnp.int32, (LANES, LANES), 1) < N
        diag = r2 == c2
        nat = (lo, hi)
        oth = (hi, lo)
        sel = lambda m, x: jnp.where(m, x, 0.0)
        swap = lambda x: pltpu.roll(x, N, 1)
        cat0 = lambda xs: jnp.concatenate(xs, axis=0)
        dot = lambda x, y: jnp.dot(x, y, preferred_element_type=F32)
        zeros_b = jnp.zeros((L, LANES), BF16)
        rows = pl.ds(pl.multiple_of(c * L, L), L)
        U = [(b, p) for b in range(nb) for p in range(n_pairs)]
        HH = range(2)
        idx = {name: j for j, name in enumerate(_RWKV_STAGED)}
        blk = lambda name, u: st_ref[slot, idx[name], u[0] * L:(u[0] + 1) * L, u[1] * LANES:(u[1] + 1) * LANES]
        ad = [blk("a_dec", u) for u in U]
        rd = [blk("r_dec", u) for u in U]
        ad_sw = [swap(x) for x in ad]
        rd_sw = [swap(x) for x in rd]
        be_sw = [swap(blk("b_end", u)) for u in U]
        ke_sw = [swap(blk("k_end", u)) for u in U]
        sd_sw = [jnp.exp(swap(gl_ref[slot, u[0], :, u[1] * LANES:(u[1] + 1) * LANES])[SUBLANES - 1:SUBLANES, :])
                 for u in U]
        vn = [[sel(nat[h], blk("v", u)).astype(BF16) for h in HH] for u in U]
        nu = range(len(U))

        b_inv = [blk("b_inv", u) for u in U]
        k_inv = [blk("k_inv", u) for u in U]
        pe = [_mm_nt(cat0([sel(lo, ad[i]), sel(lo, rd[i])]), cat0([b_inv[i], k_inv[i]])) for i in nu]
        po = [_mm_nt(cat0([sel(hi, ad[i]), sel(hi, rd[i])]), cat0([k_inv[i], b_inv[i]])) for i in nu]
        tick()
        aa = [[sel(strict2, pe[i][0:L, :]), sel(strict2, po[i][0:L, :])] for i in nu]
        ar = [[sel(incl2, pe[i][L:2 * L, :]).astype(BF16), sel(incl2, po[i][L:2 * L, :]).astype(BF16)] for i in nu]
        t0 = [dot(jnp.where(lo, aa[i][1], aa[i][0]).astype(BF16),
                  cat0([jnp.concatenate([vn[i][1], zeros_b], axis=1), jnp.concatenate([zeros_b, vn[i][0]], axis=1)]))
              for i in nu]
        eye2 = jnp.where(rowi == colm, 1.0, 0.0)
        q0 = [jnp.where(lo, aa[i][0], eye2) for i in nu]
        q1 = [jnp.where(lo, eye2, aa[i][1]) for i in nu]
        for _ in range(6):
            out = [dot(jnp.where(lo, q0[i], q1[i]).astype(BF16),
                       cat0([jnp.concatenate([q0[i].astype(BF16), zeros_b], axis=1),
                             jnp.concatenate([zeros_b, q1[i].astype(BF16)], axis=1)])) for i in nu]
            tick()
            q0 = [out[i][:, 0:LANES] + sel(hi, q0[i]) for i in nu]
            q1 = [out[i][:, LANES:2 * LANES] + sel(lo, q1[i]) for i in nu]
        x0 = [[jnp.where(hi, ad_sw[i], t0[i][:, LANES:2 * LANES]).astype(BF16),
               jnp.where(lo, ad_sw[i], t0[i][:, 0:LANES]).astype(BF16)] for i in nu]
        tx = [dot(jnp.where(lo, q1[i], q0[i]).astype(BF16),
                  cat0([jnp.concatenate([x0[i][1], zeros_b], axis=1), jnp.concatenate([zeros_b, x0[i][0]], axis=1)]))
              for i in nu]
        tick()
        xb = [[tx[i][:, LANES:2 * LANES].astype(BF16), tx[i][:, 0:LANES].astype(BF16)] for i in nu]
        rmy = [[sel(hi, rd_sw[i]) + dot(ar[i][0], cat0([xb[i][0], vn[i][0]])),
                sel(lo, rd_sw[i]) + dot(ar[i][1], cat0([vn[i][1], xb[i][1]]))] for i in nu]
        gg = [_mm_tn(cat0([xb[i][0], xb[i][1], vn[i][0], vn[i][1]]),
                     cat0([sel(hi, be_sw[i]), sel(lo, be_sw[i]), sel(hi, ke_sw[i]), sel(lo, ke_sw[i])]))
              for i in nu]
        tick()
        sp = [sp_ref[i] for i in nu]
        ys = [_mm_nt(jnp.where(lo, rmy[i][1], rmy[i][0]), sp[i]) + jnp.where(lo, rmy[i][0], rmy[i][1]) for i in nu]
        for i in nu:
            sp_ref[i] = sp[i] * sd_sw[i] + _mm(sp[i], sel(diag, gg[i])) + jnp.where(diag, 0.0, gg[i])
        if filler is not None:
            for _ in filler:
                pass
        y = jnp.concatenate([jnp.concatenate(ys[b * n_pairs:(b + 1) * n_pairs], axis=1) for b in range(nb)], axis=0)
        mean = seg_sum(y) * (1.0 / N)
        dy = y - mean
        var = seg_sum(dy * dy) * (1.0 / N)
        y = dy * lax.rsqrt(var + RWKV_GN_EPS) * lng_ref[...] + lnb_ref[...]
        y = (y + st_ref[slot, idx["bonus"]]) * st_ref[slot, idx["gate"]]
        for b in range(nb):
            o_ref[b, rows, :] = y[b * L:(b + 1) * L, :]

    for _ in prepare(0, 0):
        pass

    def body(c, carry):
        slot = jnp.bitwise_and(c, 1)
        recur(c, slot, prepare(c + 1, 1 - slot))
        return carry

    lax.fori_loop(0, n_chunks - 1, body, 0)
    recur(n_chunks - 1, (n_chunks - 1) % 2, None)


def rwkv_mixer(pb, mu, w0, w2, a0, a2, g2, k_k, k_a, r_k, ln_g, ln_b):
    b, t, cols = pb.shape
    tt = min(RWKV_TILE, t)
    nb = min(RWKV_BATCH_PER_STEP, b)
    W = RWKV_WIDTH
    row = lambda x: x.reshape(1, -1)
    w2p = jnp.zeros((LANES, W), BF16).at[:RWKV_LORA_W].set(w2.astype(BF16))
    a2p = jnp.zeros((LANES, W), BF16).at[RWKV_LORA_W:].set(a2.astype(BF16))
    head_of = jnp.arange(W // 2) // RWKV_N
    seg = (head_of[:, None] == head_of[None, :]).astype(BF16)
    consts = [row(mu), row(w0), w2p, row(a0), a2p, g2.astype(BF16), row(k_k), row(k_a), row(r_k), row(ln_g),
              row(ln_b), seg]
    return pl.pallas_call(
        _rwkv_body,
        out_shape=jax.ShapeDtypeStruct((b, t, W), F32),
        grid=(b // nb, t // tt),
        in_specs=[pl.BlockSpec((nb, tt, cols), lambda i, j: (i, j, 0))] + [_const_spec(x.shape) for x in consts],
        out_specs=pl.BlockSpec((nb, tt, W), lambda i, j: (i, j, 0)),
        scratch_shapes=[pltpu.VMEM((nb * RWKV_HEADS // 2, LANES, LANES), F32),
                        pltpu.VMEM((nb, SUBLANES, cols), F32),
                        pltpu.VMEM((2, len(_RWKV_STAGED), nb * RWKV_CHUNK, W), F32),
                        pltpu.VMEM((2, nb, SUBLANES, W), F32)],
        compiler_params=_params(("parallel", "arbitrary")),
        name="rwkv_mixer",
    )(pb, *consts)


def _rglru_body(p_ref, cw_ref, cb_ref, gw_ref, gb_ref, lam_ref, o_ref, xprev_ref, hprev_ref):
    tt = p_ref.shape[1]
    W = LRU_WIDTH

    @pl.when(pl.program_id(1) == 0)
    def _():
        xprev_ref[0:SUBLANES, :] = jnp.zeros((SUBLANES, W), F32)
        hprev_ref[...] = jnp.zeros_like(hprev_ref)

    x = p_ref[0, :, 0:W]
    gate = p_ref[0, :, W:2 * W]
    xc = _causal_conv(x, xprev_ref, cw_ref[...], cb_ref[...], LRU_CONV)
    xcb = xc.astype(BF16)
    pre = []
    for gi in range(2):
        pre.append(jnp.concatenate(
            [jnp.dot(xcb[:, n * LRU_BLOCK:(n + 1) * LRU_BLOCK], gw_ref[gi, n], preferred_element_type=F32)
             for n in range(LRU_BLOCKS)], axis=1) + gb_ref[gi:gi + 1, :])
    r_gate = jax.nn.sigmoid(pre[0])
    i_gate = jax.nn.sigmoid(pre[1])
    log_a = -LRU_C * r_gate * _softplus(-lam_ref[...])
    a = jnp.exp(log_a)
    w2 = -jnp.tanh(log_a) * (a * a + 1.0)
    u = jnp.where(w2 > 0.0, w2 * lax.rsqrt(w2), 0.0) * (i_gate * xc)
    row_in_group = jnp.bitwise_and(lax.broadcasted_iota(jnp.int32, (tt, W), 0), SUBLANES - 1)
    d = 1
    while d < SUBLANES:
        keep = row_in_group >= d
        u = u + a * jnp.where(keep, pltpu.roll(u, d, 0), 0.0)
        a = a * jnp.where(keep, pltpu.roll(a, d, 0), 1.0)
        d *= 2
    carry = hprev_ref[0:1, :]
    hs = []
    for g in range(tt // SUBLANES):
        rs = slice(g * SUBLANES, (g + 1) * SUBLANES)
        hg = u[rs, :] + a[rs, :] * carry
        carry = hg[SUBLANES - 1:SUBLANES, :]
        hs.append(hg)
    hprev_ref[0:1, :] = carry
    o_ref[0] = jnp.concatenate(hs, axis=0) * jax.nn.gelu(gate)


def rglru_mixer(pc, conv_w, conv_b, gate_w, gate_b, lam):
    b, t, cols = pc.shape
    tt = min(MIX_TILE, t)
    W = LRU_WIDTH
    cw = jnp.zeros((SUBLANES, W), F32).at[:LRU_CONV].set(conv_w)
    consts = [cw, conv_b.reshape(1, W), gate_w.astype(BF16), gate_b, lam.reshape(1, W)]
    return pl.pallas_call(
        _rglru_body,
        out_shape=jax.ShapeDtypeStruct((b, t, W), F32),
        grid=(b, t // tt),
        in_specs=[pl.BlockSpec((1, tt, cols), lambda i, j: (i, j, 0))] + [_const_spec(x.shape) for x in consts],
        out_specs=pl.BlockSpec((1, tt, W), lambda i, j: (i, j, 0)),
        scratch_shapes=[pltpu.VMEM((SUBLANES + tt, W), F32), pltpu.VMEM((SUBLANES, W), F32)],
        compiler_params=_params(("parallel", "arbitrary")),
        name="rglru_mixer",
    )(pc, *consts)


def _mlstm_body(p_ref, cw_ref, cb_ref, wq_ref, wk_ref, wv_ref, bif_ref, ng_ref, o_ref,
                xprev_ref, c_ref, m_ref, q_s, k_s, v_s):
    tt = p_ref.shape[1]
    L = MLSTM_CHUNK
    W = MLSTM_WIDTH
    DH = MLSTM_DH
    H = MLSTM_HEADS
    n_chunks = tt // L

    @pl.when(pl.program_id(1) == 0)
    def _():
        xprev_ref[:, 0:SUBLANES, :] = jnp.zeros((xprev_ref.shape[0], SUBLANES, W), F32)
        c_ref[...] = jnp.zeros_like(c_ref)
        m_ref[...] = jnp.zeros_like(m_ref)

    nb = p_ref.shape[0]
    for b in range(nb):
        x = p_ref[b, :, 0:W]
        xc = _silu(_causal_conv(x, xprev_ref.at[b], cw_ref[...], cb_ref[...], MLSTM_CONV))
        xcb = xc.astype(BF16)
        q_s[b] = jnp.dot(xcb, wq_ref[...], preferred_element_type=F32)
        k_s[b] = jnp.dot(xcb, wk_ref[...], preferred_element_type=F32) * DH ** -0.5
        v_s[b] = jnp.dot(x.astype(BF16), wv_ref[...], preferred_element_type=F32)

    def chunk(c, carry):
        incl, _ = _tri_masks(L)
        tri01 = jnp.where(incl, 1.0, 0.0).astype(BF16)
        t0 = pl.multiple_of(c * L, L)
        rows = pl.ds(t0, L)
        nt = lambda y: lax.dot_general(y, tri01, (((1,), (1,)), ((), ())), preferred_element_type=F32)
        gates, lsg, g_t, b_rows = [], [], [], []
        for b in range(nb):
            gates.append(p_ref[b, rows, 2 * W:2 * W + LANES] + bif_ref[...])
            lsg.append(_log_sigmoid(gates[b]))
            g_t.append(gates[b].T[0:SUBLANES, :])
            x1, x2, x3 = _split3(_log_sigmoid(g_t[b]))
            b_rows.append(nt(x1) + nt(x2) + nt(x3))
        ones_b = jnp.ones((L, LANES), BF16)
        U = [(b, h) for b in range(nb) for h in range(H)]
        nu = range(len(U))
        hs = [slice(h * DH, (h + 1) * DH) for h in range(H)]
        qh = [q_s[b, rows, hs[h]] for b, h in U]
        kh = [k_s[b, rows, hs[h]] for b, h in U]
        v_aug = [jnp.concatenate([v_s[b, rows, hs[h]].astype(BF16), ones_b], axis=1) for b, h in U]
        cm = [c_ref[i] for i in nu]
        m = [m_ref[i, 0:1, :] for i in nu]
        i_rep = [jnp.broadcast_to(gates[b][:, h:h + 1], (L, LANES)) for b, h in U]
        lf_rep = [jnp.broadcast_to(lsg[b][:, H + h:H + h + 1], (L, LANES)) for b, h in U]
        b_rep = [_mm_exact_lhs01(tri01, lf_rep[i]) for i in nu]
        qk = [_mm_nt(qh[i], kh[i]) for i in nu]
        b_last = [b_rep[i][L - 1:L, :] for i in nu]
        log_e = [b_last[i] - b_rep[i] + i_rep[i] for i in nu]
        m_end = [jnp.max(log_e[i], axis=0, keepdims=True) for i in nu]
        m_new = [jnp.maximum(b_last[i] + m[i], m_end[i]) for i in nu]
        kin = [kh[i] * (jnp.exp(m_end[i] - m_new[i]) * jnp.exp(log_e[i] - m_end[i])) for i in nu]
        c_upd = [_mm_tn(kin[i], v_aug[i]) for i in nu]
        run = [i_rep[i] - b_rep[i] for i in nu]
        d = 1
        while d < L:
            run = [jnp.maximum(run[i], _shift_rows(run[i], d, -jnp.inf)) for i in nu]
            d *= 2
        m_t = [jnp.maximum(b_rep[i] + m[i], b_rep[i] + run[i]) for i in nu]
        log_d = [jnp.where(incl, b_rep[i][:, 0:L] - b_rows[b][H + h:H + h + 1, :] + g_t[b][h:h + 1, :], -jnp.inf)
                 for i, (b, h) in enumerate(U)]
        w_loc = [jnp.exp(log_d[i] - m_t[i][:, 0:L]) * qk[i] for i in nu]
        lhs = [jnp.concatenate([(jnp.exp(b_rep[i] + m[i] - m_t[i]) * qh[i]).astype(BF16), w_loc[i].astype(BF16)],
                               axis=1) for i in nu]
        nd = [jnp.dot(lhs[i], jnp.concatenate([cm[i].astype(BF16), v_aug[i]], axis=0),
                      preferred_element_type=F32) for i in nu]
        for i, (b, h) in enumerate(U):
            c_state = jnp.exp(b_last[i] + m[i] - m_new[i])
            c_ref[i] = jnp.concatenate([c_state, c_state], axis=1) * cm[i] + c_upd[i]
            m_ref[i] = jnp.broadcast_to(m_new[i], (SUBLANES, LANES))
            hv = nd[i][:, 0:DH] / jnp.maximum(jnp.abs(nd[i][:, DH:2 * DH]), jnp.exp(-m_t[i]))
            hv = hv * lax.rsqrt(jnp.mean(hv * hv, axis=-1, keepdims=True) + NORM_EPS)
            o_ref[b, rows, hs[h]] = (jax.nn.sigmoid(p_ref[b, rows, W + h * DH:W + (h + 1) * DH]) * hv
                                     * ng_ref[:, hs[h]])
        return carry

    lax.fori_loop(0, n_chunks, chunk, 0)


def _block_diag(w):
    nb, d, e = w.shape
    eye = jnp.eye(nb, dtype=w.dtype)
    return (eye[:, None, :, None] * w[:, :, None, :]).reshape(nb * d, nb * e)


def mlstm_mixer(pd, conv_w, conv_b, qkv_w, b_if, norm_g):
    b, t, cols = pd.shape
    tt = min(MIX_TILE, t)
    nb = min(MLSTM_BATCH_PER_STEP, b)
    W = MLSTM_WIDTH
    cw = jnp.zeros((SUBLANES, W), F32).at[:MLSTM_CONV].set(conv_w)
    bif = jnp.zeros((1, LANES), F32).at[0, :2 * MLSTM_HEADS].set(b_if)
    consts = [cw, conv_b.reshape(1, W)] + [_block_diag(qkv_w[i]).astype(BF16) for i in range(3)] + [
        bif, norm_g.reshape(1, W)]
    return pl.pallas_call(
        _mlstm_body,
        out_shape=jax.ShapeDtypeStruct((b, t, W), F32),
        grid=(b // nb, t // tt),
        in_specs=[pl.BlockSpec((nb, tt, cols), lambda i, j: (i, j, 0))] + [_const_spec(x.shape) for x in consts],
        out_specs=pl.BlockSpec((nb, tt, W), lambda i, j: (i, j, 0)),
        scratch_shapes=[pltpu.VMEM((nb, SUBLANES + tt, W), F32),
                        pltpu.VMEM((nb * MLSTM_HEADS, MLSTM_DH, 2 * MLSTM_DH), F32),
                        pltpu.VMEM((nb * MLSTM_HEADS, SUBLANES, LANES), F32),
                        pltpu.VMEM((nb, tt, W), F32), pltpu.VMEM((nb, tt, W), F32), pltpu.VMEM((nb, tt, W), F32)],
        compiler_params=_params(("parallel", "arbitrary")),
        name="mlstm_mixer",
    )(pd, *consts)


def _mix_xattn_body(h_ref, ya_ref, yb_ref, wm_ref, g_ref, wq_ref, k_ref, v_ref, wo_ref, o_ref):
    mixed = jnp.concatenate([ya_ref[0], yb_ref[0]], axis=1).astype(BF16)
    h = h_ref[0] + jnp.dot(mixed, wm_ref[...], preferred_element_type=F32)
    xn = _rmsnorm(h, g_ref[...]).astype(BF16)
    q = jnp.dot(xn, wq_ref[...], preferred_element_type=F32)
    outs = []
    for hd in range(XA_HEADS):
        hs = slice(hd * XA_DH, (hd + 1) * XA_DH)
        s = _mm_nt(q[:, hs], k_ref[0, :, hs]) * XA_DH ** -0.5
        e = jnp.exp(s - jnp.max(s, axis=-1, keepdims=True))
        p = e / jnp.sum(e, axis=-1, keepdims=True)
        outs.append(_mm(p, v_ref[0, :, hs]))
    o = jnp.concatenate(outs, axis=1).astype(BF16)
    o_ref[0] = h + jnp.dot(o, wo_ref[...], preferred_element_type=F32)


def mix_xattn(h, ya, yb, w_mix, g, wq, kv, wo):
    b, t, d = h.shape
    tm = min(ROW_TILE, t)
    tok = lambda w: pl.BlockSpec((1, tm, w), lambda i, j: (i, j, 0))
    mem_k = pl.BlockSpec((1, MEM_LEN, d), lambda i, j: (i, 0, 0))
    mem_v = pl.BlockSpec((1, MEM_LEN, d), lambda i, j: (i, 0, 1))
    return pl.pallas_call(
        _mix_xattn_body,
        out_shape=jax.ShapeDtypeStruct((b, t, d), F32),
        grid=(b, t // tm),
        in_specs=[tok(d), tok(ya.shape[-1]), tok(yb.shape[-1]), _const_spec(w_mix.shape), _const_spec((1, d)),
                  _const_spec(wq.shape), mem_k, mem_v, _const_spec(wo.shape)],
        out_specs=tok(d),
        compiler_params=_params(("parallel", "parallel")),
        name="mix_xattn",
    )(h, ya, yb, w_mix, g.reshape(1, d), wq, kv, kv, wo)


def _ffn_body(final_norm, h_ref, g_ref, wu_ref, wg_ref, cw_ref, cb_ref, wd_ref, fg_ref, o_ref, uprev_ref):
    tm = h_ref.shape[1]

    @pl.when(pl.program_id(1) == 0)
    def _():
        uprev_ref[...] = jnp.zeros_like(uprev_ref)

    h = h_ref[0]
    xn = _rmsnorm(h, g_ref[...]).astype(BF16)
    acc = h
    start = 0
    for width in FFN_COL_GROUPS:
        cs = slice(start, start + width)
        start += width
        u = jnp.dot(xn, wu_ref[:, cs], preferred_element_type=F32)
        gt = jnp.dot(xn, wg_ref[:, cs], preferred_element_type=F32)
        uu = jnp.concatenate([uprev_ref[:, cs], u], axis=0)
        uprev_ref[:, cs] = u[tm - SUBLANES:tm, :]
        c = cb_ref[:, cs] + cw_ref[2:3, cs] * u
        c = c + cw_ref[1:2, cs] * pltpu.roll(uu, 1, 0)[SUBLANES:SUBLANES + tm, :]
        c = c + cw_ref[0:1, cs] * pltpu.roll(uu, 2, 0)[SUBLANES:SUBLANES + tm, :]
        act = (_silu(c) * gt).astype(BF16)
        acc = acc + jnp.dot(act, wd_ref[cs, :], preferred_element_type=F32)
    if final_norm:
        acc = _rmsnorm(acc, fg_ref[...])
    o_ref[0] = acc


def ffn(h, g, wu, wg, conv_w, conv_b, wd, final_g, final_norm):
    b, t, d = h.shape
    tm = min(ROW_TILE, t)
    tok = pl.BlockSpec((1, tm, d), lambda i, j: (i, j, 0))
    consts = [g.reshape(1, d), wu, wg, conv_w, conv_b, wd, final_g.reshape(1, d)]
    return pl.pallas_call(
        functools.partial(_ffn_body, final_norm),
        out_shape=jax.ShapeDtypeStruct((b, t, d), F32),
        grid=(b, t // tm),
        in_specs=[tok] + [_const_spec(x.shape) for x in consts],
        out_specs=tok,
        scratch_shapes=[pltpu.VMEM((SUBLANES, D_FF_PAD), F32)],
        compiler_params=_params(("parallel", "arbitrary")),
        name="ffn",
    )(h, *consts)


def _pad_cols(w, n):
    return jnp.pad(w, ((0, 0), (0, n - w.shape[1])))


def _rwkv_perm(x):
    W = RWKV_WIDTH
    o_w, o_k, o_v = W, W + RWKV_LORA_W, 2 * W + RWKV_LORA_W
    o_a = 3 * W + RWKV_LORA_W
    o_g = o_a + RWKV_LORA_A
    return jnp.concatenate([x[..., :W], x[..., o_k:o_k + W], x[..., o_v:o_v + W], x[..., o_w:o_w + RWKV_LORA_W],
                            x[..., o_a:o_a + RWKV_LORA_A], x[..., o_g:]], axis=-1)


def kernel(x, mem, mem_norm_g, norm_mix_g, ab_w_in, gla_w_alpha2, gla_b_alpha, gla_norm_g, rwkv_mu, rwkv_w0, rwkv_w2, rwkv_a0, rwkv_a2, rwkv_g2, rwkv_k_k, rwkv_k_a, rwkv_r_k, rwkv_ln_g, rwkv_ln_b, cd_w_in, lru_conv_w, lru_conv_b, lru_gate_w, lru_gate_b, lru_lambda, mlstm_conv_w, mlstm_conv_b, mlstm_qkv_w, mlstm_b_if, mlstm_norm_g, w_mix_out, norm_xattn_g, xattn_wq, xattn_wkv, xattn_wo, norm_ffn_g, ffn_w_up, ffn_conv_w, ffn_conv_b, ffn_w_down, final_norm_g):
    b, t, d = x.shape
    depth = norm_mix_g.shape[0]
    n = b * t
    h = x
    mem2d = mem.reshape(b * MEM_LEN, d)
    for layer in range(depth):
        j = layer // 2
        h2d = h.reshape(n, d)
        if layer % 2 == 0:
            w = ab_w_in[j]
            wa = _pad_cols(w[:, :GLA_COLS], GLA_COLS_PAD).astype(BF16)
            wb = _rwkv_perm(w[:, GLA_COLS:]).astype(BF16)
            pa, pb = norm_matmul(h2d, norm_mix_g[layer], [wa, wb], name="proj_ab")
            ya = gla_mixer(pa.reshape(b, t, -1), gla_w_alpha2[j], gla_b_alpha[j], gla_norm_g[j])
            yb = rwkv_mixer(pb.reshape(b, t, -1), _rwkv_perm(rwkv_mu[j]), rwkv_w0[j], rwkv_w2[j], rwkv_a0[j],
                            rwkv_a2[j], rwkv_g2[j], rwkv_k_k[j], rwkv_k_a[j], rwkv_r_k[j], rwkv_ln_g[j],
                            rwkv_ln_b[j])
        else:
            w = cd_w_in[j]
            wc = w[:, :2 * LRU_WIDTH].astype(BF16)
            wd_ = _pad_cols(w[:, 2 * LRU_WIDTH:], MLSTM_COLS_PAD).astype(BF16)
            pc, pd = norm_matmul(h2d, norm_mix_g[layer], [wc, wd_], name="proj_cd")
            ya = rglru_mixer(pc.reshape(b, t, -1), lru_conv_w[j], lru_conv_b[j], lru_gate_w[j], lru_gate_b[j],
                             lru_lambda[j])
            yb = mlstm_mixer(pd.reshape(b, t, -1), mlstm_conv_w[j], mlstm_conv_b[j], mlstm_qkv_w[j],
                             mlstm_b_if[j], mlstm_norm_g[j])
        (kv,) = norm_matmul(mem2d, mem_norm_g, [xattn_wkv[layer].astype(BF16)], out_dtype=BF16, name="proj_kv")
        kv = kv.reshape(b, MEM_LEN, 2 * d)
        h = mix_xattn(h, ya, yb, w_mix_out[layer].astype(BF16), norm_xattn_g[layer], xattn_wq[layer].astype(BF16),
                      kv, xattn_wo[layer].astype(BF16))
        wup = ffn_w_up[layer]
        wu = _pad_cols(wup[:, :D_FF], D_FF_PAD).astype(BF16)
        wg = _pad_cols(wup[:, D_FF:], D_FF_PAD).astype(BF16)
        cw = _pad_cols(jnp.pad(ffn_conv_w[layer], ((0, SUBLANES - FFN_CONV), (0, 0))), D_FF_PAD)
        cb = _pad_cols(ffn_conv_b[layer].reshape(1, D_FF), D_FF_PAD)
        wdn = jnp.pad(ffn_w_down[layer], ((0, D_FF_PAD - D_FF), (0, 0))).astype(BF16)
        h = ffn(h, norm_ffn_g[layer], wu, wg, cw, cb, wdn, final_norm_g, layer == depth - 1)
    return h
```

```python
import functools

import jax
import jax.numpy as jnp
from jax import lax
from jax.experimental import pallas as pl
from jax.experimental.pallas import tpu as pltpu

F32 = jnp.float32
BF16 = jnp.bfloat16

D_MODEL = 1024
NORM_EPS = 1e-6
LANES = 128
SUBLANES = 8
VMEM_LIMIT_BYTES = 56 * 1024 * 1024

GLA_HEADS, GLA_DK, GLA_DV, GLA_RANK, GLA_TAU, GLA_CHUNK = 4, 64, 128, 16, 16.0, 64
GLA_COLS = 2 * GLA_HEADS * GLA_DK + 2 * GLA_HEADS * GLA_DV + GLA_RANK
GLA_COLS_PAD = 13 * LANES

RWKV_HEADS, RWKV_N, RWKV_WIDTH = 8, 64, 512
RWKV_LORA_W, RWKV_LORA_A, RWKV_LORA_G = 64, 64, 128
RWKV_CHUNK = 64
RWKV_DECAY_SCALE = 0.6065306597126334
RWKV_GN_EPS = RWKV_N * 1e-5
RWKV_L2_EPS = 1e-12
RWKV_COLS = 3 * RWKV_WIDTH + RWKV_LORA_W + RWKV_LORA_A + RWKV_LORA_G

LRU_WIDTH, LRU_BLOCKS, LRU_BLOCK, LRU_C, LRU_CONV = 512, 4, 128, 8.0, 4
MLSTM_HEADS, MLSTM_DH, MLSTM_WIDTH, MLSTM_CONV, MLSTM_CHUNK = 4, 128, 512, 4, 64
MLSTM_COLS_PAD = 2 * MLSTM_WIDTH + LANES

XA_HEADS, XA_DH, MEM_LEN = 4, 256, 256
D_FF, FFN_CONV = 2752, 3
D_FF_PAD = 22 * LANES
MXU_K_TILE = 256
FFN_COL_GROUPS = (6 * MXU_K_TILE, 5 * MXU_K_TILE)

MIX_TILE = 256
RWKV_TILE = 512
GLA_BATCH_PER_STEP = 4
MLSTM_BATCH_PER_STEP = 2
RWKV_BATCH_PER_STEP = 2
ROW_TILE = 512
XATTN_TILE = 1024


def _mm(a, b):
    return jnp.dot(a.astype(BF16), b.astype(BF16), preferred_element_type=F32)


def _mm_nt(a, b):
    return lax.dot_general(a.astype(BF16), b.astype(BF16), (((1,), (1,)), ((), ())), preferred_element_type=F32)


def _mm_tn(a, b):
    return lax.dot_general(a.astype(BF16), b.astype(BF16), (((0,), (0,)), ((), ())), preferred_element_type=F32)


def _split3(x):
    x1 = x.astype(BF16)
    r1 = x - x1.astype(F32)
    x2 = r1.astype(BF16)
    x3 = (r1 - x2.astype(F32)).astype(BF16)
    return x1, x2, x3


def _mm_exact_lhs01(m01, x):
    x1, x2, x3 = _split3(x)
    d = lambda y: jnp.dot(m01, y, preferred_element_type=F32)
    return d(x1) + d(x2) + d(x3)


def _mm_exact_rhs01(x, m01):
    x1, x2, x3 = _split3(x)
    d = lambda y: jnp.dot(y, m01, preferred_element_type=F32)
    return d(x1) + d(x2) + d(x3)


def _rmsnorm(x, g):
    return x * lax.rsqrt(jnp.mean(x * x, axis=-1, keepdims=True) + NORM_EPS) * g


def _log_sigmoid(x):
    return jnp.minimum(x, 0.0) - jnp.log1p(jnp.exp(-jnp.abs(x)))


def _softplus(x):
    return jnp.maximum(x, 0.0) + jnp.log1p(jnp.exp(-jnp.abs(x)))


def _silu(x):
    return x * jax.nn.sigmoid(x)


def _tri_masks(n):
    r = lax.broadcasted_iota(jnp.int32, (n, n), 0)
    c = lax.broadcasted_iota(jnp.int32, (n, n), 1)
    return r >= c, r > c


def _shift_rows(x, s, fill):
    if s == 0:
        return x
    rolled = pltpu.roll(x, s, 0)
    row = lax.broadcasted_iota(jnp.int32, x.shape, 0)
    return jnp.where(row >= s, rolled, fill)


def _causal_conv(x, buf_ref, w, b, width):
    t = x.shape[0]
    buf_ref[SUBLANES:SUBLANES + t, :] = x
    y = b + w[width - 1:width, :] * x
    for j in range(width - 1):
        s = width - 1 - j
        y = y + w[j:j + 1, :] * buf_ref[SUBLANES - s:SUBLANES - s + t, :]
    buf_ref[0:SUBLANES, :] = x[t - SUBLANES:t, :]
    return y


def _const_spec(shape):
    nd = len(shape)
    return pl.BlockSpec(shape, lambda *_: (0,) * nd, pipeline_mode=pl.Buffered(1))


def _params(sem):
    return pltpu.CompilerParams(dimension_semantics=sem, vmem_limit_bytes=VMEM_LIMIT_BYTES)


def _norm_matmul_body(n_out, x_ref, g_ref, *refs):
    xn = _rmsnorm(x_ref[...], g_ref[...]).astype(BF16)
    for w_ref, o_ref in zip(refs[:n_out], refs[n_out:]):
        o_ref[...] = jnp.dot(xn, w_ref[...], preferred_element_type=F32).astype(o_ref.dtype)


def norm_matmul(x2d, g, ws, out_dtype=F32, name="norm_matmul"):
    n, d = x2d.shape
    tm = min(ROW_TILE, n)
    assert n % tm == 0
    return pl.pallas_call(
        functools.partial(_norm_matmul_body, len(ws)),
        out_shape=[jax.ShapeDtypeStruct((n, w.shape[1]), out_dtype) for w in ws],
        grid=(n // tm,),
        in_specs=[pl.BlockSpec((tm, d), lambda i: (i, 0)), _const_spec((1, d))]
        + [_const_spec(w.shape) for w in ws],
        out_specs=[pl.BlockSpec((tm, w.shape[1]), lambda i: (i, 0)) for w in ws],
        compiler_params=_params(("parallel",)),
        name=name,
    )(x2d, g.reshape(1, d), *ws)


def _gla_body(p_ref, wa_ref, ba_ref, ng_ref, o_ref, st_ref):
    L = GLA_CHUNK
    nb = p_ref.shape[0]
    n_chunks = p_ref.shape[1] // L
    M = nb * L
    hk = GLA_HEADS * GLA_DK
    hv = GLA_HEADS * GLA_DV

    @pl.when(pl.program_id(1) == 0)
    def _():
        st_ref[...] = jnp.zeros_like(st_ref)

    def chunk(c, carry):
        incl, _ = _tri_masks(L)
        ri = lax.broadcasted_iota(jnp.int32, (M, M), 0)
        ci = lax.broadcasted_iota(jnp.int32, (M, M), 1)
        same_seq = jnp.bitwise_and(ri, -L) == jnp.bitwise_and(ci, -L)
        tri01 = jnp.where(same_seq & (ri >= ci), 1.0, 0.0).astype(BF16)
        t0 = pl.multiple_of(c * L, L)
        rows = pl.ds(t0, L)
        cols = lambda lo, hi: jnp.concatenate([p_ref[b, rows, lo:hi] for b in range(nb)], axis=0)
        q = cols(0, hk)
        k = cols(hk, 2 * hk)
        v = cols(2 * hk, 2 * hk + hv)
        gt = cols(2 * hk + hv, 2 * hk + 2 * hv)
        a_lr = cols(2 * hk + 2 * hv, GLA_COLS_PAD)
        la = _log_sigmoid(_mm(a_lr, wa_ref[...]) + ba_ref[...]) * (1.0 / GLA_TAU)
        g = _mm_exact_lhs01(tri01, la)
        g_last = jnp.concatenate([jnp.broadcast_to(g[(b + 1) * L - 1:(b + 1) * L, :], (L, hk)) for b in range(nb)],
                                 axis=0)
        q_dec = (q * GLA_DK ** -0.5) * jnp.exp(g)
        k_inv = k * jnp.exp(-g)
        k_end = k * jnp.exp(g_last - g)
        sd = [jnp.exp(g[(b + 1) * L - 1:(b + 1) * L, :]) for b in range(nb)]
        U = [(b, h) for b in range(nb) for h in range(GLA_HEADS)]
        nu = range(len(U))
        rb = [slice(b * L, (b + 1) * L) for b in range(nb)]
        ks = [slice(h * GLA_DK, (h + 1) * GLA_DK) for h in range(GLA_HEADS)]
        vs = [slice(h * GLA_DV, (h + 1) * GLA_DV) for h in range(GLA_HEADS)]
        qd = [q_dec[rb[b], ks[h]] for b, h in U]
        vh = [v[rb[b], vs[h]] for b, h in U]
        st = [st_ref[i] for i in nu]
        sc = [jnp.where(incl, _mm_nt(qd[i], k_inv[rb[b], ks[h]]), 0.0) for i, (b, h) in enumerate(U)]
        o_state = [_mm_nt(qd[i], st[i]) for i in nu]
        upd = [_mm_tn(vh[i], k_end[rb[b], ks[h]]) for i, (b, h) in enumerate(U)]
        o_loc = [_mm(sc[i], vh[i]) for i in nu]
        for i, (b, h) in enumerate(U):
            st_ref[i] = st[i] * sd[b][:, ks[h]] + upd[i]
            o = o_loc[i] + o_state[i]
            o = o * lax.rsqrt(jnp.mean(o * o, axis=-1, keepdims=True) + NORM_EPS)
            o_ref[b, rows, vs[h]] = o * ng_ref[:, vs[h]] * _silu(gt[rb[b], vs[h]])
        return carry

    lax.fori_loop(0, n_chunks, chunk, 0)


def gla_mixer(pa, w_alpha2, b_alpha, norm_g):
    b, t, _ = pa.shape
    tt = min(MIX_TILE, t)
    nb = min(GLA_BATCH_PER_STEP, b)
    hk, hv = GLA_HEADS * GLA_DK, GLA_HEADS * GLA_DV
    wa = jnp.zeros((LANES, hk), BF16).at[:GLA_RANK].set(w_alpha2.astype(BF16))
    return pl.pallas_call(
        _gla_body,
        out_shape=jax.ShapeDtypeStruct((b, t, hv), F32),
        grid=(b // nb, t // tt),
        in_specs=[pl.BlockSpec((nb, tt, GLA_COLS_PAD), lambda i, j: (i, j, 0)),
                  _const_spec((LANES, hk)), _const_spec((1, hk)), _const_spec((1, hv))],
        out_specs=pl.BlockSpec((nb, tt, hv), lambda i, j: (i, j, 0)),
        scratch_shapes=[pltpu.VMEM((nb * GLA_HEADS, GLA_DV, GLA_DK), F32)],
        compiler_params=_params(("parallel", "arbitrary")),
        name="gla_mixer",
    )(pa, wa, b_alpha.reshape(1, hk), norm_g.reshape(1, hv))


_RWKV_STAGED = ("a_dec", "r_dec", "b_inv", "k_inv", "b_end", "k_end", "v", "gate", "bonus")


def _rwkv_body(p_ref, mu_ref, w0_ref, w2_ref, a0_ref, a2_ref, g2_ref, kk_ref, ka_ref, rk_ref, lng_ref, lnb_ref,
               seg_ref, tri_ref, quad_ref, o_ref, sp_ref, prev_ref, st_ref, gl_ref):
    L = RWKV_CHUNK
    W = RWKV_WIDTH
    N = RWKV_N
    nb = p_ref.shape[0]
    n_chunks = p_ref.shape[1] // L
    n_pairs = RWKV_HEADS // 2

    @pl.when(pl.program_id(1) == 0)
    def _():
        sp_ref[...] = jnp.zeros_like(sp_ref)
        prev_ref[...] = jnp.zeros_like(prev_ref)

    M = nb * L
    HW = W // 2

    def seg_sum(x):
        parts = [piece[:, s * HW:(s + 1) * HW] for piece in _split3(x) for s in range(2)]
        out = jnp.dot(jnp.concatenate(parts, axis=0), seg_ref[...], preferred_element_type=F32)
        m = x.shape[0]
        halves = [out[s * m:(s + 1) * m] + out[(2 + s) * m:(3 + s) * m] + out[(4 + s) * m:(5 + s) * m]
                  for s in range(2)]
        return jnp.concatenate(halves, axis=1)

    def prepare(c, slot):
        rows = pl.ds(pl.multiple_of(c * L, L), L)
        p = jnp.concatenate([p_ref[b, rows, :] for b in range(nb)], axis=0)
        rolled = pltpu.roll(p, 1, 0)
        first = lax.broadcasted_iota(jnp.int32, (SUBLANES, p.shape[1]), 0) == 0
        pieces = []
        for b in range(nb):
            pieces.append(jnp.where(first, prev_ref[b, 0:1, :], rolled[b * L:b * L + SUBLANES, :]))
            pieces.append(rolled[b * L + SUBLANES:(b + 1) * L, :])
            prev_ref[b, 0:1, :] = p[(b + 1) * L - 1:(b + 1) * L, :]
        sh = jnp.concatenate(pieces, axis=0)
        pf = p + (sh - p) * mu_ref[...]
        r = pf[:, 0:W]
        k = pf[:, W:2 * W]
        v = pf[:, 2 * W:3 * W]
        wa = pf[:, 3 * W:3 * W + LANES]
        g_lr = pf[:, 3 * W + LANES:3 * W + 2 * LANES]
        log_w = -RWKV_DECAY_SCALE * jax.nn.sigmoid(w0_ref[...] + _mm(jnp.tanh(wa), w2_ref[...]))
        a = jax.nn.sigmoid(a0_ref[...] + _mm(wa, a2_ref[...]))
        gate = _mm(jax.nn.sigmoid(g_lr), g2_ref[...])
        yield
        kk = k * kk_ref[...]
        kk = kk * lax.rsqrt(seg_sum(kk * kk) + RWKV_L2_EPS)
        yield
        k = k * (1.0 + (a - 1.0) * ka_ref[...])
        b_vec = kk * a
        ri = lax.broadcasted_iota(jnp.int32, (M, M), 0)
        ci = lax.broadcasted_iota(jnp.int32, (M, M), 1)
        same_seq = jnp.bitwise_and(ri, -L) == jnp.bitwise_and(ci, -L)
        tri01 = jnp.where(same_seq & (ri >= ci), 1.0, 0.0).astype(BF16)
        g = _mm_exact_lhs01(tri01, log_w)
        yield
        g_last = jnp.concatenate([jnp.broadcast_to(g[(b + 1) * L - 1:(b + 1) * L, :], (L, W)) for b in range(nb)],
                                 axis=0)
        e_neg = jnp.exp(-g)
        e_end = jnp.exp(g_last - g)
        bonus = seg_sum(r * k * rk_ref[...]) * v
        yield
        vals = dict(a_dec=-kk * jnp.exp(g - log_w), r_dec=r * jnp.exp(g), b_inv=b_vec * e_neg, k_inv=k * e_neg,
                    b_end=b_vec * e_end, k_end=k * e_end, v=v, gate=gate, bonus=bonus)
        for j, name in enumerate(_RWKV_STAGED):
            st_ref[slot, j] = vals[name]
        for b in range(nb):
            gl_ref[slot, b] = g[(b + 1) * L - SUBLANES:(b + 1) * L, :]
        yield

    def recur(c, slot, filler):
        def tick():
            if filler is not None:
                next(filler, None)

        lane = lax.broadcasted_iota(jnp.int32, (L, LANES), 1)
        rowi = lax.broadcasted_iota(jnp.int32, (L, LANES), 0)
        lo = lane < N
        hi = lane >= N
        strict_f = tri_ref[0]
        incl_f = tri_ref[1]
        eye2 = tri_ref[2]
        diag_f = quad_ref[0]
        anti_f = quad_ref[1]
        nat = (lo, hi)
        oth = (hi, lo)
        sel = lambda m, x: jnp.where(m, x, 0.0)
        swap = lambda x: pltpu.roll(x, N, 1)
        cat0 = lambda xs: jnp.concatenate(xs, axis=0)
        dot = lambda x, y: jnp.dot(x, y, preferred_element_type=F32)
        zeros_b = jnp.zeros((L, LANES), BF16)
        rows = pl.ds(pl.multiple_of(c * L, L), L)
        U = [(b, p) for b in range(nb) for p in range(n_pairs)]
        HH = range(2)
        idx = {name: j for j, name in enumerate(_RWKV_STAGED)}
        blk = lambda name, u: st_ref[slot, idx[name], u[0] * L:(u[0] + 1) * L, u[1] * LANES:(u[1] + 1) * LANES]
        nu = range(len(U))
        vn = lambda i, h: sel(nat[h], blk("v", U[i])).astype(BF16)

        ad = [blk("a_dec", u) for u in U]
        rd = [blk("r_dec", u) for u in U]
        b_inv = [blk("b_inv", u) for u in U]
        k_inv = [blk("k_inv", u) for u in U]
        pe = [_mm_nt(cat0([sel(lo, ad[i]), sel(lo, rd[i])]), cat0([b_inv[i], k_inv[i]])) for i in nu]
        po = [_mm_nt(cat0([sel(hi, ad[i]), sel(hi, rd[i])]), cat0([k_inv[i], b_inv[i]])) for i in nu]
        tick()
        aa = [[pe[i][0:L, :] * strict_f, po[i][0:L, :] * strict_f] for i in nu]
        ar = [[(pe[i][L:2 * L, :] * incl_f).astype(BF16), (po[i][L:2 * L, :] * incl_f).astype(BF16)] for i in nu]
        t0 = [dot(jnp.where(lo, aa[i][1], aa[i][0]).astype(BF16),
                  cat0([jnp.concatenate([vn(i, 1), zeros_b], axis=1), jnp.concatenate([zeros_b, vn(i, 0)], axis=1)]))
              for i in nu]
        ad_sw = [swap(x) for x in ad]
        x0 = [[jnp.where(hi, ad_sw[i], t0[i][:, LANES:2 * LANES]).astype(BF16),
               jnp.where(lo, ad_sw[i], t0[i][:, 0:LANES]).astype(BF16)] for i in nu]
        q0 =[jnp.where(lo, aa[i][0], eye2) for i in nu]
        q1 = [jnp.where(lo, eye2, aa[i][1]) for i in nu]
        for _ in range(6):
            out = [dot(jnp.where(lo, q0[i], q1[i]).astype(BF16),
                       cat0([jnp.concatenate([q0[i].astype(BF16), zeros_b], axis=1),
                             jnp.concatenate([zeros_b, q1[i].astype(BF16)], axis=1)])) for i in nu]
            tick()
            q0 = [out[i][:, 0:LANES] + sel(hi, q0[i]) for i in nu]
            q1 = [out[i][:, LANES:2 * LANES] + sel(lo, q1[i]) for i in nu]
        tx = [dot(jnp.where(lo, q1[i], q0[i]).astype(BF16),
                  cat0([jnp.concatenate([x0[i][1], zeros_b], axis=1), jnp.concatenate([zeros_b, x0[i][0]], axis=1)]))
              for i in nu]
        tick()
        xb = [[tx[i][:, LANES:2 * LANES].astype(BF16), tx[i][:, 0:LANES].astype(BF16)] for i in nu]
        rd_sw = [swap(blk("r_dec", u)) for u in U]
        rmy = [[sel(hi, rd_sw[i]) + dot(ar[i][0], cat0([xb[i][0], vn(i, 0)])),
                sel(lo, rd_sw[i]) + dot(ar[i][1], cat0([vn(i, 1), xb[i][1]]))] for i in nu]
        be_sw = [swap(blk("b_end", u)) for u in U]
        ke_sw = [swap(blk("k_end", u)) for u in U]
        gg = [_mm_tn(cat0([xb[i][0], xb[i][1], vn(i, 0), vn(i, 1)]),
                     cat0([sel(hi, be_sw[i]), sel(lo, be_sw[i]), sel(hi, ke_sw[i]), sel(lo, ke_sw[i])]))
              for i in nu]
        tick()
        sp = [sp_ref[i] for i in nu]
        ys = [_mm_nt(jnp.where(lo, rmy[i][1], rmy[i][0]), sp[i]) + jnp.where(lo, rmy[i][0], rmy[i][1]) for i in nu]
        for i, u in enumerate(U):
            sd_sw = jnp.exp(swap(gl_ref[slot, u[0], :, u[1] * LANES:(u[1] + 1) * LANES])[SUBLANES - 1:SUBLANES, :])
            sp_ref[i] = sp[i] * sd_sw + _mm(sp[i], gg[i] * diag_f) + gg[i] * anti_f
        if filler is not None:
            for _ in filler:
                pass
        y = jnp.concatenate([jnp.concatenate(ys[b * n_pairs:(b + 1) * n_pairs], axis=1) for b in range(nb)], axis=0)
        mean = seg_sum(y) * (1.0 / N)
        dy = y - mean
        var = seg_sum(dy * dy) * (1.0 / N)
        y = dy * lax.rsqrt(var + RWKV_GN_EPS) * lng_ref[...] + lnb_ref[...]
        y = (y + st_ref[slot, idx["bonus"]]) * st_ref[slot, idx["gate"]]
        for b in range(nb):
            o_ref[b, rows, :] = y[b * L:(b + 1) * L, :]

    for _ in prepare(0, 0):
        pass

    def body(c, carry):
        slot = jnp.bitwise_and(c, 1)
        recur(c, slot, prepare(c + 1, 1 - slot))
        return carry

    lax.fori_loop(0, n_chunks - 1, body, 0)
    recur(n_chunks - 1, (n_chunks - 1) % 2, None)


def rwkv_mixer(pb, mu, w0, w2, a0, a2, g2, k_k, k_a, r_k, ln_g, ln_b):
    b, t, cols = pb.shape
    tt = min(RWKV_TILE, t)
    nb = min(RWKV_BATCH_PER_STEP, b)
    W = RWKV_WIDTH
    row = lambda x: x.reshape(1, -1)
    w2p = jnp.zeros((LANES, W), BF16).at[:RWKV_LORA_W].set(w2.astype(BF16))
    a2p = jnp.zeros((LANES, W), BF16).at[RWKV_LORA_W:].set(a2.astype(BF16))
    head_of = jnp.arange(W // 2) // RWKV_N
    seg = (head_of[:, None] == head_of[None, :]).astype(BF16)
    t_idx = jnp.arange(RWKV_CHUNK)[:, None]
    s_idx = jnp.arange(LANES)[None, :] % RWKV_N
    tri = jnp.stack([t_idx > s_idx, t_idx >= s_idx, t_idx == s_idx]).astype(F32)
    half = jnp.arange(LANES) // RWKV_N
    same_half = half[:, None] == half[None, :]
    quad = jnp.stack([same_half, ~same_half]).astype(F32)
    consts = [row(mu), row(w0), w2p, row(a0), a2p, g2.astype(BF16), row(k_k), row(k_a), row(r_k), row(ln_g),
              row(ln_b), seg, tri, quad]
    return pl.pallas_call(
        _rwkv_body,
        out_shape=jax.ShapeDtypeStruct((b, t, W), F32),
        grid=(b // nb, t // tt),
        in_specs=[pl.BlockSpec((nb, tt, cols), lambda i, j: (i, j, 0))] + [_const_spec(x.shape) for x in consts],
        out_specs=pl.BlockSpec((nb, tt, W), lambda i, j: (i, j, 0)),
        scratch_shapes=[pltpu.VMEM((nb * RWKV_HEADS // 2, LANES, LANES), F32),
                        pltpu.VMEM((nb, SUBLANES, cols), F32),
                        pltpu.VMEM((2, len(_RWKV_STAGED), nb * RWKV_CHUNK, W), F32),
                        pltpu.VMEM((2, nb, SUBLANES, W), F32)],
        compiler_params=_params(("parallel", "arbitrary")),
        name="rwkv_mixer",
    )(pb, *consts)


def _rglru_body(p_ref, cw_ref, cb_ref, gw_ref, gb_ref, lam_ref, o_ref, xprev_ref, hprev_ref):
    tt = p_ref.shape[1]
    W = LRU_WIDTH

    @pl.when(pl.program_id(1) == 0)
    def _():
        xprev_ref[0:SUBLANES, :] = jnp.zeros((SUBLANES, W), F32)
        hprev_ref[...] = jnp.zeros_like(hprev_ref)

    x = p_ref[0, :, 0:W]
    gate = p_ref[0, :, W:2 * W]
    xc = _causal_conv(x, xprev_ref, cw_ref[...], cb_ref[...], LRU_CONV)
    xcb = xc.astype(BF16)
    pre = []
    for gi in range(2):
        pre.append(jnp.concatenate(
            [jnp.dot(xcb[:, n * LRU_BLOCK:(n + 1) * LRU_BLOCK], gw_ref[gi, n], preferred_element_type=F32)
             for n in range(LRU_BLOCKS)], axis=1) + gb_ref[gi:gi + 1, :])
    r_gate = jax.nn.sigmoid(pre[0])
    i_gate = jax.nn.sigmoid(pre[1])
    log_a = -LRU_C * r_gate * _softplus(-lam_ref[...])
    a = jnp.exp(log_a)
    w2 = -jnp.tanh(log_a) * (a * a + 1.0)
    u = jnp.where(w2 > 0.0, w2 * lax.rsqrt(w2), 0.0) * (i_gate * xc)
    row_in_group = jnp.bitwise_and(lax.broadcasted_iota(jnp.int32, (tt, W), 0), SUBLANES - 1)
    d = 1
    while d < SUBLANES:
        keep = row_in_group >= d
        u = u + a * jnp.where(keep, pltpu.roll(u, d, 0), 0.0)
        a = a * jnp.where(keep, pltpu.roll(a, d, 0), 1.0)
        d *= 2
    carry = hprev_ref[0:1, :]
    hs = []
    for g in range(tt // SUBLANES):
        rs = slice(g * SUBLANES, (g + 1) * SUBLANES)
        hg = u[rs, :] + a[rs, :] * carry
        carry = hg[SUBLANES - 1:SUBLANES, :]
        hs.append(hg)
    hprev_ref[0:1, :] = carry
    o_ref[0] = jnp.concatenate(hs, axis=0) * jax.nn.gelu(gate)


def rglru_mixer(pc, conv_w, conv_b, gate_w, gate_b, lam):
    b, t, cols = pc.shape
    tt = min(MIX_TILE, t)
    W = LRU_WIDTH
    cw = jnp.zeros((SUBLANES, W), F32).at[:LRU_CONV].set(conv_w)
    consts = [cw, conv_b.reshape(1, W), gate_w.astype(BF16), gate_b, lam.reshape(1, W)]
    return pl.pallas_call(
        _rglru_body,
        out_shape=jax.ShapeDtypeStruct((b, t, W), F32),
        grid=(b, t // tt),
        in_specs=[pl.BlockSpec((1, tt, cols), lambda i, j: (i, j, 0))] + [_const_spec(x.shape) for x in consts],
        out_specs=pl.BlockSpec((1, tt, W), lambda i, j: (i, j, 0)),
        scratch_shapes=[pltpu.VMEM((SUBLANES + tt, W), F32), pltpu.VMEM((SUBLANES, W), F32)],
        compiler_params=_params(("parallel", "arbitrary")),
        name="rglru_mixer",
    )(pc, *consts)


def _mlstm_body(p_ref, cw_ref, cb_ref, wq_ref, wk_ref, wv_ref, bif_ref, ng_ref, o_ref,
                xprev_ref, c_ref, m_ref, q_s, k_s, v_s):
    tt = p_ref.shape[1]
    L = MLSTM_CHUNK
    W = MLSTM_WIDTH
    DH = MLSTM_DH
    H = MLSTM_HEADS
    n_chunks = tt // L

    @pl.when(pl.program_id(1) == 0)
    def _():
        xprev_ref[:, 0:SUBLANES, :] = jnp.zeros((xprev_ref.shape[0], SUBLANES, W), F32)
        c_ref[...] = jnp.zeros_like(c_ref)
        m_ref[...] = jnp.zeros_like(m_ref)

    nb = p_ref.shape[0]
    for b in range(nb):
        x = p_ref[b, :, 0:W]
        xc = _silu(_causal_conv(x, xprev_ref.at[b], cw_ref[...], cb_ref[...], MLSTM_CONV))
        xcb = xc.astype(BF16)
        q_s[b] = jnp.dot(xcb, wq_ref[...], preferred_element_type=F32)
        k_s[b] = jnp.dot(xcb, wk_ref[...], preferred_element_type=F32) * DH ** -0.5
        v_s[b] = jnp.dot(x.astype(BF16), wv_ref[...], preferred_element_type=F32)

    def chunk(c, carry):
        incl, _ = _tri_masks(L)
        tri01 = jnp.where(incl, 1.0, 0.0).astype(BF16)
        t0 = pl.multiple_of(c * L, L)
        rows = pl.ds(t0, L)
        nt = lambda y: lax.dot_general(y, tri01, (((1,), (1,)), ((), ())), preferred_element_type=F32)
        gates, lsg, g_t, b_rows = [], [], [], []
        for b in range(nb):
            gates.append(p_ref[b, rows, 2 * W:2 * W + LANES] + bif_ref[...])
            lsg.append(_log_sigmoid(gates[b]))
            g_t.append(gates[b].T[0:SUBLANES, :])
            x1, x2, x3 = _split3(_log_sigmoid(g_t[b]))
            b_rows.append(nt(x1) + nt(x2) + nt(x3))
        ones_b = jnp.ones((L, LANES), BF16)
        U = [(b, h) for b in range(nb) for h in range(H)]
        nu = range(len(U))
        hs = [slice(h * DH, (h + 1) * DH) for h in range(H)]
        qh = [q_s[b, rows, hs[h]] for b, h in U]
        kh = [k_s[b, rows, hs[h]] for b, h in U]
        v_aug = [jnp.concatenate([v_s[b, rows, hs[h]].astype(BF16), ones_b], axis=1) for b, h in U]
        cm = [c_ref[i] for i in nu]
        m = [m_ref[i, 0:1, :] for i in nu]
        i_rep = [jnp.broadcast_to(gates[b][:, h:h + 1], (L, LANES)) for b, h in U]
        lf_rep = [jnp.broadcast_to(lsg[b][:, H + h:H + h + 1], (L, LANES)) for b, h in U]
        b_rep = [_mm_exact_lhs01(tri01, lf_rep[i]) for i in nu]
        qk = [_mm_nt(qh[i], kh[i]) for i in nu]
        b_last = [b_rep[i][L - 1:L, :] for i in nu]
        log_e = [b_last[i] - b_rep[i] + i_rep[i] for i in nu]
        m_end = [jnp.max(log_e[i], axis=0, keepdims=True) for i in nu]
        m_new = [jnp.maximum(b_last[i] + m[i], m_end[i]) for i in nu]
        kin = [kh[i] * (jnp.exp(m_end[i] - m_new[i]) * jnp.exp(log_e[i] - m_end[i])) for i in nu]
        c_upd = [_mm_tn(kin[i], v_aug[i]) for i in nu]
        run = [i_rep[i] - b_rep[i] for i in nu]
        d = 1
        while d < L:
            run = [jnp.maximum(run[i], _shift_rows(run[i], d, -jnp.inf)) for i in nu]
            d *= 2
        m_t = [jnp.maximum(b_rep[i] + m[i], b_rep[i] + run[i]) for i in nu]
        log_d = [jnp.where(incl, b_rep[i][:, 0:L] - b_rows[b][H + h:H + h + 1, :] + g_t[b][h:h + 1, :], -jnp.inf)
                 for i, (b, h) in enumerate(U)]
        w_loc = [jnp.exp(log_d[i] - m_t[i][:, 0:L]) * qk[i] for i in nu]
        lhs = [jnp.concatenate([(jnp.exp(b_rep[i] + m[i] - m_t[i]) * qh[i]).astype(BF16), w_loc[i].astype(BF16)],
                               axis=1) for i in nu]
        nd = [jnp.dot(lhs[i], jnp.concatenate([cm[i].astype(BF16), v_aug[i]], axis=0),
                      preferred_element_type=F32) for i in nu]
        for i, (b, h) in enumerate(U):
            c_state = jnp.exp(b_last[i] + m[i] - m_new[i])
            c_ref[i] = jnp.concatenate([c_state, c_state], axis=1) * cm[i] + c_upd[i]
            m_ref[i] = jnp.broadcast_to(m_new[i], (SUBLANES, LANES))
            hv = nd[i][:, 0:DH] / jnp.maximum(jnp.abs(nd[i][:, DH:2 * DH]), jnp.exp(-m_t[i]))
            hv = hv * lax.rsqrt(jnp.mean(hv * hv, axis=-1, keepdims=True) + NORM_EPS)
            o_ref[b, rows, hs[h]] = (jax.nn.sigmoid(p_ref[b, rows, W + h * DH:W + (h + 1) * DH]) * hv
                                     * ng_ref[:, hs[h]])
        return carry

    lax.fori_loop(0, n_chunks, chunk, 0)


def _block_diag(w):
    nb, d, e = w.shape
    rows = jnp.tile(w.reshape(nb * d, e), (1, nb))
    same_block = (jnp.arange(nb * d)[:, None] // d) == (jnp.arange(nb * e)[None, :] // e)
    return jnp.where(same_block, rows, 0.0)


def mlstm_mixer(pd, conv_w, conv_b, qkv_w, b_if, norm_g):
    b, t, cols = pd.shape
    tt = min(MIX_TILE, t)
    nb = min(MLSTM_BATCH_PER_STEP, b)
    W = MLSTM_WIDTH
    cw = jnp.zeros((SUBLANES, W), F32).at[:MLSTM_CONV].set(conv_w)
    bif = jnp.zeros((1, LANES), F32).at[0, :2 * MLSTM_HEADS].set(b_if)
    consts = [cw, conv_b.reshape(1, W)] + [_block_diag(qkv_w[i]).astype(BF16) for i in range(3)] + [
        bif, norm_g.reshape(1, W)]
    return pl.pallas_call(
        _mlstm_body,
        out_shape=jax.ShapeDtypeStruct((b, t, W), F32),
        grid=(b // nb, t // tt),
        in_specs=[pl.BlockSpec((nb, tt, cols), lambda i, j: (i, j, 0))] + [_const_spec(x.shape) for x in consts],
        out_specs=pl.BlockSpec((nb, tt, W), lambda i, j: (i, j, 0)),
        scratch_shapes=[pltpu.VMEM((nb, SUBLANES + tt, W), F32),
                        pltpu.VMEM((nb * MLSTM_HEADS, MLSTM_DH, 2 * MLSTM_DH), F32),
                        pltpu.VMEM((nb * MLSTM_HEADS, SUBLANES, LANES), F32),
                        pltpu.VMEM((nb, tt, W), F32), pltpu.VMEM((nb, tt, W), F32), pltpu.VMEM((nb, tt, W), F32)],
        compiler_params=_params(("parallel", "arbitrary")),
        name="mlstm_mixer",
    )(pd, *consts)


def _mix_xattn_body(h_ref, ya_ref, yb_ref, wm_ref, g_ref, wq_ref, k_ref, v_ref, wo_ref, o_ref):
    mixed = jnp.concatenate([ya_ref[0], yb_ref[0]], axis=1).astype(BF16)
    h = h_ref[0] + jnp.dot(mixed, wm_ref[...], preferred_element_type=F32)
    xn = _rmsnorm(h, g_ref[...]).astype(BF16)
    q = jnp.dot(xn, wq_ref[...], preferred_element_type=F32)
    outs = []
    for hd in range(XA_HEADS):
        hs = slice(hd * XA_DH, (hd + 1) * XA_DH)
        s = _mm_nt(q[:, hs], k_ref[0, :, hs]) * XA_DH ** -0.5
        e = jnp.exp(s - jnp.max(s, axis=-1, keepdims=True))
        p = e / jnp.sum(e, axis=-1, keepdims=True)
        outs.append(_mm(p, v_ref[0, :, hs]))
    o = jnp.concatenate(outs, axis=1).astype(BF16)
    o_ref[0] = h + jnp.dot(o, wo_ref[...], preferred_element_type=F32)


def mix_xattn(h, ya, yb, w_mix, g, wq, kv, wo):
    b, t, d = h.shape
    tm = min(XATTN_TILE, t)
    tok = lambda w: pl.BlockSpec((1, tm, w), lambda i, j: (i, j, 0))
    mem_k = pl.BlockSpec((1, MEM_LEN, d), lambda i, j: (i, 0, 0))
    mem_v = pl.BlockSpec((1, MEM_LEN, d), lambda i, j: (i, 0, 1))
    return pl.pallas_call(
        _mix_xattn_body,
        out_shape=jax.ShapeDtypeStruct((b, t, d), F32),
        grid=(b, t // tm),
        in_specs=[tok(d), tok(ya.shape[-1]), tok(yb.shape[-1]), _const_spec(w_mix.shape), _const_spec((1, d)),
                  _const_spec(wq.shape), mem_k, mem_v, _const_spec(wo.shape)],
        out_specs=tok(d),
        compiler_params=_params(("parallel", "parallel")),
        name="mix_xattn",
    )(h, ya, yb, w_mix, g.reshape(1, d), wq, kv, kv, wo)


def _ffn_body(final_norm, h_ref, g_ref, wu_ref, wg_ref, cw_ref, cb_ref, wd_ref, fg_ref, o_ref, uprev_ref):
    tm = h_ref.shape[1]

    @pl.when(pl.program_id(1) == 0)
    def _():
        uprev_ref[...] = jnp.zeros_like(uprev_ref)

    h = h_ref[0]
    xn = _rmsnorm(h, g_ref[...]).astype(BF16)
    acc = h
    start = 0
    for width in FFN_COL_GROUPS:
        cs = slice(start, start + width)
        start += width
        u = jnp.dot(xn, wu_ref[:, cs], preferred_element_type=F32)
        gt = jnp.dot(xn, wg_ref[:, cs], preferred_element_type=F32)
        uu = jnp.concatenate([uprev_ref[:, cs], u], axis=0)
        uprev_ref[:, cs] = u[tm - SUBLANES:tm, :]
        c = cb_ref[:, cs] + cw_ref[2:3, cs] * u
        c = c + cw_ref[1:2, cs] * pltpu.roll(uu, 1, 0)[SUBLANES:SUBLANES + tm, :]
        c = c + cw_ref[0:1, cs] * pltpu.roll(uu, 2, 0)[SUBLANES:SUBLANES + tm, :]
        act = (_silu(c) * gt).astype(BF16)
        acc = acc + jnp.dot(act, wd_ref[cs, :], preferred_element_type=F32)
    if final_norm:
        acc = _rmsnorm(acc, fg_ref[...])
    o_ref[0] = acc


def ffn(h, g, wu, wg, conv_w, conv_b, wd, final_g, final_norm):
    b, t, d = h.shape
    tm = min(ROW_TILE, t)
    tok = pl.BlockSpec((1, tm, d), lambda i, j: (i, j, 0))
    consts = [g.reshape(1, d), wu, wg, conv_w, conv_b, wd, final_g.reshape(1, d)]
    return pl.pallas_call(
        functools.partial(_ffn_body, final_norm),
        out_shape=jax.ShapeDtypeStruct((b, t, d), F32),
        grid=(b, t // tm),
        in_specs=[tok] + [_const_spec(x.shape) for x in consts],
        out_specs=tok,
        scratch_shapes=[pltpu.VMEM((SUBLANES, D_FF_PAD), F32)],
        compiler_params=_params(("parallel", "arbitrary")),
        name="ffn",
    )(h, *consts)


def _pad_cols(w, n):
    return jnp.pad(w, ((0, 0), (0, n - w.shape[1])))


def _rwkv_perm(x):
    W = RWKV_WIDTH
    o_w, o_k, o_v = W, W + RWKV_LORA_W, 2 * W + RWKV_LORA_W
    o_a = 3 * W + RWKV_LORA_W
    o_g = o_a + RWKV_LORA_A
    return jnp.concatenate([x[..., :W], x[..., o_k:o_k + W], x[..., o_v:o_v + W], x[..., o_w:o_w + RWKV_LORA_W],
                            x[..., o_a:o_a + RWKV_LORA_A], x[..., o_g:]], axis=-1)


def kernel(x, mem, mem_norm_g, norm_mix_g, ab_w_in, gla_w_alpha2, gla_b_alpha, gla_norm_g, rwkv_mu, rwkv_w0, rwkv_w2, rwkv_a0, rwkv_a2, rwkv_g2, rwkv_k_k, rwkv_k_a, rwkv_r_k, rwkv_ln_g, rwkv_ln_b, cd_w_in, lru_conv_w, lru_conv_b, lru_gate_w, lru_gate_b, lru_lambda, mlstm_conv_w, mlstm_conv_b, mlstm_qkv_w, mlstm_b_if, mlstm_norm_g, w_mix_out, norm_xattn_g, xattn_wq, xattn_wkv, xattn_wo, norm_ffn_g, ffn_w_up, ffn_conv_w, ffn_conv_b, ffn_w_down, final_norm_g):
    b, t, d = x.shape
    depth = norm_mix_g.shape[0]
    n = b * t
    h = x
    mem2d = mem.reshape(b * MEM_LEN, d)
    for layer in range(depth):
        j = layer // 2
        h2d = h.reshape(n, d)
        if layer % 2 == 0:
            w = ab_w_in[j]
            wa = _pad_cols(w[:, :GLA_COLS], GLA_COLS_PAD).astype(BF16)
            wb = _rwkv_perm(w[:, GLA_COLS:]).astype(BF16)
            pa, pb = norm_matmul(h2d, norm_mix_g[layer], [wa, wb], name="proj_ab")
            ya = gla_mixer(pa.reshape(b, t, -1), gla_w_alpha2[j], gla_b_alpha[j], gla_norm_g[j])
            yb = rwkv_mixer(pb.reshape(b, t, -1), _rwkv_perm(rwkv_mu[j]), rwkv_w0[j], rwkv_w2[j], rwkv_a0[j],
                            rwkv_a2[j], rwkv_g2[j], rwkv_k_k[j], rwkv_k_a[j], rwkv_r_k[j], rwkv_ln_g[j],
                            rwkv_ln_b[j])
        else:
            w = cd_w_in[j]
            wc = w[:, :2 * LRU_WIDTH].astype(BF16)
            wd_ = _pad_cols(w[:, 2 * LRU_WIDTH:], MLSTM_COLS_PAD).astype(BF16)
            pc, pd = norm_matmul(h2d, norm_mix_g[layer], [wc, wd_], name="proj_cd")
            ya = rglru_mixer(pc.reshape(b, t, -1), lru_conv_w[j], lru_conv_b[j], lru_gate_w[j], lru_gate_b[j],
                             lru_lambda[j])
            yb = mlstm_mixer(pd.reshape(b, t, -1), mlstm_conv_w[j], mlstm_conv_b[j], mlstm_qkv_w[j],
                             mlstm_b_if[j], mlstm_norm_g[j])
        (kv,) = norm_matmul(mem2d, mem_norm_g, [xattn_wkv[layer].astype(BF16)], out_dtype=BF16, name="proj_kv")
        kv = kv.reshape(b, MEM_LEN, 2 * d)
        h = mix_xattn(h, ya, yb, w_mix_out[layer].astype(BF16), norm_xattn_g[layer], xattn_wq[layer].astype(BF16),
                      kv, xattn_wo[layer].astype(BF16))
        wup = ffn_w_up[layer]
        wu = _pad_cols(wup[:, :D_FF], D_FF_PAD).astype(BF16)
        wg = _pad_cols(wup[:, D_FF:], D_FF_PAD).astype(BF16)
        cw = _pad_cols(jnp.pad(ffn_conv_w[layer], ((0, SUBLANES - FFN_CONV), (0, 0))), D_FF_PAD)
        cb = _pad_cols(ffn_conv_b[layer].reshape(1, D_FF), D_FF_PAD)
        wdn = jnp.pad(ffn_w_down[layer], ((0, D_FF_PAD - D_FF), (0, 0))).astype(BF16)
        h = ffn(h, norm_ffn_g[layer], wu, wg, cw, cb, wdn, final_norm_g, layer == depth - 1)
    return h
```

```python
import functools

import jax
import jax.numpy as jnp
from jax import lax
from jax.experimental import pallas as pl
from jax.experimental.pallas import tpu as pltpu

F32 = jnp.float32
BF16 = jnp.bfloat16

D_MODEL = 1024
NORM_EPS = 1e-6
LANES = 128
SUBLANES = 8
VMEM_LIMIT_BYTES = 56 * 1024 * 1024

GLA_HEADS, GLA_DK, GLA_DV, GLA_RANK, GLA_TAU, GLA_CHUNK = 4, 64, 128, 16, 16.0, 64
GLA_COLS = 2 * GLA_HEADS * GLA_DK + 2 * GLA_HEADS * GLA_DV + GLA_RANK
GLA_COLS_PAD = 13 * LANES

RWKV_HEADS, RWKV_N, RWKV_WIDTH = 8, 64, 512
RWKV_LORA_W, RWKV_LORA_A, RWKV_LORA_G = 64, 64, 128
RWKV_CHUNK = 64
RWKV_DECAY_SCALE = 0.6065306597126334
RWKV_GN_EPS = RWKV_N * 1e-5
RWKV_L2_EPS = 1e-12
RWKV_COLS = 3 * RWKV_WIDTH + RWKV_LORA_W + RWKV_LORA_A + RWKV_LORA_G

LRU_WIDTH, LRU_BLOCKS, LRU_BLOCK, LRU_C, LRU_CONV = 512, 4, 128, 8.0, 4
MLSTM_HEADS, MLSTM_DH, MLSTM_WIDTH, MLSTM_CONV, MLSTM_CHUNK = 4, 128, 512, 4, 64
MLSTM_COLS_PAD = 2 * MLSTM_WIDTH + LANES

XA_HEADS, XA_DH, MEM_LEN = 4, 256, 256
D_FF, FFN_CONV = 2752, 3
D_FF_PAD = 22 * LANES
MXU_K_TILE = 256
FFN_COL_GROUPS = (6 * MXU_K_TILE, 5 * MXU_K_TILE)

MIX_TILE = 256
RWKV_TILE = 256
GLA_BATCH_PER_STEP = 4
MLSTM_BATCH_PER_STEP = 2
RWKV_BATCH_PER_STEP = 4
ROW_TILE = 512
XATTN_TILE = 1024
XATTN_ROW_GROUPS = 2
PROJ_ROW_GROUPS = 2


def _mm(a, b):
    return jnp.dot(a.astype(BF16), b.astype(BF16), preferred_element_type=F32)


def _mm_nt(a, b):
    return lax.dot_general(a.astype(BF16), b.astype(BF16), (((1,), (1,)), ((), ())), preferred_element_type=F32)


def _mm_tn(a, b):
    return lax.dot_general(a.astype(BF16), b.astype(BF16), (((0,), (0,)), ((), ())), preferred_element_type=F32)


def _split3(x):
    x1 = x.astype(BF16)
    r1 = x - x1.astype(F32)
    x2 = r1.astype(BF16)
    x3 = (r1 - x2.astype(F32)).astype(BF16)
    return x1, x2, x3


def _mm_exact_lhs01(m01, x):
    x1, x2, x3 = _split3(x)
    d = lambda y: jnp.dot(m01, y, preferred_element_type=F32)
    return d(x1) + d(x2) + d(x3)


def _rmsnorm(x, g):
    return x * lax.rsqrt(jnp.mean(x * x, axis=-1, keepdims=True) + NORM_EPS) * g


def _log_sigmoid(x):
    return jnp.minimum(x, 0.0) - jnp.log1p(jnp.exp(-jnp.abs(x)))


def _softplus(x):
    return jnp.maximum(x, 0.0) + jnp.log1p(jnp.exp(-jnp.abs(x)))


def _silu(x):
    return x * jax.nn.sigmoid(x)


def _tri_masks(n):
    r = lax.broadcasted_iota(jnp.int32, (n, n), 0)
    c = lax.broadcasted_iota(jnp.int32, (n, n), 1)
    return r >= c, r > c


def _shift_rows(x, s, fill):
    if s == 0:
        return x
    rolled = pltpu.roll(x, s, 0)
    row = lax.broadcasted_iota(jnp.int32, x.shape, 0)
    return jnp.where(row >= s, rolled, fill)


def _causal_conv(x, buf_ref, w, b, width):
    t = x.shape[0]
    buf_ref[SUBLANES:SUBLANES + t, :] = x
    y = b + w[width - 1:width, :] * x
    for j in range(width - 1):
        s = width - 1 - j
        y = y + w[j:j + 1, :] * buf_ref[SUBLANES - s:SUBLANES - s + t, :]
    buf_ref[0:SUBLANES, :] = x[t - SUBLANES:t, :]
    return y


def _const_spec(shape):
    nd = len(shape)
    return pl.BlockSpec(shape, lambda *_: (0,) * nd, pipeline_mode=pl.Buffered(1))


def _params(sem):
    return pltpu.CompilerParams(dimension_semantics=sem, vmem_limit_bytes=VMEM_LIMIT_BYTES)


def _norm_matmul_body(n_out, x_ref, g_ref, *refs):
    xn = _rmsnorm(x_ref[...], g_ref[...]).astype(BF16)
    for w_ref, o_ref in zip(refs[:n_out], refs[n_out:]):
        o_ref[...] = jnp.dot(xn, w_ref[...], preferred_element_type=F32).astype(o_ref.dtype)


def norm_matmul(x2d, g, ws, out_dtype=F32, name="norm_matmul"):
    n, d = x2d.shape
    tm = min(ROW_TILE, n)
    assert n % tm == 0
    return pl.pallas_call(
        functools.partial(_norm_matmul_body, len(ws)),
        out_shape=[jax.ShapeDtypeStruct((n, w.shape[1]), out_dtype) for w in ws],
        grid=(n // tm,),
        in_specs=[pl.BlockSpec((tm, d), lambda i: (i, 0)), _const_spec((1, d))]
        + [_const_spec(w.shape) for w in ws],
        out_specs=[pl.BlockSpec((tm, w.shape[1]), lambda i: (i, 0)) for w in ws],
        compiler_params=_params(("parallel",)),
        name=name,
    )(x2d, g.reshape(1, d), *ws)


def _gla_body(p_ref, wa_ref, ba_ref, ng_ref, o_ref, st_ref):
    L = GLA_CHUNK
    nb = p_ref.shape[0]
    n_chunks = p_ref.shape[1] // L
    M = nb * L
    hk = GLA_HEADS * GLA_DK
    hv = GLA_HEADS * GLA_DV

    @pl.when(pl.program_id(1) == 0)
    def _():
        st_ref[...] = jnp.zeros_like(st_ref)

    def chunk(c, carry):
        incl, _ = _tri_masks(L)
        ri = lax.broadcasted_iota(jnp.int32, (M, M), 0)
        ci = lax.broadcasted_iota(jnp.int32, (M, M), 1)
        same_seq = jnp.bitwise_and(ri, -L) == jnp.bitwise_and(ci, -L)
        tri01 = jnp.where(same_seq & (ri >= ci), 1.0, 0.0).astype(BF16)
        t0 = pl.multiple_of(c * L, L)
        rows = pl.ds(t0, L)
        cols = lambda lo, hi: jnp.concatenate([p_ref[b, rows, lo:hi] for b in range(nb)], axis=0)
        q = cols(0, hk)
        k = cols(hk, 2 * hk)
        v = cols(2 * hk, 2 * hk + hv)
        gt = cols(2 * hk + hv, 2 * hk + 2 * hv)
        a_lr = cols(2 * hk + 2 * hv, GLA_COLS_PAD)
        la = _log_sigmoid(_mm(a_lr, wa_ref[...]) + ba_ref[...]) * (1.0 / GLA_TAU)
        g = _mm_exact_lhs01(tri01, la)
        g_last = jnp.concatenate([jnp.broadcast_to(g[(b + 1) * L - 1:(b + 1) * L, :], (L, hk)) for b in range(nb)],
                                 axis=0)
        q_dec = (q * GLA_DK ** -0.5) * jnp.exp(g)
        k_inv = k * jnp.exp(-g)
        k_end = k * jnp.exp(g_last - g)
        sd = [jnp.exp(g[(b + 1) * L - 1:(b + 1) * L, :]) for b in range(nb)]
        U = [(b, h) for b in range(nb) for h in range(GLA_HEADS)]
        nu = range(len(U))
        rb = [slice(b * L, (b + 1) * L) for b in range(nb)]
        ks = [slice(h * GLA_DK, (h + 1) * GLA_DK) for h in range(GLA_HEADS)]
        vs = [slice(h * GLA_DV, (h + 1) * GLA_DV) for h in range(GLA_HEADS)]
        qd = [q_dec[rb[b], ks[h]] for b, h in U]
        vh = [v[rb[b], vs[h]] for b, h in U]
        st = [st_ref[i] for i in nu]
        sc = [jnp.where(incl, _mm_nt(qd[i], k_inv[rb[b], ks[h]]), 0.0) for i, (b, h) in enumerate(U)]
        o_state = [_mm_nt(qd[i], st[i]) for i in nu]
        upd = [_mm_tn(vh[i], k_end[rb[b], ks[h]]) for i, (b, h) in enumerate(U)]
        o_loc = [_mm(sc[i], vh[i]) for i in nu]
        for i, (b, h) in enumerate(U):
            st_ref[i] = st[i] * sd[b][:, ks[h]] + upd[i]
            o = o_loc[i] + o_state[i]
            o = o * lax.rsqrt(jnp.mean(o * o, axis=-1, keepdims=True) + NORM_EPS)
            o_ref[b, rows, vs[h]] = o * ng_ref[:, vs[h]] * _silu(gt[rb[b], vs[h]])
        return carry

    lax.fori_loop(0, n_chunks, chunk, 0)


def gla_mixer(pa, w_alpha2, b_alpha, norm_g):
    b, t, _ = pa.shape
    tt = min(MIX_TILE, t)
    nb = min(GLA_BATCH_PER_STEP, b)
    hk, hv = GLA_HEADS * GLA_DK, GLA_HEADS * GLA_DV
    wa = jnp.zeros((LANES, hk), BF16).at[:GLA_RANK].set(w_alpha2.astype(BF16))
    return pl.pallas_call(
        _gla_body,
        out_shape=jax.ShapeDtypeStruct((b, t, hv), F32),
        grid=(b // nb, t // tt),
        in_specs=[pl.BlockSpec((nb, tt, GLA_COLS_PAD), lambda i, j: (i, j, 0)),
                  _const_spec((LANES, hk)), _const_spec((1, hk)), _const_spec((1, hv))],
        out_specs=pl.BlockSpec((nb, tt, hv), lambda i, j: (i, j, 0)),
        scratch_shapes=[pltpu.VMEM((nb * GLA_HEADS, GLA_DV, GLA_DK), F32)],
        compiler_params=_params(("parallel", "arbitrary")),
        name="gla_mixer",
    )(pa, wa, b_alpha.reshape(1, hk), norm_g.reshape(1, hv))


_RWKV_SB = ("a_dec", "r_dec", "b_inv", "k_inv", "b_end", "k_end", "v")


def _seg_sum(x, seg):
    hw = seg.shape[0]
    parts = [piece[:, s * hw:(s + 1) * hw] for piece in _split3(x) for s in range(2)]
    out = jnp.dot(jnp.concatenate(parts, axis=0), seg, preferred_element_type=F32)
    m = x.shape[0]
    halves = [out[s * m:(s + 1) * m] + out[(2 + s) * m:(3 + s) * m] + out[(4 + s) * m:(5 + s) * m]
              for s in range(2)]
    return jnp.concatenate(halves, axis=1)


def _proj_ab_body(h_ref, g_ref, wa_ref, wb_ref, mu_ref, w0_ref, w2_ref, a0_ref, a2_ref, g2_ref, kk_ref, ka_ref,
                  rk_ref, seg_ref, pa_ref, sb_ref, sf_ref, gl_ref, prev_ref):
    L = RWKV_CHUNK
    W = RWKV_WIDTH
    tm = h_ref.shape[1]
    n_blocks = tm // L

    @pl.when(pl.program_id(1) == 0)
    def _():
        prev_ref[...] = jnp.zeros_like(prev_ref)

    ng = PROJ_ROW_GROUPS
    gm = tm // ng
    G = range(ng)
    rg = [slice(i * gm, (i + 1) * gm) for i in G]
    xn = [_rmsnorm(h_ref[0, rg[i], :], g_ref[...]).astype(BF16) for i in G]
    p = [jnp.dot(xn[i], wb_ref[...], preferred_element_type=F32) for i in G]

    n_a = wa_ref.shape[1]
    a_cols = [(0, 4 * LANES), (4 * LANES, 8 * LANES), (8 * LANES, n_a)]
    pending = [(i, c) for c in a_cols for i in G]

    def project_a(count):
        for _ in range(count):
            if pending:
                i, (lo_c, hi_c) = pending.pop(0)
                pa_ref[0, rg[i], lo_c:hi_c] = jnp.dot(xn[i], wa_ref[:, lo_c:hi_c], preferred_element_type=F32)

    first = lax.broadcasted_iota(jnp.int32, (SUBLANES, p[0].shape[1]), 0) == 0
    before = [prev_ref[0:1, :]] + [p[i][gm - 1:gm, :] for i in range(ng - 1)]
    rolled = [pltpu.roll(p[i], 1, 0) for i in G]
    sh = [jnp.concatenate([jnp.where(first, before[i], rolled[i][0:SUBLANES, :]), rolled[i][SUBLANES:gm, :]],
                          axis=0) for i in G]
    prev_ref[0:1, :] = p[ng - 1][gm - 1:gm, :]
    pf = [p[i] + (sh[i] - p[i]) * mu_ref[...] for i in G]
    r = [pf[i][:, 0:W] for i in G]
    k = [pf[i][:, W:2 * W] for i in G]
    v = [pf[i][:, 2 * W:3 * W] for i in G]
    wa = [pf[i][:, 3 * W:3 * W + LANES] for i in G]
    g_lr = [pf[i][:, 3 * W + LANES:3 * W + 2 * LANES] for i in G]
    log_w = [-RWKV_DECAY_SCALE * jax.nn.sigmoid(w0_ref[...] + _mm(jnp.tanh(wa[i]), w2_ref[...])) for i in G]
    a = [jax.nn.sigmoid(a0_ref[...] + _mm(wa[i], a2_ref[...])) for i in G]
    gate = [_mm(jax.nn.sigmoid(g_lr[i]), g2_ref[...]) for i in G]
    project_a(2)
    kk = [k[i] * kk_ref[...] for i in G]
    kk = [kk[i] * lax.rsqrt(_seg_sum(kk[i] * kk[i], seg_ref[...]) + RWKV_L2_EPS) for i in G]
    project_a(2)
    k = [k[i] * (1.0 + (a[i] - 1.0) * ka_ref[...]) for i in G]
    b_vec = [kk[i] * a[i] for i in G]
    bonus_sum = [_seg_sum(r[i] * k[i] * rk_ref[...], seg_ref[...]) for i in G]
    project_a(2)
    row_in_block = jnp.bitwise_and(lax.broadcasted_iota(jnp.int32, (gm, W), 0), L - 1)
    nbk = gm // L
    for i in G:
        g = log_w[i]
        d = 1
        while d < L:
            g = g + jnp.where(row_in_block >= d, pltpu.roll(g, d, 0), 0.0)
            d *= 2
        g_last = jnp.concatenate([jnp.broadcast_to(g[(j + 1) * L - 1:(j + 1) * L, :], (L, W)) for j in range(nbk)],
                                 axis=0)
        e_neg = jnp.exp(-g)
        e_end = jnp.exp(g_last - g)
        vals = dict(a_dec=-kk[i] * jnp.exp(g - log_w[i]), r_dec=r[i] * jnp.exp(g), b_inv=b_vec[i] * e_neg,
                    k_inv=k[i] * e_neg, b_end=b_vec[i] * e_end, k_end=k[i] * e_end, v=v[i])
        for j, name in enumerate(_RWKV_SB):
            sb_ref[0, rg[i], j * W:(j + 1) * W] = vals[name].astype(BF16)
        sf_ref[0, rg[i], 0:W] = gate[i]
        sf_ref[0, rg[i], W:2 * W] = bonus_sum[i] * v[i]
        for j in range(nbk):
            row0 = (i * nbk + j) * SUBLANES
            gl_ref[0, row0:row0 + SUBLANES, :] = g[(j + 1) * L - SUBLANES:(j + 1) * L, :]
        project_a(1)
    project_a(len(pending))


def proj_ab(h, g, wa, wb, mu, w0, w2, a0, a2, g2, k_k, k_a, r_k):
    b, t, d = h.shape
    tm = min(ROW_TILE, t)
    W = RWKV_WIDTH
    row = lambda x: x.reshape(1, -1)
    w2p = jnp.zeros((LANES, W), BF16).at[:RWKV_LORA_W].set(w2.astype(BF16))
    a2p = jnp.zeros((LANES, W), BF16).at[RWKV_LORA_W:].set(a2.astype(BF16))
    head_of = jnp.arange(W // 2) // RWKV_N
    seg = (head_of[:, None] == head_of[None, :]).astype(BF16)
    consts = [row(g), wa, wb, row(mu), row(w0), w2p, row(a0), a2p, g2.astype(BF16), row(k_k), row(k_a), row(r_k),
              seg]
    nsb = len(_RWKV_SB) * W
    tok = lambda w: pl.BlockSpec((1, tm, w), lambda i, j: (i, j, 0))
    return pl.pallas_call(
        _proj_ab_body,
        out_shape=[jax.ShapeDtypeStruct((b, t, wa.shape[1]), F32), jax.ShapeDtypeStruct((b, t, nsb), BF16),
                   jax.ShapeDtypeStruct((b, t, 2 * W), F32),
                   jax.ShapeDtypeStruct((b, t // RWKV_CHUNK * SUBLANES, W), F32)],
        grid=(b, t // tm),
        in_specs=[tok(d)] + [_const_spec(x.shape) for x in consts],
        out_specs=[tok(wa.shape[1]), tok(nsb), tok(2 * W),
                   pl.BlockSpec((1, tm // RWKV_CHUNK * SUBLANES, W), lambda i, j: (i, j, 0))],
        scratch_shapes=[pltpu.VMEM((SUBLANES, wb.shape[1]), F32)],
        compiler_params=_params(("parallel", "arbitrary")),
        name="proj_ab",
    )(h, *consts)


def _rwkv_rec_body(sb_ref, sf_ref, gl_ref, lng_ref, lnb_ref, seg_ref, tri_ref, quad_ref, o_ref, sp_ref):
    L = RWKV_CHUNK
    W = RWKV_WIDTH
    N = RWKV_N
    nb = sb_ref.shape[0]
    n_chunks = sb_ref.shape[1] // L
    n_pairs = RWKV_HEADS // 2
    idx = {name: j for j, name in enumerate(_RWKV_SB)}

    @pl.when(pl.program_id(1) == 0)
    def _():
        sp_ref[...] = jnp.zeros_like(sp_ref)

    def chunk(c, carry):
        lane = lax.broadcasted_iota(jnp.int32, (L, LANES), 1)
        lo = lane < N
        hi = lane >= N
        nat = (lo, hi)
        strict_f = tri_ref[0]
        incl_f = tri_ref[1]
        eye2 = tri_ref[2]
        diag_f = quad_ref[0]
        anti_f = quad_ref[1]
        zb = jnp.zeros((L, LANES), BF16)
        sel = lambda m, x: jnp.where(m, x, jnp.zeros_like(x))
        swap = lambda x: pltpu.roll(x, N, 1)
        cat0 = lambda xs: jnp.concatenate(xs, axis=0)
        cat1 = lambda xs: jnp.concatenate(xs, axis=1)
        dot = lambda x, y: jnp.dot(x, y, preferred_element_type=F32)
        nt = lambda x, y: lax.dot_general(x, y, (((1,), (1,)), ((), ())), preferred_element_type=F32)
        rows = pl.ds(pl.multiple_of(c * L, L), L)
        U = [(b, p) for b in range(nb) for p in range(n_pairs)]
        nu = range(len(U))

        def blk(name, u):
            col = idx[name] * W + u[1] * LANES
            return sb_ref[u[0], rows, col:col + LANES]

        vn = lambda i, h: sel(nat[h], blk("v", U[i]))
        ad = [blk("a_dec", u) for u in U]
        rd = [blk("r_dec", u) for u in U]
        b_inv = [blk("b_inv", u) for u in U]
        k_inv = [blk("k_inv", u) for u in U]
        pe = [nt(cat0([sel(lo, ad[i]), sel(lo, rd[i])]), cat0([b_inv[i], k_inv[i]])) for i in nu]
        po = [nt(cat0([sel(hi, ad[i]), sel(hi, rd[i])]), cat0([k_inv[i], b_inv[i]])) for i in nu]
        aa = [[pe[i][0:L, :] * strict_f, po[i][0:L, :] * strict_f] for i in nu]
        ar = [[(pe[i][L:2 * L, :] * incl_f).astype(BF16), (po[i][L:2 * L, :] * incl_f).astype(BF16)] for i in nu]
        t0 = [dot(jnp.where(lo, aa[i][1], aa[i][0]).astype(BF16),
                  cat0([cat1([vn(i, 1), zb]), cat1([zb, vn(i, 0)])])) for i in nu]
        ad_sw = [swap(x) for x in ad]
        x0 = [[jnp.where(hi, ad_sw[i], t0[i][:, LANES:2 * LANES].astype(BF16)),
               jnp.where(lo, ad_sw[i], t0[i][:, 0:LANES].astype(BF16))] for i in nu]
        q0 = [jnp.where(lo, aa[i][0], eye2) for i in nu]
        q1 = [jnp.where(lo, eye2, aa[i][1]) for i in nu]
        for _ in range(6):
            out = [dot(jnp.where(lo, q0[i], q1[i]).astype(BF16),
                       cat0([cat1([q0[i].astype(BF16), zb]), cat1([zb, q1[i].astype(BF16)])])) for i in nu]
            q0 = [out[i][:, 0:LANES] + sel(hi, q0[i]) for i in nu]
            q1 = [out[i][:, LANES:2 * LANES] + sel(lo, q1[i]) for i in nu]
        tx = [dot(jnp.where(lo, q1[i], q0[i]).astype(BF16),
                  cat0([cat1([x0[i][1], zb]), cat1([zb, x0[i][0]])])) for i in nu]
        xb = [[tx[i][:, LANES:2 * LANES].astype(BF16), tx[i][:, 0:LANES].astype(BF16)] for i in nu]
        rd_sw = [swap(blk("r_dec", u)).astype(F32) for u in U]
        rmy = [[sel(hi, rd_sw[i]) + dot(ar[i][0], cat0([xb[i][0], vn(i, 0)])),
                sel(lo, rd_sw[i]) + dot(ar[i][1], cat0([vn(i, 1), xb[i][1]]))] for i in nu]
        be_sw = [swap(blk("b_end", u)) for u in U]
        ke_sw = [swap(blk("k_end", u)) for u in U]
        gg = [lax.dot_general(cat0([xb[i][0], xb[i][1], vn(i, 0), vn(i, 1)]),
                              cat0([sel(hi, be_sw[i]), sel(lo, be_sw[i]), sel(hi, ke_sw[i]), sel(lo, ke_sw[i])]),
                              (((0,), (0,)), ((), ())), preferred_element_type=F32) for i in nu]
        sp = [sp_ref[i] for i in nu]
        ys = [_mm_nt(jnp.where(lo, rmy[i][1], rmy[i][0]), sp[i]) + jnp.where(lo, rmy[i][0], rmy[i][1]) for i in nu]
        g_rows = pl.ds(pl.multiple_of(c * SUBLANES, SUBLANES), SUBLANES)
        for i, u in enumerate(U):
            sd_sw = jnp.exp(swap(gl_ref[u[0], g_rows, u[1] * LANES:(u[1] + 1) * LANES])[SUBLANES - 1:SUBLANES, :])
            sp_ref[i] = sp[i] * sd_sw + _mm(sp[i], gg[i] * diag_f) + gg[i] * anti_f
        y = cat0([cat1(ys[b * n_pairs:(b + 1) * n_pairs]) for b in range(nb)])
        mean = _seg_sum(y, seg_ref[...]) * (1.0 / N)
        dy = y - mean
        var = _seg_sum(dy * dy, seg_ref[...]) * (1.0 / N)
        y = dy * lax.rsqrt(var + RWKV_GN_EPS) * lng_ref[...] + lnb_ref[...]
        for b in range(nb):
            yb = y[b * L:(b + 1) * L, :]
            o_ref[b, rows, :] = (yb + sf_ref[b, rows, W:2 * W]) * sf_ref[b, rows, 0:W]
        return carry

    lax.fori_loop(0, n_chunks, chunk, 0)


def rwkv_recurrence(sb, sf, gl, ln_g, ln_b):
    b, t, _ = sb.shape
    tt = min(RWKV_TILE, t)
    nb = min(RWKV_BATCH_PER_STEP, b)
    W = RWKV_WIDTH
    row = lambda x: x.reshape(1, -1)
    head_of = jnp.arange(W // 2) // RWKV_N
    seg = (head_of[:, None] == head_of[None, :]).astype(BF16)
    t_idx = jnp.arange(RWKV_CHUNK)[:, None]
    s_idx = jnp.arange(LANES)[None, :] % RWKV_N
    tri = jnp.stack([t_idx > s_idx, t_idx >= s_idx, t_idx == s_idx]).astype(F32)
    half = jnp.arange(LANES) // RWKV_N
    same_half = half[:, None] == half[None, :]
    quad = jnp.stack([same_half, ~same_half]).astype(F32)
    consts = [row(ln_g), row(ln_b), seg, tri, quad]
    tok = lambda w: pl.BlockSpec((nb, tt, w), lambda i, j: (i, j, 0))
    return pl.pallas_call(
        _rwkv_rec_body,
        out_shape=jax.ShapeDtypeStruct((b, t, W), F32),
        grid=(b // nb, t // tt),
        in_specs=[tok(sb.shape[2]), tok(sf.shape[2]),
                  pl.BlockSpec((nb, tt // RWKV_CHUNK * SUBLANES, W), lambda i, j: (i, j, 0))]
        + [_const_spec(x.shape) for x in consts],
        out_specs=tok(W),
        scratch_shapes=[pltpu.VMEM((nb * RWKV_HEADS // 2, LANES, LANES), F32)],
        compiler_params=_params(("parallel", "arbitrary")),
        name="rwkv_mixer",
    )(sb, sf, gl, *consts)


def _rglru_body(p_ref, cw_ref, cb_ref, gw_ref, gb_ref, lam_ref, o_ref, xprev_ref, hprev_ref):
    tt = p_ref.shape[1]
    W = LRU_WIDTH

    @pl.when(pl.program_id(1) == 0)
    def _():
        xprev_ref[0:SUBLANES, :] = jnp.zeros((SUBLANES, W), F32)
        hprev_ref[...] = jnp.zeros_like(hprev_ref)

    x = p_ref[0, :, 0:W]
    gate = p_ref[0, :, W:2 * W]
    xc = _causal_conv(x, xprev_ref, cw_ref[...], cb_ref[...], LRU_CONV)
    xcb = xc.astype(BF16)
    pre = []
    for gi in range(2):
        pre.append(jnp.concatenate(
            [jnp.dot(xcb[:, n * LRU_BLOCK:(n + 1) * LRU_BLOCK], gw_ref[gi, n], preferred_element_type=F32)
             for n in range(LRU_BLOCKS)], axis=1) + gb_ref[gi:gi + 1, :])
    r_gate = jax.nn.sigmoid(pre[0])
    i_gate = jax.nn.sigmoid(pre[1])
    log_a = -LRU_C * r_gate * _softplus(-lam_ref[...])
    a = jnp.exp(log_a)
    w2 = -jnp.tanh(log_a) * (a * a + 1.0)
    u = jnp.where(w2 > 0.0, w2 * lax.rsqrt(w2), 0.0) * (i_gate * xc)
    row_in_group = jnp.bitwise_and(lax.broadcasted_iota(jnp.int32, (tt, W), 0), SUBLANES - 1)
    d = 1
    while d < SUBLANES:
        keep = row_in_group >= d
        u = u + a * jnp.where(keep, pltpu.roll(u, d, 0), 0.0)
        a = a * jnp.where(keep, pltpu.roll(a, d, 0), 1.0)
        d *= 2
    carry = hprev_ref[0:1, :]
    hs = []
    for g in range(tt // SUBLANES):
        rs = slice(g * SUBLANES, (g + 1) * SUBLANES)
        hg = u[rs, :] + a[rs, :] * carry
        carry = hg[SUBLANES - 1:SUBLANES, :]
        hs.append(hg)
    hprev_ref[0:1, :] = carry
    o_ref[0] = jnp.concatenate(hs, axis=0) * jax.nn.gelu(gate)


def rglru_mixer(pc, conv_w, conv_b, gate_w, gate_b, lam):
    b, t, cols = pc.shape
    tt = min(MIX_TILE, t)
    W = LRU_WIDTH
    cw = jnp.zeros((SUBLANES, W), F32).at[:LRU_CONV].set(conv_w)
    consts = [cw, conv_b.reshape(1, W), gate_w.astype(BF16), gate_b, lam.reshape(1, W)]
    return pl.pallas_call(
        _rglru_body,
        out_shape=jax.ShapeDtypeStruct((b, t, W), F32),
        grid=(b, t // tt),
        in_specs=[pl.BlockSpec((1, tt, cols), lambda i, j: (i, j, 0))] + [_const_spec(x.shape) for x in consts],
        out_specs=pl.BlockSpec((1, tt, W), lambda i, j: (i, j, 0)),
        scratch_shapes=[pltpu.VMEM((SUBLANES + tt, W), F32), pltpu.VMEM((SUBLANES, W), F32)],
        compiler_params=_params(("parallel", "arbitrary")),
        name="rglru_mixer",
    )(pc, *consts)


def _mlstm_body(p_ref, cw_ref, cb_ref, wq_ref, wk_ref, wv_ref, bif_ref, ng_ref, o_ref,
                xprev_ref, c_ref, m_ref, q_s, k_s, v_s):
    tt = p_ref.shape[1]
    L = MLSTM_CHUNK
    W = MLSTM_WIDTH
    DH = MLSTM_DH
    H = MLSTM_HEADS
    n_chunks = tt // L

    @pl.when(pl.program_id(1) == 0)
    def _():
        xprev_ref[:, 0:SUBLANES, :] = jnp.zeros((xprev_ref.shape[0], SUBLANES, W), F32)
        c_ref[...] = jnp.zeros_like(c_ref)
        m_ref[...] = jnp.zeros_like(m_ref)

    nb = p_ref.shape[0]
    for b in range(nb):
        x = p_ref[b, :, 0:W]
        xc = _silu(_causal_conv(x, xprev_ref.at[b], cw_ref[...], cb_ref[...], MLSTM_CONV))
        xcb = xc.astype(BF16)
        xb = x.astype(BF16)
        half = W // 2
        for s in range(2):
            cs = slice(s * half, (s + 1) * half)
            q_s[b, :, cs] = jnp.dot(xcb[:, cs], wq_ref[s], preferred_element_type=F32)
            k_s[b, :, cs] = jnp.dot(xcb[:, cs], wk_ref[s], preferred_element_type=F32) * DH ** -0.5
            v_s[b, :, cs] = jnp.dot(xb[:, cs], wv_ref[s], preferred_element_type=F32)

    def chunk(c, carry):
        incl, _ = _tri_masks(L)
        tri01 = jnp.where(incl, 1.0, 0.0).astype(BF16)
        t0 = pl.multiple_of(c * L, L)
        rows = pl.ds(t0, L)
        nt = lambda y: lax.dot_general(y, tri01, (((1,), (1,)), ((), ())), preferred_element_type=F32)
        gates, lsg, g_t, b_rows = [], [], [], []
        for b in range(nb):
            gates.append(p_ref[b, rows, 2 * W:2 * W + LANES] + bif_ref[...])
            lsg.append(_log_sigmoid(gates[b]))
            g_t.append(gates[b].T[0:SUBLANES, :])
            x1, x2, x3 = _split3(_log_sigmoid(g_t[b]))
            b_rows.append(nt(x1) + nt(x2) + nt(x3))
        ones_b = jnp.ones((L, LANES), BF16)
        U = [(b, h) for b in range(nb) for h in range(H)]
        nu = range(len(U))
        hs = [slice(h * DH, (h + 1) * DH) for h in range(H)]
        qh = [q_s[b, rows, hs[h]] for b, h in U]
        kh = [k_s[b, rows, hs[h]] for b, h in U]
        v_aug = [jnp.concatenate([v_s[b, rows, hs[h]].astype(BF16), ones_b], axis=1) for b, h in U]
        cm = [c_ref[i] for i in nu]
        m = [m_ref[i, 0:1, :] for i in nu]
        i_rep = [jnp.broadcast_to(gates[b][:, h:h + 1], (L, LANES)) for b, h in U]
        lf_rep = [jnp.broadcast_to(lsg[b][:, H + h:H + h + 1], (L, LANES)) for b, h in U]
        b_rep = [_mm_exact_lhs01(tri01, lf_rep[i]) for i in nu]
        qk = [_mm_nt(qh[i], kh[i]) for i in nu]
        b_last = [b_rep[i][L - 1:L, :] for i in nu]
        log_e = [b_last[i] - b_rep[i] + i_rep[i] for i in nu]
        m_end = [jnp.max(log_e[i], axis=0, keepdims=True) for i in nu]
        m_new = [jnp.maximum(b_last[i] + m[i], m_end[i]) for i in nu]
        kin = [kh[i] * (jnp.exp(m_end[i] - m_new[i]) * jnp.exp(log_e[i] - m_end[i])) for i in nu]
        c_upd = [_mm_tn(kin[i], v_aug[i]) for i in nu]
        run = [i_rep[i] - b_rep[i] for i in nu]
        d = 1
        while d < L:
            run = [jnp.maximum(run[i], _shift_rows(run[i], d, -jnp.inf)) for i in nu]
            d *= 2
        m_t = [jnp.maximum(b_rep[i] + m[i], b_rep[i] + run[i]) for i in nu]
        log_d = [jnp.where(incl, b_rep[i][:, 0:L] - b_rows[b][H + h:H + h + 1, :] + g_t[b][h:h + 1, :], -jnp.inf)
                 for i, (b, h) in enumerate(U)]
        w_loc = [jnp.exp(log_d[i] - m_t[i][:, 0:L]) * qk[i] for i in nu]
        lhs = [jnp.concatenate([(jnp.exp(b_rep[i] + m[i] - m_t[i]) * qh[i]).astype(BF16), w_loc[i].astype(BF16)],
                               axis=1) for i in nu]
        nd = [jnp.dot(lhs[i], jnp.concatenate([cm[i].astype(BF16), v_aug[i]], axis=0),
                      preferred_element_type=F32) for i in nu]
        for i, (b, h) in enumerate(U):
            c_state = jnp.exp(b_last[i] + m[i] - m_new[i])
            c_ref[i] = jnp.concatenate([c_state, c_state], axis=1) * cm[i] + c_upd[i]
            m_ref[i] = jnp.broadcast_to(m_new[i], (SUBLANES, LANES))
            hv = nd[i][:, 0:DH] / jnp.maximum(jnp.abs(nd[i][:, DH:2 * DH]), jnp.exp(-m_t[i]))
            hv = hv * lax.rsqrt(jnp.mean(hv * hv, axis=-1, keepdims=True) + NORM_EPS)
            o_ref[b, rows, hs[h]] = (jax.nn.sigmoid(p_ref[b, rows, W + h * DH:W + (h + 1) * DH]) * hv
                                     * ng_ref[:, hs[h]])
        return carry

    lax.fori_loop(0, n_chunks, chunk, 0)


def _block_diag(w):
    nb, d, e = w.shape
    rows = jnp.tile(w.reshape(nb * d, e), (1, nb))
    same_block = (jnp.arange(nb * d)[:, None] // d) == (jnp.arange(nb * e)[None, :] // e)
    return jnp.where(same_block, rows, 0.0)


def mlstm_mixer(pd, conv_w, conv_b, qkv_w, b_if, norm_g):
    b, t, cols = pd.shape
    tt = min(MIX_TILE, t)
    nb = min(MLSTM_BATCH_PER_STEP, b)
    W = MLSTM_WIDTH
    cw = jnp.zeros((SUBLANES, W), F32).at[:MLSTM_CONV].set(conv_w)
    bif = jnp.zeros((1, LANES), F32).at[0, :2 * MLSTM_HEADS].set(b_if)
    nblk = qkv_w.shape[1] // 2
    diag_tiles = lambda w: jnp.stack([_block_diag(w[:nblk]), _block_diag(w[nblk:])]).astype(BF16)
    consts = [cw, conv_b.reshape(1, W)] + [diag_tiles(qkv_w[i]) for i in range(3)] + [
        bif, norm_g.reshape(1, W)]
    return pl.pallas_call(
        _mlstm_body,
        out_shape=jax.ShapeDtypeStruct((b, t, W), F32),
        grid=(b // nb, t // tt),
        in_specs=[pl.BlockSpec((nb, tt, cols), lambda i, j: (i, j, 0))] + [_const_spec(x.shape) for x in consts],
        out_specs=pl.BlockSpec((nb, tt, W), lambda i, j: (i, j, 0)),
        scratch_shapes=[pltpu.VMEM((nb, SUBLANES + tt, W), F32),
                        pltpu.VMEM((nb * MLSTM_HEADS, MLSTM_DH, 2 * MLSTM_DH), F32),
                        pltpu.VMEM((nb * MLSTM_HEADS, SUBLANES, LANES), F32),
                        pltpu.VMEM((nb, tt, W), F32), pltpu.VMEM((nb, tt, W), F32), pltpu.VMEM((nb, tt, W), F32)],
        compiler_params=_params(("parallel", "arbitrary")),
        name="mlstm_mixer",
    )(pd, *consts)


def _mix_xattn_body(h_ref, ya_ref, yb_ref, wm_ref, g_ref, wq_ref, k_ref, v_ref, wo_ref, o_ref):
    tm = h_ref.shape[1]
    ng = XATTN_ROW_GROUPS
    rg = [slice(i * (tm // ng), (i + 1) * (tm // ng)) for i in range(ng)]
    G = range(ng)
    HD = range(XA_HEADS)
    hs = [slice(hd * XA_DH, (hd + 1) * XA_DH) for hd in HD]
    mixed = [jnp.concatenate([ya_ref[0, rg[i], :], yb_ref[0, rg[i], :]], axis=1).astype(BF16) for i in G]
    h = [h_ref[0, rg[i], :] + jnp.dot(mixed[i], wm_ref[...], preferred_element_type=F32) for i in G]
    xn = [_rmsnorm(h[i], g_ref[...]).astype(BF16) for i in G]
    q = [jnp.dot(xn[i], wq_ref[...], preferred_element_type=F32) for i in G]
    s = [[_mm_nt(q[i][:, hs[hd]], k_ref[0, :, hs[hd]]) * XA_DH ** -0.5 for hd in HD] for i in G]
    e = [[jnp.exp(s[i][hd] - jnp.max(s[i][hd], axis=-1, keepdims=True)) for hd in HD] for i in G]
    p = [[e[i][hd] / jnp.sum(e[i][hd], axis=-1, keepdims=True) for hd in HD] for i in G]
    pv = [[_mm(p[i][hd], v_ref[0, :, hs[hd]]) for hd in HD] for i in G]
    for i in G:
        o = jnp.concatenate(pv[i], axis=1).astype(BF16)
        o_ref[0, rg[i], :] = h[i] + jnp.dot(o, wo_ref[...], preferred_element_type=F32)


def mix_xattn(h, ya, yb, w_mix, g, wq, kv, wo):
    b, t, d = h.shape
    tm = min(XATTN_TILE, t)
    tok = lambda w: pl.BlockSpec((1, tm, w), lambda i, j: (i, j, 0))
    mem_k = pl.BlockSpec((1, MEM_LEN, d), lambda i, j: (i, 0, 0))
    mem_v = pl.BlockSpec((1, MEM_LEN, d), lambda i, j: (i, 0, 1))
    return pl.pallas_call(
        _mix_xattn_body,
        out_shape=jax.ShapeDtypeStruct((b, t, d), F32),
        grid=(b, t // tm),
        in_specs=[tok(d), tok(ya.shape[-1]), tok(yb.shape[-1]), _const_spec(w_mix.shape), _const_spec((1, d)),
                  _const_spec(wq.shape), mem_k, mem_v, _const_spec(wo.shape)],
        out_specs=tok(d),
        compiler_params=_params(("parallel", "parallel")),
        name="mix_xattn",
    )(h, ya, yb, w_mix, g.reshape(1, d), wq, kv, kv, wo)


def _ffn_body(final_norm, h_ref, g_ref, wu_ref, wg_ref, cw_ref, cb_ref, wd_ref, fg_ref, o_ref, uprev_ref):
    tm = h_ref.shape[1]

    @pl.when(pl.program_id(1) == 0)
    def _():
        uprev_ref[...] = jnp.zeros_like(uprev_ref)

    h = h_ref[0]
    xn = _rmsnorm(h, g_ref[...]).astype(BF16)
    acc = h
    start = 0
    for width in FFN_COL_GROUPS:
        cs = slice(start, start + width)
        start += width
        u = jnp.dot(xn, wu_ref[:, cs], preferred_element_type=F32)
        gt = jnp.dot(xn, wg_ref[:, cs], preferred_element_type=F32)
        uu = jnp.concatenate([uprev_ref[:, cs], u], axis=0)
        uprev_ref[:, cs] = u[tm - SUBLANES:tm, :]
        c = cb_ref[:, cs] + cw_ref[2:3, cs] * u
        c = c + cw_ref[1:2, cs] * pltpu.roll(uu, 1, 0)[SUBLANES:SUBLANES + tm, :]
        c = c + cw_ref[0:1, cs] * pltpu.roll(uu, 2, 0)[SUBLANES:SUBLANES + tm, :]
        act = (_silu(c) * gt).astype(BF16)
        acc = acc + jnp.dot(act, wd_ref[cs, :], preferred_element_type=F32)
    if final_norm:
        acc = _rmsnorm(acc, fg_ref[...])
    o_ref[0] = acc


def ffn(h, g, wu, wg, conv_w, conv_b, wd, final_g, final_norm):
    b, t, d = h.shape
    tm = min(ROW_TILE, t)
    tok = pl.BlockSpec((1, tm, d), lambda i, j: (i, j, 0))
    consts = [g.reshape(1, d), wu, wg, conv_w, conv_b, wd, final_g.reshape(1, d)]
    return pl.pallas_call(
        functools.partial(_ffn_body, final_norm),
        out_shape=jax.ShapeDtypeStruct((b, t, d), F32),
        grid=(b, t // tm),
        in_specs=[tok] + [_const_spec(x.shape) for x in consts],
        out_specs=tok,
        scratch_shapes=[pltpu.VMEM((SUBLANES, D_FF_PAD), F32)],
        compiler_params=_params(("parallel", "arbitrary")),
        name="ffn",
    )(h, *consts)


def _pad_cols(w, n):
    return jnp.pad(w, ((0, 0), (0, n - w.shape[1])))


def _rwkv_perm(x):
    W = RWKV_WIDTH
    o_w, o_k, o_v = W, W + RWKV_LORA_W, 2 * W + RWKV_LORA_W
    o_a = 3 * W + RWKV_LORA_W
    o_g = o_a + RWKV_LORA_A
    return jnp.concatenate([x[..., :W], x[..., o_k:o_k + W], x[..., o_v:o_v + W], x[..., o_w:o_w + RWKV_LORA_W],
                            x[..., o_a:o_a + RWKV_LORA_A], x[..., o_g:]], axis=-1)


def kernel(x, mem, mem_norm_g, norm_mix_g, ab_w_in, gla_w_alpha2, gla_b_alpha, gla_norm_g, rwkv_mu, rwkv_w0, rwkv_w2, rwkv_a0, rwkv_a2, rwkv_g2, rwkv_k_k, rwkv_k_a, rwkv_r_k, rwkv_ln_g, rwkv_ln_b, cd_w_in, lru_conv_w, lru_conv_b, lru_gate_w, lru_gate_b, lru_lambda, mlstm_conv_w, mlstm_conv_b, mlstm_qkv_w, mlstm_b_if, mlstm_norm_g, w_mix_out, norm_xattn_g, xattn_wq, xattn_wkv, xattn_wo, norm_ffn_g, ffn_w_up, ffn_conv_w, ffn_conv_b, ffn_w_down, final_norm_g):
    b, t, d = x.shape
    depth = norm_mix_g.shape[0]
    n = b * t
    h = x
    mem2d = mem.reshape(b * MEM_LEN, d)
    for layer in range(depth):
        j = layer // 2
        h2d = h.reshape(n, d)
        if layer % 2 == 0:
            w = ab_w_in[j]
            wa = _pad_cols(w[:, :GLA_COLS], GLA_COLS_PAD).astype(BF16)
            wb = _rwkv_perm(w[:, GLA_COLS:]).astype(BF16)
            pa, sb, sf, gl = proj_ab(h, norm_mix_g[layer], wa, wb, _rwkv_perm(rwkv_mu[j]), rwkv_w0[j], rwkv_w2[j],
                                     rwkv_a0[j], rwkv_a2[j], rwkv_g2[j], rwkv_k_k[j], rwkv_k_a[j], rwkv_r_k[j])
            ya = gla_mixer(pa, gla_w_alpha2[j], gla_b_alpha[j], gla_norm_g[j])
            yb = rwkv_recurrence(sb, sf, gl, rwkv_ln_g[j], rwkv_ln_b[j])
        else:
            w = cd_w_in[j]
            wc = w[:, :2 * LRU_WIDTH].astype(BF16)
            wd_ = _pad_cols(w[:, 2 * LRU_WIDTH:], MLSTM_COLS_PAD).astype(BF16)
            pc, pd = norm_matmul(h2d, norm_mix_g[layer], [wc, wd_], name="proj_cd")
            ya = rglru_mixer(pc.reshape(b, t, -1), lru_conv_w[j], lru_conv_b[j], lru_gate_w[j], lru_gate_b[j],
                             lru_lambda[j])
            yb = mlstm_mixer(pd.reshape(b, t, -1), mlstm_conv_w[j], mlstm_conv_b[j], mlstm_qkv_w[j],
                             mlstm_b_if[j], mlstm_norm_g[j])
        (kv,) = norm_matmul(mem2d, mem_norm_g, [xattn_wkv[layer].astype(BF16)], out_dtype=BF16, name="proj_kv")
        kv = kv.reshape(b, MEM_LEN, 2 * d)
        h = mix_xattn(h, ya, yb, w_mix_out[layer].astype(BF16), norm_xattn_g[layer], xattn_wq[layer].astype(BF16),
                      kv, xattn_wo[layer].astype(BF16))
        wup = ffn_w_up[layer]
        wu = _pad_cols(wup[:, :D_FF], D_FF_PAD).astype(BF16)
        wg = _pad_cols(wup[:, D_FF:], D_FF_PAD).astype(BF16)
        cw = _pad_cols(jnp.pad(ffn_conv_w[layer], ((0, SUBLANES - FFN_CONV), (0, 0))), D_FF_PAD)
        cb = _pad_cols(ffn_conv_b[layer].reshape(1, D_FF), D_FF_PAD)
        wdn = jnp.pad(ffn_w_down[layer], ((0, D_FF_PAD - D_FF), (0, 0))).astype(BF16)
        h = ffn(h, norm_ffn_g[layer], wu, wg, cw, cb, wdn, final_norm_g, layer == depth - 1)
    return h
```

```python
import functools

import jax
import jax.numpy as jnp
from jax import lax
from jax.experimental import pallas as pl
from jax.experimental.pallas import tpu as pltpu

F32 = jnp.float32
BF16 = jnp.bfloat16

D_MODEL = 1024
NORM_EPS = 1e-6
LANES = 128
SUBLANES = 8
VMEM_LIMIT_BYTES = 56 * 1024 * 1024

GLA_HEADS, GLA_DK, GLA_DV, GLA_RANK, GLA_TAU, GLA_CHUNK = 4, 64, 128, 16, 16.0, 64
GLA_COLS = 2 * GLA_HEADS * GLA_DK + 2 * GLA_HEADS * GLA_DV + GLA_RANK
GLA_COLS_PAD = 13 * LANES

RWKV_HEADS, RWKV_N, RWKV_WIDTH = 8, 64, 512
RWKV_LORA_W, RWKV_LORA_A, RWKV_LORA_G = 64, 64, 128
RWKV_CHUNK = 64
RWKV_DECAY_SCALE = 0.6065306597126334
RWKV_GN_EPS = RWKV_N * 1e-5
RWKV_L2_EPS = 1e-12
RWKV_COLS = 3 * RWKV_WIDTH + RWKV_LORA_W + RWKV_LORA_A + RWKV_LORA_G

LRU_WIDTH, LRU_BLOCKS, LRU_BLOCK, LRU_C, LRU_CONV = 512, 4, 128, 8.0, 4
MLSTM_HEADS, MLSTM_DH, MLSTM_WIDTH, MLSTM_CONV, MLSTM_CHUNK = 4, 128, 512, 4, 64
MLSTM_COLS_PAD = 2 * MLSTM_WIDTH + LANES
MLSTM_IN_COLS = 4 * MLSTM_WIDTH + LANES

XA_HEADS, XA_DH, MEM_LEN = 4, 256, 256
D_FF, FFN_CONV = 2752, 3
D_FF_PAD = 22 * LANES
MXU_K_TILE = 256
FFN_COL_GROUPS = (6 * MXU_K_TILE, 5 * MXU_K_TILE)

MIX_TILE = 256
RWKV_TILE = 256
GLA_BATCH_PER_STEP = 4
MLSTM_BATCH_PER_STEP = 2
RWKV_BATCH_PER_STEP = 4
ROW_TILE = 512
XATTN_TILE = 1024
XATTN_ROW_GROUPS = 2
PROJ_ROW_GROUPS = 2


def _mm(a, b):
    return jnp.dot(a.astype(BF16), b.astype(BF16), preferred_element_type=F32)


def _mm_nt(a, b):
    return lax.dot_general(a.astype(BF16), b.astype(BF16), (((1,), (1,)), ((), ())), preferred_element_type=F32)


def _mm_tn(a, b):
    return lax.dot_general(a.astype(BF16), b.astype(BF16), (((0,), (0,)), ((), ())), preferred_element_type=F32)


def _split3(x):
    x1 = x.astype(BF16)
    r1 = x - x1.astype(F32)
    x2 = r1.astype(BF16)
    x3 = (r1 - x2.astype(F32)).astype(BF16)
    return x1, x2, x3


def _mm_exact_lhs01(m01, x):
    x1, x2, x3 = _split3(x)
    d = lambda y: jnp.dot(m01, y, preferred_element_type=F32)
    return d(x1) + d(x2) + d(x3)


def _rmsnorm(x, g):
    return x * lax.rsqrt(jnp.mean(x * x, axis=-1, keepdims=True) + NORM_EPS) * g


def _log_sigmoid(x):
    return jnp.minimum(x, 0.0) - jnp.log1p(jnp.exp(-jnp.abs(x)))


def _softplus(x):
    return jnp.maximum(x, 0.0) + jnp.log1p(jnp.exp(-jnp.abs(x)))


def _silu(x):
    return x * jax.nn.sigmoid(x)


def _tri_masks(n):
    r = lax.broadcasted_iota(jnp.int32, (n, n), 0)
    c = lax.broadcasted_iota(jnp.int32, (n, n), 1)
    return r >= c, r > c


def _shift_rows(x, s, fill):
    if s == 0:
        return x
    rolled = pltpu.roll(x, s, 0)
    row = lax.broadcasted_iota(jnp.int32, x.shape, 0)
    return jnp.where(row >= s, rolled, fill)


def _causal_conv(x, buf_ref, w, b, width):
    t = x.shape[0]
    buf_ref[SUBLANES:SUBLANES + t, :] = x
    y = b + w[width - 1:width, :] * x
    for j in range(width - 1):
        s = width - 1 - j
        y = y + w[j:j + 1, :] * buf_ref[SUBLANES - s:SUBLANES - s + t, :]
    buf_ref[0:SUBLANES, :] = x[t - SUBLANES:t, :]
    return y


def _const_spec(shape):
    nd = len(shape)
    return pl.BlockSpec(shape, lambda *_: (0,) * nd, pipeline_mode=pl.Buffered(1))


def _params(sem):
    return pltpu.CompilerParams(dimension_semantics=sem, vmem_limit_bytes=VMEM_LIMIT_BYTES)


def _norm_matmul_body(n_out, x_ref, g_ref, *refs):
    xn = _rmsnorm(x_ref[...], g_ref[...]).astype(BF16)
    for w_ref, o_ref in zip(refs[:n_out], refs[n_out:]):
        o_ref[...] = jnp.dot(xn, w_ref[...], preferred_element_type=F32).astype(o_ref.dtype)


def norm_matmul(x2d, g, ws, out_dtype=F32, name="norm_matmul"):
    n, d = x2d.shape
    tm = min(ROW_TILE, n)
    assert n % tm == 0
    return pl.pallas_call(
        functools.partial(_norm_matmul_body, len(ws)),
        out_shape=[jax.ShapeDtypeStruct((n, w.shape[1]), out_dtype) for w in ws],
        grid=(n // tm,),
        in_specs=[pl.BlockSpec((tm, d), lambda i: (i, 0)), _const_spec((1, d))]
        + [_const_spec(w.shape) for w in ws],
        out_specs=[pl.BlockSpec((tm, w.shape[1]), lambda i: (i, 0)) for w in ws],
        compiler_params=_params(("parallel",)),
        name=name,
    )(x2d, g.reshape(1, d), *ws)


def _gla_body(p_ref, wa_ref, ba_ref, ng_ref, o_ref, st_ref):
    L = GLA_CHUNK
    nb = p_ref.shape[0]
    n_chunks = p_ref.shape[1] // L
    M = nb * L
    hk = GLA_HEADS * GLA_DK
    hv = GLA_HEADS * GLA_DV

    @pl.when(pl.program_id(1) == 0)
    def _():
        st_ref[...] = jnp.zeros_like(st_ref)

    def chunk(c, carry):
        incl, _ = _tri_masks(L)
        ri = lax.broadcasted_iota(jnp.int32, (M, M), 0)
        ci = lax.broadcasted_iota(jnp.int32, (M, M), 1)
        same_seq = jnp.bitwise_and(ri, -L) == jnp.bitwise_and(ci, -L)
        tri01 = jnp.where(same_seq & (ri >= ci), 1.0, 0.0).astype(BF16)
        t0 = pl.multiple_of(c * L, L)
        rows = pl.ds(t0, L)
        cols = lambda lo, hi: jnp.concatenate([p_ref[b, rows, lo:hi] for b in range(nb)], axis=0)
        q = cols(0, hk)
        k = cols(hk, 2 * hk)
        v = cols(2 * hk, 2 * hk + hv)
        gt = cols(2 * hk + hv, 2 * hk + 2 * hv)
        a_lr = cols(2 * hk + 2 * hv, GLA_COLS_PAD)
        la = _log_sigmoid(_mm(a_lr, wa_ref[...]) + ba_ref[...]) * (1.0 / GLA_TAU)
        g = _mm_exact_lhs01(tri01, la)
        g_last = jnp.concatenate([jnp.broadcast_to(g[(b + 1) * L - 1:(b + 1) * L, :], (L, hk)) for b in range(nb)],
                                 axis=0)
        q_dec = (q * GLA_DK ** -0.5) * jnp.exp(g)
        k_inv = k * jnp.exp(-g)
        k_end = k * jnp.exp(g_last - g)
        sd = [jnp.exp(g[(b + 1) * L - 1:(b + 1) * L, :]) for b in range(nb)]
        U = [(b, h) for b in range(nb) for h in range(GLA_HEADS)]
        nu = range(len(U))
        rb = [slice(b * L, (b + 1) * L) for b in range(nb)]
        ks = [slice(h * GLA_DK, (h + 1) * GLA_DK) for h in range(GLA_HEADS)]
        vs = [slice(h * GLA_DV, (h + 1) * GLA_DV) for h in range(GLA_HEADS)]
        qd = [q_dec[rb[b], ks[h]] for b, h in U]
        vh = [v[rb[b], vs[h]] for b, h in U]
        st = [st_ref[i] for i in nu]
        sc = [jnp.where(incl, _mm_nt(qd[i], k_inv[rb[b], ks[h]]), 0.0) for i, (b, h) in enumerate(U)]
        o_state = [_mm_nt(qd[i], st[i]) for i in nu]
        upd = [_mm_tn(vh[i], k_end[rb[b], ks[h]]) for i, (b, h) in enumerate(U)]
        o_loc = [_mm(sc[i], vh[i]) for i in nu]
        for i, (b, h) in enumerate(U):
            st_ref[i] = st[i] * sd[b][:, ks[h]] + upd[i]
            o = o_loc[i] + o_state[i]
            o = o * lax.rsqrt(jnp.mean(o * o, axis=-1, keepdims=True) + NORM_EPS)
            o_ref[b, rows, vs[h]] = o * ng_ref[:, vs[h]] * _silu(gt[rb[b], vs[h]])
        return carry

    lax.fori_loop(0, n_chunks, chunk, 0)


def gla_mixer(pa, w_alpha2, b_alpha, norm_g):
    b, t, _ = pa.shape
    tt = min(MIX_TILE, t)
    nb = min(GLA_BATCH_PER_STEP, b)
    hk, hv = GLA_HEADS * GLA_DK, GLA_HEADS * GLA_DV
    wa = jnp.zeros((LANES, hk), BF16).at[:GLA_RANK].set(w_alpha2.astype(BF16))
    return pl.pallas_call(
        _gla_body,
        out_shape=jax.ShapeDtypeStruct((b, t, hv), F32),
        grid=(b // nb, t // tt),
        in_specs=[pl.BlockSpec((nb, tt, GLA_COLS_PAD), lambda i, j: (i, j, 0)),
                  _const_spec((LANES, hk)), _const_spec((1, hk)), _const_spec((1, hv))],
        out_specs=pl.BlockSpec((nb, tt, hv), lambda i, j: (i, j, 0)),
        scratch_shapes=[pltpu.VMEM((nb * GLA_HEADS, GLA_DV, GLA_DK), F32)],
        compiler_params=_params(("parallel", "arbitrary")),
        name="gla_mixer",
    )(pa, wa, b_alpha.reshape(1, hk), norm_g.reshape(1, hv))


_RWKV_SB = ("a_dec", "r_dec", "b_inv", "k_inv", "b_end", "k_end", "v")


def _seg_sum(x, seg):
    hw = seg.shape[0]
    parts = [piece[:, s * hw:(s + 1) * hw] for piece in _split3(x) for s in range(2)]
    out = jnp.dot(jnp.concatenate(parts, axis=0), seg, preferred_element_type=F32)
    m = x.shape[0]
    halves = [out[s * m:(s + 1) * m] + out[(2 + s) * m:(3 + s) * m] + out[(4 + s) * m:(5 + s) * m]
              for s in range(2)]
    return jnp.concatenate(halves, axis=1)


def _proj_ab_body(h_ref, g_ref, wa_ref, wb_ref, mu_ref, w0_ref, w2_ref, a0_ref, a2_ref, g2_ref, kk_ref, ka_ref,
                  rk_ref, seg_ref, pa_ref, sb_ref, sf_ref, gl_ref, prev_ref):
    L = RWKV_CHUNK
    W = RWKV_WIDTH
    tm = h_ref.shape[1]
    n_blocks = tm // L

    @pl.when(pl.program_id(1) == 0)
    def _():
        prev_ref[...] = jnp.zeros_like(prev_ref)

    ng = PROJ_ROW_GROUPS
    gm = tm // ng
    G = range(ng)
    rg = [slice(i * gm, (i + 1) * gm) for i in G]
    xn = [_rmsnorm(h_ref[0, rg[i], :], g_ref[...]).astype(BF16) for i in G]
    p = [jnp.dot(xn[i], wb_ref[...], preferred_element_type=F32) for i in G]

    n_a = wa_ref.shape[1]
    a_cols = [(0, 4 * LANES), (4 * LANES, 8 * LANES), (8 * LANES, n_a)]
    pending = [(i, c) for c in a_cols for i in G]

    def project_a(count):
        for _ in range(count):
            if pending:
                i, (lo_c, hi_c) = pending.pop(0)
                pa_ref[0, rg[i], lo_c:hi_c] = jnp.dot(xn[i], wa_ref[:, lo_c:hi_c], preferred_element_type=F32)

    first = lax.broadcasted_iota(jnp.int32, (SUBLANES, p[0].shape[1]), 0) == 0
    before = [prev_ref[0:1, :]] + [p[i][gm - 1:gm, :] for i in range(ng - 1)]
    rolled = [pltpu.roll(p[i], 1, 0) for i in G]
    sh = [jnp.concatenate([jnp.where(first, before[i], rolled[i][0:SUBLANES, :]), rolled[i][SUBLANES:gm, :]],
                          axis=0) for i in G]
    prev_ref[0:1, :] = p[ng - 1][gm - 1:gm, :]
    pf = [p[i] + (sh[i] - p[i]) * mu_ref[...] for i in G]
    r = [pf[i][:, 0:W] for i in G]
    k = [pf[i][:, W:2 * W] for i in G]
    v = [pf[i][:, 2 * W:3 * W] for i in G]
    wa = [pf[i][:, 3 * W:3 * W + LANES] for i in G]
    g_lr = [pf[i][:, 3 * W + LANES:3 * W + 2 * LANES] for i in G]
    log_w = [-RWKV_DECAY_SCALE * jax.nn.sigmoid(w0_ref[...] + _mm(jnp.tanh(wa[i]), w2_ref[...])) for i in G]
    a = [jax.nn.sigmoid(a0_ref[...] + _mm(wa[i], a2_ref[...])) for i in G]
    gate = [_mm(jax.nn.sigmoid(g_lr[i]), g2_ref[...]) for i in G]
    project_a(2)
    kk = [k[i] * kk_ref[...] for i in G]
    kk = [kk[i] * lax.rsqrt(_seg_sum(kk[i] * kk[i], seg_ref[...]) + RWKV_L2_EPS) for i in G]
    project_a(2)
    k = [k[i] * (1.0 + (a[i] - 1.0) * ka_ref[...]) for i in G]
    b_vec = [kk[i] * a[i] for i in G]
    bonus_sum = [_seg_sum(r[i] * k[i] * rk_ref[...], seg_ref[...]) for i in G]
    project_a(2)
    row_in_block = jnp.bitwise_and(lax.broadcasted_iota(jnp.int32, (gm, W), 0), L - 1)
    nbk = gm // L
    for i in G:
        g = log_w[i]
        d = 1
        while d < L:
            g = g + jnp.where(row_in_block >= d, pltpu.roll(g, d, 0), 0.0)
            d *= 2
        g_last = jnp.concatenate([jnp.broadcast_to(g[(j + 1) * L - 1:(j + 1) * L, :], (L, W)) for j in range(nbk)],
                                 axis=0)
        e_neg = jnp.exp(-g)
        e_end = jnp.exp(g_last - g)
        vals = dict(a_dec=-kk[i] * jnp.exp(g - log_w[i]), r_dec=r[i] * jnp.exp(g), b_inv=b_vec[i] * e_neg,
                    k_inv=k[i] * e_neg, b_end=b_vec[i] * e_end, k_end=k[i] * e_end, v=v[i])
        for j, name in enumerate(_RWKV_SB):
            sb_ref[0, rg[i], j * W:(j + 1) * W] = vals[name].astype(BF16)
        sf_ref[0, rg[i], 0:W] = gate[i]
        sf_ref[0, rg[i], W:2 * W] = bonus_sum[i] * v[i]
        for j in range(nbk):
            row0 = (i * nbk + j) * SUBLANES
            gl_ref[0, row0:row0 + SUBLANES, :] = g[(j + 1) * L - SUBLANES:(j + 1) * L, :]
        project_a(1)
    project_a(len(pending))


def proj_ab(h, g, wa, wb, mu, w0, w2, a0, a2, g2, k_k, k_a, r_k):
    b, t, d = h.shape
    tm = min(ROW_TILE, t)
    W = RWKV_WIDTH
    row = lambda x: x.reshape(1, -1)
    w2p = jnp.zeros((LANES, W), BF16).at[:RWKV_LORA_W].set(w2.astype(BF16))
    a2p = jnp.zeros((LANES, W), BF16).at[RWKV_LORA_W:].set(a2.astype(BF16))
    head_of = jnp.arange(W // 2) // RWKV_N
    seg = (head_of[:, None] == head_of[None, :]).astype(BF16)
    consts = [row(g), wa, wb, row(mu), row(w0), w2p, row(a0), a2p, g2.astype(BF16), row(k_k), row(k_a), row(r_k),
              seg]
    nsb = len(_RWKV_SB) * W
    tok = lambda w: pl.BlockSpec((1, tm, w), lambda i, j: (i, j, 0))
    return pl.pallas_call(
        _proj_ab_body,
        out_shape=[jax.ShapeDtypeStruct((b, t, wa.shape[1]), F32), jax.ShapeDtypeStruct((b, t, nsb), BF16),
                   jax.ShapeDtypeStruct((b, t, 2 * W), F32),
                   jax.ShapeDtypeStruct((b, t // RWKV_CHUNK * SUBLANES, W), F32)],
        grid=(b, t // tm),
        in_specs=[tok(d)] + [_const_spec(x.shape) for x in consts],
        out_specs=[tok(wa.shape[1]), tok(nsb), tok(2 * W),
                   pl.BlockSpec((1, tm // RWKV_CHUNK * SUBLANES, W), lambda i, j: (i, j, 0))],
        scratch_shapes=[pltpu.VMEM((SUBLANES, wb.shape[1]), F32)],
        compiler_params=_params(("parallel", "arbitrary")),
        name="proj_ab",
    )(h, *consts)


def _rwkv_rec_body(sb_ref, sf_ref, gl_ref, lng_ref, lnb_ref, seg_ref, tri_ref, quad_ref, o_ref, sp_ref):
    L = RWKV_CHUNK
    W = RWKV_WIDTH
    N = RWKV_N
    nb = sb_ref.shape[0]
    n_chunks = sb_ref.shape[1] // L
    n_pairs = RWKV_HEADS // 2
    idx = {name: j for j, name in enumerate(_RWKV_SB)}

    @pl.when(pl.program_id(1) == 0)
    def _():
        sp_ref[...] = jnp.zeros_like(sp_ref)

    def chunk(c, carry):
        lane = lax.broadcasted_iota(jnp.int32, (L, LANES), 1)
        lo = lane < N
        hi = lane >= N
        nat = (lo, hi)
        strict_f = tri_ref[0]
        incl_f = tri_ref[1]
        eye2 = tri_ref[2]
        diag_f = quad_ref[0]
        anti_f = quad_ref[1]
        zb = jnp.zeros((L, LANES), BF16)
        sel = lambda m, x: jnp.where(m, x, jnp.zeros_like(x))
        swap = lambda x: pltpu.roll(x, N, 1)
        cat0 = lambda xs: jnp.concatenate(xs, axis=0)
        cat1 = lambda xs: jnp.concatenate(xs, axis=1)
        dot = lambda x, y: jnp.dot(x, y, preferred_element_type=F32)
        nt = lambda x, y: lax.dot_general(x, y, (((1,), (1,)), ((), ())), preferred_element_type=F32)
        rows = pl.ds(pl.multiple_of(c * L, L), L)
        U = [(b, p) for b in range(nb) for p in range(n_pairs)]
        nu = range(len(U))

        def blk(name, u):
            col = idx[name] * W + u[1] * LANES
            return sb_ref[u[0], rows, col:col + LANES]

        vn = lambda i, h: sel(nat[h], blk("v", U[i]))
        ad = [blk("a_dec", u) for u in U]
        rd = [blk("r_dec", u) for u in U]
        b_inv = [blk("b_inv", u) for u in U]
        k_inv = [blk("k_inv", u) for u in U]
        pe = [nt(cat0([sel(lo, ad[i]), sel(lo, rd[i])]), cat0([b_inv[i], k_inv[i]])) for i in nu]
        po = [nt(cat0([sel(hi, ad[i]), sel(hi, rd[i])]), cat0([k_inv[i], b_inv[i]])) for i in nu]
        aa = [[pe[i][0:L, :] * strict_f, po[i][0:L, :] * strict_f] for i in nu]
        ar = [[(pe[i][L:2 * L, :] * incl_f).astype(BF16), (po[i][L:2 * L, :] * incl_f).astype(BF16)] for i in nu]
        t0 = [dot(jnp.where(lo, aa[i][1], aa[i][0]).astype(BF16),
                  cat0([cat1([vn(i, 1), zb]), cat1([zb, vn(i, 0)])])) for i in nu]
        ad_sw = [swap(x) for x in ad]
        x0 = [[jnp.where(hi, ad_sw[i], t0[i][:, LANES:2 * LANES].astype(BF16)),
               jnp.where(lo, ad_sw[i], t0[i][:, 0:LANES].astype(BF16))] for i in nu]
        q0 = [jnp.where(lo, aa[i][0], eye2) for i in nu]
        q1 = [jnp.where(lo, eye2, aa[i][1]) for i in nu]
        for _ in range(6):
            out = [dot(jnp.where(lo, q0[i], q1[i]).astype(BF16),
                       cat0([cat1([q0[i].astype(BF16), zb]), cat1([zb, q1[i].astype(BF16)])])) for i in nu]
            q0 = [out[i][:, 0:LANES] + sel(hi, q0[i]) for i in nu]
            q1 = [out[i][:, LANES:2 * LANES] + sel(lo, q1[i]) for i in nu]
        tx = [dot(jnp.where(lo, q1[i], q0[i]).astype(BF16),
                  cat0([cat1([x0[i][1], zb]), cat1([zb, x0[i][0]])])) for i in nu]
        xb = [[tx[i][:, LANES:2 * LANES].astype(BF16), tx[i][:, 0:LANES].astype(BF16)] for i in nu]
        rd_sw = [swap(blk("r_dec", u)).astype(F32) for u in U]
        rmy = [[sel(hi, rd_sw[i]) + dot(ar[i][0], cat0([xb[i][0], vn(i, 0)])),
                sel(lo, rd_sw[i]) + dot(ar[i][1], cat0([vn(i, 1), xb[i][1]]))] for i in nu]
        be_sw = [swap(blk("b_end", u)) for u in U]
        ke_sw = [swap(blk("k_end", u)) for u in U]
        gg = [lax.dot_general(cat0([xb[i][0], xb[i][1], vn(i, 0), vn(i, 1)]),
                              cat0([sel(hi, be_sw[i]), sel(lo, be_sw[i]), sel(hi, ke_sw[i]), sel(lo, ke_sw[i])]),
                              (((0,), (0,)), ((), ())), preferred_element_type=F32) for i in nu]
        sp = [sp_ref[i] for i in nu]
        ys = [_mm_nt(jnp.where(lo, rmy[i][1], rmy[i][0]), sp[i]) + jnp.where(lo, rmy[i][0], rmy[i][1]) for i in nu]
        g_rows = pl.ds(pl.multiple_of(c * SUBLANES, SUBLANES), SUBLANES)
        for i, u in enumerate(U):
            sd_sw = jnp.exp(swap(gl_ref[u[0], g_rows, u[1] * LANES:(u[1] + 1) * LANES])[SUBLANES - 1:SUBLANES, :])
            sp_ref[i] = sp[i] * sd_sw + _mm(sp[i], gg[i] * diag_f) + gg[i] * anti_f
        y = cat0([cat1(ys[b * n_pairs:(b + 1) * n_pairs]) for b in range(nb)])
        mean = _seg_sum(y, seg_ref[...]) * (1.0 / N)
        dy = y - mean
        var = _seg_sum(dy * dy, seg_ref[...]) * (1.0 / N)
        y = dy * lax.rsqrt(var + RWKV_GN_EPS) * lng_ref[...] + lnb_ref[...]
        for b in range(nb):
            yb = y[b * L:(b + 1) * L, :]
            o_ref[b, rows, :] = (yb + sf_ref[b, rows, W:2 * W]) * sf_ref[b, rows, 0:W]
        return carry

    lax.fori_loop(0, n_chunks, chunk, 0)


def rwkv_recurrence(sb, sf, gl, ln_g, ln_b):
    b, t, _ = sb.shape
    tt = min(RWKV_TILE, t)
    nb = min(RWKV_BATCH_PER_STEP, b)
    W = RWKV_WIDTH
    row = lambda x: x.reshape(1, -1)
    head_of = jnp.arange(W // 2) // RWKV_N
    seg = (head_of[:, None] == head_of[None, :]).astype(BF16)
    t_idx = jnp.arange(RWKV_CHUNK)[:, None]
    s_idx = jnp.arange(LANES)[None, :] % RWKV_N
    tri = jnp.stack([t_idx > s_idx, t_idx >= s_idx, t_idx == s_idx]).astype(F32)
    half = jnp.arange(LANES) // RWKV_N
    same_half = half[:, None] == half[None, :]
    quad = jnp.stack([same_half, ~same_half]).astype(F32)
    consts = [row(ln_g), row(ln_b), seg, tri, quad]
    tok = lambda w: pl.BlockSpec((nb, tt, w), lambda i, j: (i, j, 0))
    return pl.pallas_call(
        _rwkv_rec_body,
        out_shape=jax.ShapeDtypeStruct((b, t, W), F32),
        grid=(b // nb, t // tt),
        in_specs=[tok(sb.shape[2]), tok(sf.shape[2]),
                  pl.BlockSpec((nb, tt // RWKV_CHUNK * SUBLANES, W), lambda i, j: (i, j, 0))]
        + [_const_spec(x.shape) for x in consts],
        out_specs=tok(W),
        scratch_shapes=[pltpu.VMEM((nb * RWKV_HEADS // 2, LANES, LANES), F32)],
        compiler_params=_params(("parallel", "arbitrary")),
        name="rwkv_mixer",
    )(sb, sf, gl, *consts)


def _proj_cd_body(h_ref, g_ref, wc_ref, wd_ref, lcw_ref, lcb_ref, gw_ref, gb_ref, lam_ref, mcw_ref, mcb_ref,
                  wq_ref, wk_ref, wv_ref, bif_ref, lr_ref, md_ref, lwin_ref, mwin_ref):
    tm = h_ref.shape[1]
    W = LRU_WIDTH

    @pl.when(pl.program_id(1) == 0)
    def _():
        lwin_ref[0:SUBLANES, :] = jnp.zeros((SUBLANES, W), F32)
        mwin_ref[0:SUBLANES, :] = jnp.zeros((SUBLANES, W), F32)

    ng = PROJ_ROW_GROUPS
    gm = tm // ng
    G = range(ng)
    rg = [slice(i * gm, (i + 1) * gm) for i in G]
    xn = [_rmsnorm(h_ref[0, rg[i], :], g_ref[...]).astype(BF16) for i in G]
    pc = [jnp.dot(xn[i], wc_ref[...], preferred_element_type=F32) for i in G]
    pd = [jnp.dot(xn[i], wd_ref[...], preferred_element_type=F32) for i in G]
    half = W // 2
    for i in G:
        x = pc[i][:, 0:W]
        xc = _causal_conv(x, lwin_ref, lcw_ref[...], lcb_ref[...], LRU_CONV)
        xcb = xc.astype(BF16)
        pre = []
        for gi in range(2):
            pre.append(jnp.concatenate(
                [jnp.dot(xcb[:, n * LRU_BLOCK:(n + 1) * LRU_BLOCK], gw_ref[gi, n], preferred_element_type=F32)
                 for n in range(LRU_BLOCKS)], axis=1) + gb_ref[gi:gi + 1, :])
        r_gate = jax.nn.sigmoid(pre[0])
        i_gate = jax.nn.sigmoid(pre[1])
        log_a = -LRU_C * r_gate * _softplus(-lam_ref[...])
        a = jnp.exp(log_a)
        w2 = -jnp.tanh(log_a) * (a * a + 1.0)
        lr_ref[0, rg[i], 0:W] = a
        lr_ref[0, rg[i], W:2 * W] = jnp.where(w2 > 0.0, w2 * lax.rsqrt(w2), 0.0) * (i_gate * xc)
        lr_ref[0, rg[i], 2 * W:3 * W] = jax.nn.gelu(pc[i][:, W:2 * W])
        mx = pd[i][:, 0:W]
        mxc = _silu(_causal_conv(mx, mwin_ref, mcw_ref[...], mcb_ref[...], MLSTM_CONV)).astype(BF16)
        mxb = mx.astype(BF16)
        for s in range(2):
            cs = slice(s * half, (s + 1) * half)
            md_ref[0, rg[i], s * half:(s + 1) * half] = jnp.dot(mxc[:, cs], wq_ref[s], preferred_element_type=F32)
            md_ref[0, rg[i], W + s * half:W + (s + 1) * half] = (
                jnp.dot(mxc[:, cs], wk_ref[s], preferred_element_type=F32) * MLSTM_DH ** -0.5)
            md_ref[0, rg[i], 2 * W + s * half:2 * W + (s + 1) * half] = jnp.dot(
                mxb[:, cs], wv_ref[s], preferred_element_type=F32)
        md_ref[0, rg[i], 3 * W:4 * W] = jax.nn.sigmoid(pd[i][:, W:2 * W])
        md_ref[0, rg[i], 4 * W:4 * W + LANES] = pd[i][:, 2 * W:2 * W + LANES] + bif_ref[...]


def proj_cd(h, g, wc, wd, lru_conv_w, lru_conv_b, gate_w, gate_b, lam, m_conv_w, m_conv_b, qkv_w, b_if):
    b, t, d = h.shape
    tm = min(ROW_TILE, t)
    W = LRU_WIDTH
    row = lambda x: x.reshape(1, -1)
    pad_taps = lambda w: jnp.zeros((SUBLANES, W), F32).at[:w.shape[0]].set(w)
    nblk = qkv_w.shape[1] // 2
    diag_tiles = lambda w: jnp.stack([_block_diag(w[:nblk]), _block_diag(w[nblk:])]).astype(BF16)
    bif = jnp.zeros((1, LANES), F32).at[0, :2 * MLSTM_HEADS].set(b_if)
    consts = [row(g), wc, wd, pad_taps(lru_conv_w), row(lru_conv_b), gate_w.astype(BF16), gate_b, row(lam),
              pad_taps(m_conv_w), row(m_conv_b)] + [diag_tiles(qkv_w[i]) for i in range(3)] + [bif]
    gm = tm // PROJ_ROW_GROUPS
    tok = lambda w: pl.BlockSpec((1, tm, w), lambda i, j: (i, j, 0))
    return pl.pallas_call(
        _proj_cd_body,
        out_shape=[jax.ShapeDtypeStruct((b, t, 3 * W), F32), jax.ShapeDtypeStruct((b, t, MLSTM_IN_COLS), F32)],
        grid=(b, t // tm),
        in_specs=[tok(d)] + [_const_spec(x.shape) for x in consts],
        out_specs=[tok(3 * W), tok(MLSTM_IN_COLS)],
        scratch_shapes=[pltpu.VMEM((SUBLANES + gm, W), F32), pltpu.VMEM((SUBLANES + gm, W), F32)],
        compiler_params=_params(("parallel", "arbitrary")),
        name="proj_cd",
    )(h, *consts)


def _lru_scan_body(p_ref, o_ref, hprev_ref):
    tt = p_ref.shape[1]
    W = LRU_WIDTH

    @pl.when(pl.program_id(1) == 0)
    def _():
        hprev_ref[...] = jnp.zeros_like(hprev_ref)

    a = p_ref[0, :, 0:W]
    u = p_ref[0, :, W:2 * W]
    row_in_group = jnp.bitwise_and(lax.broadcasted_iota(jnp.int32, (tt, W), 0), SUBLANES - 1)
    d = 1
    while d < SUBLANES:
        keep = row_in_group >= d
        u = u + a * jnp.where(keep, pltpu.roll(u, d, 0), 0.0)
        a = a * jnp.where(keep, pltpu.roll(a, d, 0), 1.0)
        d *= 2
    carry = hprev_ref[0:1, :]
    for g in range(tt // SUBLANES):
        rs = slice(g * SUBLANES, (g + 1) * SUBLANES)
        hg = u[rs, :] + a[rs, :] * carry
        carry = hg[SUBLANES - 1:SUBLANES, :]
        o_ref[0, rs, :] = hg * p_ref[0, rs, 2 * W:3 * W]
    hprev_ref[0:1, :] = carry


def lru_scan(lr):
    b, t, cols = lr.shape
    tt = min(ROW_TILE, t)
    W = LRU_WIDTH
    return pl.pallas_call(
        _lru_scan_body,
        out_shape=jax.ShapeDtypeStruct((b, t, W), F32),
        grid=(b, t // tt),
        in_specs=[pl.BlockSpec((1, tt, cols), lambda i, j: (i, j, 0))],
        out_specs=pl.BlockSpec((1, tt, W), lambda i, j: (i, j, 0)),
        scratch_shapes=[pltpu.VMEM((SUBLANES, W), F32)],
        compiler_params=_params(("parallel", "arbitrary")),
        name="rglru_mixer",
    )(lr)


def _mlstm_body(p_ref, ng_ref, o_ref, c_ref, m_ref):
    tt = p_ref.shape[1]
    L = MLSTM_CHUNK
    W = MLSTM_WIDTH
    DH = MLSTM_DH
    H = MLSTM_HEADS
    n_chunks = tt // L
    nb = p_ref.shape[0]

    @pl.when(pl.program_id(1) == 0)
    def _():
        c_ref[...] = jnp.zeros_like(c_ref)
        m_ref[...] = jnp.zeros_like(m_ref)

    def chunk(c, carry):
        incl, _ = _tri_masks(L)
        tri01 = jnp.where(incl, 1.0, 0.0).astype(BF16)
        t0 = pl.multiple_of(c * L, L)
        rows = pl.ds(t0, L)
        nt = lambda y: lax.dot_general(y, tri01, (((1,), (1,)), ((), ())), preferred_element_type=F32)
        gates, lsg, g_t, b_rows = [], [], [], []
        for b in range(nb):
            gates.append(p_ref[b, rows, 4 * W:4 * W + LANES])
            lsg.append(_log_sigmoid(gates[b]))
            g_t.append(gates[b].T[0:SUBLANES, :])
            x1, x2, x3 = _split3(_log_sigmoid(g_t[b]))
            b_rows.append(nt(x1) + nt(x2) + nt(x3))
        ones_b = jnp.ones((L, LANES), BF16)
        U = [(b, h) for b in range(nb) for h in range(H)]
        nu = range(len(U))
        hs = [slice(h * DH, (h + 1) * DH) for h in range(H)]
        col = lambda j, h: slice(j * W + h * DH, j * W + (h + 1) * DH)
        qh = [p_ref[b, rows, col(0, h)] for b, h in U]
        kh = [p_ref[b, rows, col(1, h)] for b, h in U]
        v_aug = [jnp.concatenate([p_ref[b, rows, col(2, h)].astype(BF16), ones_b], axis=1) for b, h in U]
        cm = [c_ref[i] for i in nu]
        m = [m_ref[i, 0:1, :] for i in nu]
        i_rep = [jnp.broadcast_to(gates[b][:, h:h + 1], (L, LANES)) for b, h in U]
        lf_rep = [jnp.broadcast_to(lsg[b][:, H + h:H + h + 1], (L, LANES)) for b, h in U]
        b_rep = [_mm_exact_lhs01(tri01, lf_rep[i]) for i in nu]
        qk = [_mm_nt(qh[i], kh[i]) for i in nu]
        b_last = [b_rep[i][L - 1:L, :] for i in nu]
        log_e = [b_last[i] - b_rep[i] + i_rep[i] for i in nu]
        m_end = [jnp.max(log_e[i], axis=0, keepdims=True) for i in nu]
        m_new = [jnp.maximum(b_last[i] + m[i], m_end[i]) for i in nu]
        kin = [kh[i] * (jnp.exp(m_end[i] - m_new[i]) * jnp.exp(log_e[i] - m_end[i])) for i in nu]
        c_upd = [_mm_tn(kin[i], v_aug[i]) for i in nu]
        run = [i_rep[i] - b_rep[i] for i in nu]
        d = 1
        while d < L:
            run = [jnp.maximum(run[i], _shift_rows(run[i], d, -jnp.inf)) for i in nu]
            d *= 2
        m_t = [jnp.maximum(b_rep[i] + m[i], b_rep[i] + run[i]) for i in nu]
        log_d = [jnp.where(incl, b_rep[i][:, 0:L] - b_rows[b][H + h:H + h + 1, :] + g_t[b][h:h + 1, :], -jnp.inf)
                 for i, (b, h) in enumerate(U)]
        w_loc = [jnp.exp(log_d[i] - m_t[i][:, 0:L]) * qk[i] for i in nu]
        lhs = [jnp.concatenate([(jnp.exp(b_rep[i] + m[i] - m_t[i]) * qh[i]).astype(BF16), w_loc[i].astype(BF16)],
                               axis=1) for i in nu]
        nd = [jnp.dot(lhs[i], jnp.concatenate([cm[i].astype(BF16), v_aug[i]], axis=0),
                      preferred_element_type=F32) for i in nu]
        for i, (b, h) in enumerate(U):
            c_state = jnp.exp(b_last[i] + m[i] - m_new[i])
            c_ref[i] = jnp.concatenate([c_state, c_state], axis=1) * cm[i] + c_upd[i]
            m_ref[i] = jnp.broadcast_to(m_new[i], (SUBLANES, LANES))
            hv = nd[i][:, 0:DH] / jnp.maximum(jnp.abs(nd[i][:, DH:2 * DH]), jnp.exp(-m_t[i]))
            hv = hv * lax.rsqrt(jnp.mean(hv * hv, axis=-1, keepdims=True) + NORM_EPS)
            o_ref[b, rows, hs[h]] = p_ref[b, rows, col(3, h)] * hv * ng_ref[:, hs[h]]
        return carry

    lax.fori_loop(0, n_chunks, chunk, 0)


def _block_diag(w):
    nb, d, e = w.shape
    rows = jnp.tile(w.reshape(nb * d, e), (1, nb))
    same_block = (jnp.arange(nb * d)[:, None] // d) == (jnp.arange(nb * e)[None, :] // e)
    return jnp.where(same_block, rows, 0.0)


def mlstm_mixer(md, norm_g):
    b, t, cols = md.shape
    tt = min(MIX_TILE, t)
    nb = min(MLSTM_BATCH_PER_STEP, b)
    W = MLSTM_WIDTH
    return pl.pallas_call(
        _mlstm_body,
        out_shape=jax.ShapeDtypeStruct((b, t, W), F32),
        grid=(b // nb, t // tt),
        in_specs=[pl.BlockSpec((nb, tt, cols), lambda i, j: (i, j, 0)), _const_spec((1, W))],
        out_specs=pl.BlockSpec((nb, tt, W), lambda i, j: (i, j, 0)),
        scratch_shapes=[pltpu.VMEM((nb * MLSTM_HEADS, MLSTM_DH, 2 * MLSTM_DH), F32),
                        pltpu.VMEM((nb * MLSTM_HEADS, SUBLANES, LANES), F32)],
        compiler_params=_params(("parallel", "arbitrary")),
        name="mlstm_mixer",
    )(md, norm_g.reshape(1, W))


def _mix_xattn_body(h_ref, ya_ref, yb_ref, wm_ref, g_ref, wq_ref, k_ref, v_ref, wo_ref, o_ref):
    tm = h_ref.shape[1]
    ng = XATTN_ROW_GROUPS
    rg = [slice(i * (tm // ng), (i + 1) * (tm // ng)) for i in range(ng)]
    G = range(ng)
    HD = range(XA_HEADS)
    hs = [slice(hd * XA_DH, (hd + 1) * XA_DH) for hd in HD]
    mixed = [jnp.concatenate([ya_ref[0, rg[i], :], yb_ref[0, rg[i], :]], axis=1).astype(BF16) for i in G]
    h = [h_ref[0, rg[i], :] + jnp.dot(mixed[i], wm_ref[...], preferred_element_type=F32) for i in G]
    xn = [_rmsnorm(h[i], g_ref[...]).astype(BF16) for i in G]
    q = [jnp.dot(xn[i], wq_ref[...], preferred_element_type=F32) for i in G]
    s = [[_mm_nt(q[i][:, hs[hd]], k_ref[0, :, hs[hd]]) * XA_DH ** -0.5 for hd in HD] for i in G]
    e = [[jnp.exp(s[i][hd] - jnp.max(s[i][hd], axis=-1, keepdims=True)) for hd in HD] for i in G]
    p = [[e[i][hd] / jnp.sum(e[i][hd], axis=-1, keepdims=True) for hd in HD] for i in G]
    pv = [[_mm(p[i][hd], v_ref[0, :, hs[hd]]) for hd in HD] for i in G]
    for i in G:
        o = jnp.concatenate(pv[i], axis=1).astype(BF16)
        o_ref[0, rg[i], :] = h[i] + jnp.dot(o, wo_ref[...], preferred_element_type=F32)


def mix_xattn(h, ya, yb, w_mix, g, wq, kv, wo):
    b, t, d = h.shape
    tm = min(XATTN_TILE, t)
    tok = lambda w: pl.BlockSpec((1, tm, w), lambda i, j: (i, j, 0))
    mem_k = pl.BlockSpec((1, MEM_LEN, d), lambda i, j: (i, 0, 0))
    mem_v = pl.BlockSpec((1, MEM_LEN, d), lambda i, j: (i, 0, 1))
    return pl.pallas_call(
        _mix_xattn_body,
        out_shape=jax.ShapeDtypeStruct((b, t, d), F32),
        grid=(b, t // tm),
        in_specs=[tok(d), tok(ya.shape[-1]), tok(yb.shape[-1]), _const_spec(w_mix.shape), _const_spec((1, d)),
                  _const_spec(wq.shape), mem_k, mem_v, _const_spec(wo.shape)],
        out_specs=tok(d),
        compiler_params=_params(("parallel", "parallel")),
        name="mix_xattn",
    )(h, ya, yb, w_mix, g.reshape(1, d), wq, kv, kv, wo)


def _ffn_body(final_norm, h_ref, g_ref, wu_ref, wg_ref, cw_ref, cb_ref, wd_ref, fg_ref, o_ref, uprev_ref):
    tm = h_ref.shape[1]

    @pl.when(pl.program_id(1) == 0)
    def _():
        uprev_ref[...] = jnp.zeros_like(uprev_ref)

    h = h_ref[0]
    xn = _rmsnorm(h, g_ref[...]).astype(BF16)
    acc = h
    start = 0
    for width in FFN_COL_GROUPS:
        cs = slice(start, start + width)
        start += width
        u = jnp.dot(xn, wu_ref[:, cs], preferred_element_type=F32)
        gt = jnp.dot(xn, wg_ref[:, cs], preferred_element_type=F32)
        uu = jnp.concatenate([uprev_ref[:, cs], u], axis=0)
        uprev_ref[:, cs] = u[tm - SUBLANES:tm, :]
        c = cb_ref[:, cs] + cw_ref[2:3, cs] * u
        c = c + cw_ref[1:2, cs] * pltpu.roll(uu, 1, 0)[SUBLANES:SUBLANES + tm, :]
        c = c + cw_ref[0:1, cs] * pltpu.roll(uu, 2, 0)[SUBLANES:SUBLANES + tm, :]
        act = (_silu(c) * gt).astype(BF16)
        acc = acc + jnp.dot(act, wd_ref[cs, :], preferred_element_type=F32)
    if final_norm:
        acc = _rmsnorm(acc, fg_ref[...])
    o_ref[0] = acc


def ffn(h, g, wu, wg, conv_w, conv_b, wd, final_g, final_norm):
    b, t, d = h.shape
    tm = min(ROW_TILE, t)
    tok = pl.BlockSpec((1, tm, d), lambda i, j: (i, j, 0))
    consts = [g.reshape(1, d), wu, wg, conv_w, conv_b, wd, final_g.reshape(1, d)]
    return pl.pallas_call(
        functools.partial(_ffn_body, final_norm),
        out_shape=jax.ShapeDtypeStruct((b, t, d), F32),
        grid=(b, t // tm),
        in_specs=[tok] + [_const_spec(x.shape) for x in consts],
        out_specs=tok,
        scratch_shapes=[pltpu.VMEM((SUBLANES, D_FF_PAD), F32)],
        compiler_params=_params(("parallel", "arbitrary")),
        name="ffn",
    )(h, *consts)


def _pad_cols(w, n):
    return jnp.pad(w, ((0, 0), (0, n - w.shape[1])))


def _rwkv_perm(x):
    W = RWKV_WIDTH
    o_w, o_k, o_v = W, W + RWKV_LORA_W, 2 * W + RWKV_LORA_W
    o_a = 3 * W + RWKV_LORA_W
    o_g = o_a + RWKV_LORA_A
    return jnp.concatenate([x[..., :W], x[..., o_k:o_k + W], x[..., o_v:o_v + W], x[..., o_w:o_w + RWKV_LORA_W],
                            x[..., o_a:o_a + RWKV_LORA_A], x[..., o_g:]], axis=-1)


def kernel(x, mem, mem_norm_g, norm_mix_g, ab_w_in, gla_w_alpha2, gla_b_alpha, gla_norm_g, rwkv_mu, rwkv_w0, rwkv_w2, rwkv_a0, rwkv_a2, rwkv_g2, rwkv_k_k, rwkv_k_a, rwkv_r_k, rwkv_ln_g, rwkv_ln_b, cd_w_in, lru_conv_w, lru_conv_b, lru_gate_w, lru_gate_b, lru_lambda, mlstm_conv_w, mlstm_conv_b, mlstm_qkv_w, mlstm_b_if, mlstm_norm_g, w_mix_out, norm_xattn_g, xattn_wq, xattn_wkv, xattn_wo, norm_ffn_g, ffn_w_up, ffn_conv_w, ffn_conv_b, ffn_w_down, final_norm_g):
    b, t, d = x.shape
    depth = norm_mix_g.shape[0]
    n = b * t
    h = x
    mem2d = mem.reshape(b * MEM_LEN, d)
    for layer in range(depth):
        j = layer // 2
        h2d = h.reshape(n, d)
        if layer % 2 == 0:
            w = ab_w_in[j]
            wa = _pad_cols(w[:, :GLA_COLS], GLA_COLS_PAD).astype(BF16)
            wb = _rwkv_perm(w[:, GLA_COLS:]).astype(BF16)
            pa, sb, sf, gl = proj_ab(h, norm_mix_g[layer], wa, wb, _rwkv_perm(rwkv_mu[j]), rwkv_w0[j], rwkv_w2[j],
                                     rwkv_a0[j], rwkv_a2[j], rwkv_g2[j], rwkv_k_k[j], rwkv_k_a[j], rwkv_r_k[j])
            ya = gla_mixer(pa, gla_w_alpha2[j], gla_b_alpha[j], gla_norm_g[j])
            yb = rwkv_recurrence(sb, sf, gl, rwkv_ln_g[j], rwkv_ln_b[j])
        else:
            w = cd_w_in[j]
            wc = w[:, :2 * LRU_WIDTH].astype(BF16)
            wd_ = _pad_cols(w[:, 2 * LRU_WIDTH:], MLSTM_COLS_PAD).astype(BF16)
            lr, md = proj_cd(h, norm_mix_g[layer], wc, wd_, lru_conv_w[j], lru_conv_b[j], lru_gate_w[j],
                             lru_gate_b[j], lru_lambda[j], mlstm_conv_w[j], mlstm_conv_b[j], mlstm_qkv_w[j],
                             mlstm_b_if[j])
            ya = lru_scan(lr)
            yb = mlstm_mixer(md, mlstm_norm_g[j])
        (kv,) = norm_matmul(mem2d, mem_norm_g, [xattn_wkv[layer].astype(BF16)], out_dtype=BF16, name="proj_kv")
        kv = kv.reshape(b, MEM_LEN, 2 * d)
        h = mix_xattn(h, ya, yb, w_mix_out[layer].astype(BF16), norm_xattn_g[layer], xattn_wq[layer].astype(BF16),
                      kv, xattn_wo[layer].astype(BF16))
        wup = ffn_w_up[layer]
        wu = _pad_cols(wup[:, :D_FF], D_FF_PAD).astype(BF16)
        wg = _pad_cols(wup[:, D_FF:], D_FF_PAD).astype(BF16)
        cw = _pad_cols(jnp.pad(ffn_conv_w[layer], ((0, SUBLANES - FFN_CONV), (0, 0))), D_FF_PAD)
        cb = _pad_cols(ffn_conv_b[layer].reshape(1, D_FF), D_FF_PAD)
        wdn = jnp.pad(ffn_w_down[layer], ((0, D_FF_PAD - D_FF), (0, 0))).astype(BF16)
        h = ffn(h, norm_ffn_g[layer], wu, wg, cw, cb, wdn, final_norm_g, layer == depth - 1)
    return h
```

```python
import functools

import jax
import jax.numpy as jnp
from jax import lax
from jax.experimental import pallas as pl
from jax.experimental.pallas import tpu as pltpu

F32 = jnp.float32
BF16 = jnp.bfloat16

D_MODEL = 1024
NORM_EPS = 1e-6
LANES = 128
SUBLANES = 8
VMEM_LIMIT_BYTES = 56 * 1024 * 1024

GLA_HEADS, GLA_DK, GLA_DV, GLA_RANK, GLA_TAU, GLA_CHUNK = 4, 64, 128, 16, 16.0, 64
GLA_COLS = 2 * GLA_HEADS * GLA_DK + 2 * GLA_HEADS * GLA_DV + GLA_RANK
GLA_COLS_PAD = 13 * LANES

RWKV_HEADS, RWKV_N, RWKV_WIDTH = 8, 64, 512
RWKV_LORA_W, RWKV_LORA_A, RWKV_LORA_G = 64, 64, 128
RWKV_CHUNK = 64
RWKV_DECAY_SCALE = 0.6065306597126334
RWKV_GN_EPS = RWKV_N * 1e-5
RWKV_L2_EPS = 1e-12
RWKV_COLS = 3 * RWKV_WIDTH + RWKV_LORA_W + RWKV_LORA_A + RWKV_LORA_G

LRU_WIDTH, LRU_BLOCKS, LRU_BLOCK, LRU_C, LRU_CONV = 512, 4, 128, 8.0, 4
MLSTM_HEADS, MLSTM_DH, MLSTM_WIDTH, MLSTM_CONV, MLSTM_CHUNK = 4, 128, 512, 4, 64
MLSTM_COLS_PAD = 2 * MLSTM_WIDTH + LANES
MLSTM_IN_COLS = 4 * MLSTM_WIDTH + LANES

XA_HEADS, XA_DH, MEM_LEN = 4, 256, 256
D_FF, FFN_CONV = 2752, 3
D_FF_PAD = 22 * LANES
MXU_K_TILE = 256
FFN_COL_GROUPS = (6 * MXU_K_TILE, 5 * MXU_K_TILE)

MIX_TILE = 256
RWKV_TILE = 256
GLA_BATCH_PER_STEP = 4
MLSTM_BATCH_PER_STEP = 2
RWKV_BATCH_PER_STEP = 4
ROW_TILE = 512
XATTN_TILE = 1024
XATTN_ROW_GROUPS = 2
PROJ_ROW_GROUPS = 2


def _mm(a, b):
    return jnp.dot(a.astype(BF16), b.astype(BF16), preferred_element_type=F32)


def _mm_nt(a, b):
    return lax.dot_general(a.astype(BF16), b.astype(BF16), (((1,), (1,)), ((), ())), preferred_element_type=F32)


def _mm_tn(a, b):
    return lax.dot_general(a.astype(BF16), b.astype(BF16), (((0,), (0,)), ((), ())), preferred_element_type=F32)


def _split3(x):
    x1 = x.astype(BF16)
    r1 = x - x1.astype(F32)
    x2 = r1.astype(BF16)
    x3 = (r1 - x2.astype(F32)).astype(BF16)
    return x1, x2, x3


def _mm_exact_lhs01(m01, x):
    x1, x2, x3 = _split3(x)
    d = lambda y: jnp.dot(m01, y, preferred_element_type=F32)
    return d(x1) + d(x2) + d(x3)


def _rmsnorm(x, g):
    return x * lax.rsqrt(jnp.mean(x * x, axis=-1, keepdims=True) + NORM_EPS) * g


def _log_sigmoid(x):
    return jnp.minimum(x, 0.0) - jnp.log1p(jnp.exp(-jnp.abs(x)))


def _softplus(x):
    return jnp.maximum(x, 0.0) + jnp.log1p(jnp.exp(-jnp.abs(x)))


def _silu(x):
    return x * jax.nn.sigmoid(x)


def _tri_masks(n):
    r = lax.broadcasted_iota(jnp.int32, (n, n), 0)
    c = lax.broadcasted_iota(jnp.int32, (n, n), 1)
    return r >= c, r > c


def _shift_rows(x, s, fill):
    if s == 0:
        return x
    rolled = pltpu.roll(x, s, 0)
    row = lax.broadcasted_iota(jnp.int32, x.shape, 0)
    return jnp.where(row >= s, rolled, fill)


def _causal_conv(x, buf_ref, w, b, width):
    t = x.shape[0]
    buf_ref[SUBLANES:SUBLANES + t, :] = x
    y = b + w[width - 1:width, :] * x
    for j in range(width - 1):
        s = width - 1 - j
        y = y + w[j:j + 1, :] * buf_ref[SUBLANES - s:SUBLANES - s + t, :]
    buf_ref[0:SUBLANES, :] = x[t - SUBLANES:t, :]
    return y


def _const_spec(shape):
    nd = len(shape)
    return pl.BlockSpec(shape, lambda *_: (0,) * nd, pipeline_mode=pl.Buffered(1))


def _params(sem):
    return pltpu.CompilerParams(dimension_semantics=sem, vmem_limit_bytes=VMEM_LIMIT_BYTES)


def _norm_matmul_body(n_out, x_ref, g_ref, *refs):
    xn = _rmsnorm(x_ref[...], g_ref[...]).astype(BF16)
    for w_ref, o_ref in zip(refs[:n_out], refs[n_out:]):
        o_ref[...] = jnp.dot(xn, w_ref[...], preferred_element_type=F32).astype(o_ref.dtype)


def norm_matmul(x2d, g, ws, out_dtype=F32, name="norm_matmul"):
    n, d = x2d.shape
    tm = min(ROW_TILE, n)
    assert n % tm == 0
    return pl.pallas_call(
        functools.partial(_norm_matmul_body, len(ws)),
        out_shape=[jax.ShapeDtypeStruct((n, w.shape[1]), out_dtype) for w in ws],
        grid=(n // tm,),
        in_specs=[pl.BlockSpec((tm, d), lambda i: (i, 0)), _const_spec((1, d))]
        + [_const_spec(w.shape) for w in ws],
        out_specs=[pl.BlockSpec((tm, w.shape[1]), lambda i: (i, 0)) for w in ws],
        compiler_params=_params(("parallel",)),
        name=name,
    )(x2d, g.reshape(1, d), *ws)


def _gla_body(p_ref, wa_ref, ba_ref, ng_ref, o_ref, st_ref):
    L = GLA_CHUNK
    nb = p_ref.shape[0]
    n_chunks = p_ref.shape[1] // L
    M = nb * L
    hk = GLA_HEADS * GLA_DK
    hv = GLA_HEADS * GLA_DV

    @pl.when(pl.program_id(1) == 0)
    def _():
        st_ref[...] = jnp.zeros_like(st_ref)

    def chunk(c, carry):
        incl, _ = _tri_masks(L)
        ri = lax.broadcasted_iota(jnp.int32, (M, M), 0)
        ci = lax.broadcasted_iota(jnp.int32, (M, M), 1)
        same_seq = jnp.bitwise_and(ri, -L) == jnp.bitwise_and(ci, -L)
        tri01 = jnp.where(same_seq & (ri >= ci), 1.0, 0.0).astype(BF16)
        t0 = pl.multiple_of(c * L, L)
        rows = pl.ds(t0, L)
        cols = lambda lo, hi: jnp.concatenate([p_ref[b, rows, lo:hi] for b in range(nb)], axis=0)
        q = cols(0, hk)
        k = cols(hk, 2 * hk)
        v = cols(2 * hk, 2 * hk + hv)
        gt = cols(2 * hk + hv, 2 * hk + 2 * hv)
        a_lr = cols(2 * hk + 2 * hv, GLA_COLS_PAD)
        la = _log_sigmoid(_mm(a_lr, wa_ref[...]) + ba_ref[...]) * (1.0 / GLA_TAU)
        g = _mm_exact_lhs01(tri01, la)
        g_last = jnp.concatenate([jnp.broadcast_to(g[(b + 1) * L - 1:(b + 1) * L, :], (L, hk)) for b in range(nb)],
                                 axis=0)
        q_dec = (q * GLA_DK ** -0.5) * jnp.exp(g)
        k_inv = k * jnp.exp(-g)
        k_end = k * jnp.exp(g_last - g)
        sd = [jnp.exp(g[(b + 1) * L - 1:(b + 1) * L, :]) for b in range(nb)]
        U = [(b, h) for b in range(nb) for h in range(GLA_HEADS)]
        nu = range(len(U))
        rb = [slice(b * L, (b + 1) * L) for b in range(nb)]
        ks = [slice(h * GLA_DK, (h + 1) * GLA_DK) for h in range(GLA_HEADS)]
        vs = [slice(h * GLA_DV, (h + 1) * GLA_DV) for h in range(GLA_HEADS)]
        qd = [q_dec[rb[b], ks[h]] for b, h in U]
        vh = [v[rb[b], vs[h]] for b, h in U]
        st = [st_ref[i] for i in nu]
        sc = [jnp.where(incl, _mm_nt(qd[i], k_inv[rb[b], ks[h]]), 0.0) for i, (b, h) in enumerate(U)]
        o_state = [_mm_nt(qd[i], st[i]) for i in nu]
        upd = [_mm_tn(vh[i], k_end[rb[b], ks[h]]) for i, (b, h) in enumerate(U)]
        o_loc = [_mm(sc[i], vh[i]) for i in nu]
        for i, (b, h) in enumerate(U):
            st_ref[i] = st[i] * sd[b][:, ks[h]] + upd[i]
            o = o_loc[i] + o_state[i]
            o = o * lax.rsqrt(jnp.mean(o * o, axis=-1, keepdims=True) + NORM_EPS)
            o_ref[b, rows, vs[h]] = o * ng_ref[:, vs[h]] * _silu(gt[rb[b], vs[h]])
        return carry

    lax.fori_loop(0, n_chunks, chunk, 0)


def gla_mixer(pa, w_alpha2, b_alpha, norm_g):
    b, t, _ = pa.shape
    tt = min(MIX_TILE, t)
    nb = min(GLA_BATCH_PER_STEP, b)
    hk, hv = GLA_HEADS * GLA_DK, GLA_HEADS * GLA_DV
    wa = jnp.zeros((LANES, hk), BF16).at[:GLA_RANK].set(w_alpha2.astype(BF16))
    return pl.pallas_call(
        _gla_body,
        out_shape=jax.ShapeDtypeStruct((b, t, hv), F32),
        grid=(b // nb, t // tt),
        in_specs=[pl.BlockSpec((nb, tt, GLA_COLS_PAD), lambda i, j: (i, j, 0)),
                  _const_spec((LANES, hk)), _const_spec((1, hk)), _const_spec((1, hv))],
        out_specs=pl.BlockSpec((nb, tt, hv), lambda i, j: (i, j, 0)),
        scratch_shapes=[pltpu.VMEM((nb * GLA_HEADS, GLA_DV, GLA_DK), F32)],
        compiler_params=_params(("parallel", "arbitrary")),
        name="gla_mixer",
    )(pa, wa, b_alpha.reshape(1, hk), norm_g.reshape(1, hv))


_RWKV_SB = ("a_dec", "r_dec", "b_inv", "k_inv", "b_end", "k_end", "v")


def _seg_sum(x, seg):
    hw = seg.shape[0]
    parts = [piece[:, s * hw:(s + 1) * hw] for piece in _split3(x) for s in range(2)]
    out = jnp.dot(jnp.concatenate(parts, axis=0), seg, preferred_element_type=F32)
    m = x.shape[0]
    halves = [out[s * m:(s + 1) * m] + out[(2 + s) * m:(3 + s) * m] + out[(4 + s) * m:(5 + s) * m]
              for s in range(2)]
    return jnp.concatenate(halves, axis=1)


def _proj_ab_body(h_ref, g_ref, wa_ref, wb_ref, mu_ref, w0_ref, w2_ref, a0_ref, a2_ref, g2_ref, kk_ref, ka_ref,
                  rk_ref, seg_ref, pa_ref, sb_ref, sf_ref, gl_ref, prev_ref):
    L = RWKV_CHUNK
    W = RWKV_WIDTH
    tm = h_ref.shape[1]
    n_blocks = tm // L

    @pl.when(pl.program_id(1) == 0)
    def _():
        prev_ref[...] = jnp.zeros_like(prev_ref)

    ng = PROJ_ROW_GROUPS
    gm = tm // ng
    G = range(ng)
    rg = [slice(i * gm, (i + 1) * gm) for i in G]
    xn = [_rmsnorm(h_ref[0, rg[i], :], g_ref[...]).astype(BF16) for i in G]
    p = [jnp.dot(xn[i], wb_ref[...], preferred_element_type=F32) for i in G]

    n_a = wa_ref.shape[1]
    a_cols = [(0, 4 * LANES), (4 * LANES, 8 * LANES), (8 * LANES, n_a)]
    pending = [(i, c) for c in a_cols for i in G]

    def project_a(count):
        for _ in range(count):
            if pending:
                i, (lo_c, hi_c) = pending.pop(0)
                pa_ref[0, rg[i], lo_c:hi_c] = jnp.dot(xn[i], wa_ref[:, lo_c:hi_c], preferred_element_type=F32)

    first = lax.broadcasted_iota(jnp.int32, (SUBLANES, p[0].shape[1]), 0) == 0
    before = [prev_ref[0:1, :]] + [p[i][gm - 1:gm, :] for i in range(ng - 1)]
    rolled = [pltpu.roll(p[i], 1, 0) for i in G]
    sh = [jnp.concatenate([jnp.where(first, before[i], rolled[i][0:SUBLANES, :]), rolled[i][SUBLANES:gm, :]],
                          axis=0) for i in G]
    prev_ref[0:1, :] = p[ng - 1][gm - 1:gm, :]
    pf = [p[i] + (sh[i] - p[i]) * mu_ref[...] for i in G]
    r = [pf[i][:, 0:W] for i in G]
    k = [pf[i][:, W:2 * W] for i in G]
    v = [pf[i][:, 2 * W:3 * W] for i in G]
    wa = [pf[i][:, 3 * W:3 * W + LANES] for i in G]
    g_lr = [pf[i][:, 3 * W + LANES:3 * W + 2 * LANES] for i in G]
    log_w = [-RWKV_DECAY_SCALE * jax.nn.sigmoid(w0_ref[...] + _mm(jnp.tanh(wa[i]), w2_ref[...])) for i in G]
    a = [jax.nn.sigmoid(a0_ref[...] + _mm(wa[i], a2_ref[...])) for i in G]
    gate = [_mm(jax.nn.sigmoid(g_lr[i]), g2_ref[...]) for i in G]
    project_a(2)
    kk = [k[i] * kk_ref[...] for i in G]
    kk = [kk[i] * lax.rsqrt(_seg_sum(kk[i] * kk[i], seg_ref[...]) + RWKV_L2_EPS) for i in G]
    project_a(2)
    k = [k[i] * (1.0 + (a[i] - 1.0) * ka_ref[...]) for i in G]
    b_vec = [kk[i] * a[i] for i in G]
    bonus_sum = [_seg_sum(r[i] * k[i] * rk_ref[...], seg_ref[...]) for i in G]
    project_a(2)
    row_in_block = jnp.bitwise_and(lax.broadcasted_iota(jnp.int32, (gm, W), 0), L - 1)
    nbk = gm // L
    for i in G:
        g = log_w[i]
        d = 1
        while d < L:
            g = g + jnp.where(row_in_block >= d, pltpu.roll(g, d, 0), 0.0)
            d *= 2
        g_last = jnp.concatenate([jnp.broadcast_to(g[(j + 1) * L - 1:(j + 1) * L, :], (L, W)) for j in range(nbk)],
                                 axis=0)
        e_neg = jnp.exp(-g)
        e_end = jnp.exp(g_last - g)
        vals = dict(a_dec=-kk[i] * jnp.exp(g - log_w[i]), r_dec=r[i] * jnp.exp(g), b_inv=b_vec[i] * e_neg,
                    k_inv=k[i] * e_neg, b_end=b_vec[i] * e_end, k_end=k[i] * e_end, v=v[i])
        for j, name in enumerate(_RWKV_SB):
            sb_ref[0, rg[i], j * W:(j + 1) * W] = vals[name].astype(BF16)
        sf_ref[0, rg[i], 0:W] = gate[i]
        sf_ref[0, rg[i], W:2 * W] = bonus_sum[i] * v[i]
        for j in range(nbk):
            row0 = (i * nbk + j) * SUBLANES
            gl_ref[0, row0:row0 + SUBLANES, :] = g[(j + 1) * L - SUBLANES:(j + 1) * L, :]
        project_a(1)
    project_a(len(pending))


def proj_ab(h, g, wa, wb, mu, w0, w2, a0, a2, g2, k_k, k_a, r_k):
    b, t, d = h.shape
    tm = min(ROW_TILE, t)
    W = RWKV_WIDTH
    row = lambda x: x.reshape(1, -1)
    w2p = jnp.zeros((LANES, W), BF16).at[:RWKV_LORA_W].set(w2.astype(BF16))
    a2p = jnp.zeros((LANES, W), BF16).at[RWKV_LORA_W:].set(a2.astype(BF16))
    head_of = jnp.arange(W // 2) // RWKV_N
    seg = (head_of[:, None] == head_of[None, :]).astype(BF16)
    consts = [row(g), wa, wb, row(mu), row(w0), w2p, row(a0), a2p, g2.astype(BF16), row(k_k), row(k_a), row(r_k),
              seg]
    nsb = len(_RWKV_SB) * W
    tok = lambda w: pl.BlockSpec((1, tm, w), lambda i, j: (i, j, 0))
    return pl.pallas_call(
        _proj_ab_body,
        out_shape=[jax.ShapeDtypeStruct((b, t, wa.shape[1]), F32), jax.ShapeDtypeStruct((b, t, nsb), BF16),
                   jax.ShapeDtypeStruct((b, t, 2 * W), F32),
                   jax.ShapeDtypeStruct((b, t // RWKV_CHUNK * SUBLANES, W), F32)],
        grid=(b, t // tm),
        in_specs=[tok(d)] + [_const_spec(x.shape) for x in consts],
        out_specs=[tok(wa.shape[1]), tok(nsb), tok(2 * W),
                   pl.BlockSpec((1, tm // RWKV_CHUNK * SUBLANES, W), lambda i, j: (i, j, 0))],
        scratch_shapes=[pltpu.VMEM((SUBLANES, wb.shape[1]), F32)],
        compiler_params=_params(("parallel", "arbitrary")),
        name="proj_ab",
    )(h, *consts)


def _rwkv_rec_body(sb_ref, sf_ref, gl_ref, lng_ref, lnb_ref, seg_ref, tri_ref, quad_ref, o_ref, sp_ref):
    L = RWKV_CHUNK
    W = RWKV_WIDTH
    N = RWKV_N
    nb = sb_ref.shape[0]
    n_chunks = sb_ref.shape[1] // L
    n_pairs = RWKV_HEADS // 2
    idx = {name: j for j, name in enumerate(_RWKV_SB)}

    @pl.when(pl.program_id(1) == 0)
    def _():
        sp_ref[...] = jnp.zeros_like(sp_ref)

    def chunk(c, carry):
        lane = lax.broadcasted_iota(jnp.int32, (L, LANES), 1)
        lo = lane < N
        hi = lane >= N
        nat = (lo, hi)
        strict_f = tri_ref[0]
        incl_f = tri_ref[1]
        eye2 = tri_ref[2]
        diag_f = quad_ref[0]
        anti_f = quad_ref[1]
        zb = jnp.zeros((L, LANES), BF16)
        sel = lambda m, x: jnp.where(m, x, jnp.zeros_like(x))
        swap = lambda x: pltpu.roll(x, N, 1)
        cat0 = lambda xs: jnp.concatenate(xs, axis=0)
        cat1 = lambda xs: jnp.concatenate(xs, axis=1)
        dot = lambda x, y: jnp.dot(x, y, preferred_element_type=F32)
        nt = lambda x, y: lax.dot_general(x, y, (((1,), (1,)), ((), ())), preferred_element_type=F32)
        rows = pl.ds(pl.multiple_of(c * L, L), L)
        U = [(b, p) for b in range(nb) for p in range(n_pairs)]
        nu = range(len(U))

        def blk(name, u):
            col = idx[name] * W + u[1] * LANES
            return sb_ref[u[0], rows, col:col + LANES]

        vn = lambda i, h: sel(nat[h], blk("v", U[i]))
        ad = [blk("a_dec", u) for u in U]
        rd = [blk("r_dec", u) for u in U]
        b_inv = [blk("b_inv", u) for u in U]
        k_inv = [blk("k_inv", u) for u in U]
        pe = [nt(cat0([sel(lo, ad[i]), sel(lo, rd[i])]), cat0([b_inv[i], k_inv[i]])) for i in nu]
        po = [nt(cat0([sel(hi, ad[i]), sel(hi, rd[i])]), cat0([k_inv[i], b_inv[i]])) for i in nu]
        aa = [[pe[i][0:L, :] * strict_f, po[i][0:L, :] * strict_f] for i in nu]
        ar = [[(pe[i][L:2 * L, :] * incl_f).astype(BF16), (po[i][L:2 * L, :] * incl_f).astype(BF16)] for i in nu]
        t0 = [dot(jnp.where(lo, aa[i][1], aa[i][0]).astype(BF16),
                  cat0([cat1([vn(i, 1), zb]), cat1([zb, vn(i, 0)])])) for i in nu]
        ad_sw = [swap(x) for x in ad]
        x0 = [[jnp.where(hi, ad_sw[i], t0[i][:, LANES:2 * LANES].astype(BF16)),
               jnp.where(lo, ad_sw[i], t0[i][:, 0:LANES].astype(BF16))] for i in nu]
        q0 = [jnp.where(lo, aa[i][0], eye2) for i in nu]
        q1 = [jnp.where(lo, eye2, aa[i][1]) for i in nu]
        for _ in range(6):
            out = [dot(jnp.where(lo, q0[i], q1[i]).astype(BF16),
                       cat0([cat1([q0[i].astype(BF16), zb]), cat1([zb, q1[i].astype(BF16)])])) for i in nu]
            q0 = [out[i][:, 0:LANES] + sel(hi, q0[i]) for i in nu]
            q1 = [out[i][:, LANES:2 * LANES] + sel(lo, q1[i]) for i in nu]
        tx = [dot(jnp.where(lo, q1[i], q0[i]).astype(BF16),
                  cat0([cat1([x0[i][1], zb]), cat1([zb, x0[i][0]])])) for i in nu]
        xb = [[tx[i][:, LANES:2 * LANES].astype(BF16), tx[i][:, 0:LANES].astype(BF16)] for i in nu]
        rd_sw = [swap(blk("r_dec", u)).astype(F32) for u in U]
        rmy = [[sel(hi, rd_sw[i]) + dot(ar[i][0], cat0([xb[i][0], vn(i, 0)])),
                sel(lo, rd_sw[i]) + dot(ar[i][1], cat0([vn(i, 1), xb[i][1]]))] for i in nu]
        be_sw = [swap(blk("b_end", u)) for u in U]
        ke_sw = [swap(blk("k_end", u)) for u in U]
        gg = [lax.dot_general(cat0([xb[i][0], xb[i][1], vn(i, 0), vn(i, 1)]),
                              cat0([sel(hi, be_sw[i]), sel(lo, be_sw[i]), sel(hi, ke_sw[i]), sel(lo, ke_sw[i])]),
                              (((0,), (0,)), ((), ())), preferred_element_type=F32) for i in nu]
        sp = [sp_ref[i] for i in nu]
        ys = [_mm_nt(jnp.where(lo, rmy[i][1], rmy[i][0]), sp[i]) + jnp.where(lo, rmy[i][0], rmy[i][1]) for i in nu]
        g_rows = pl.ds(pl.multiple_of(c * SUBLANES, SUBLANES), SUBLANES)
        for i, u in enumerate(U):
            sd_sw = jnp.exp(swap(gl_ref[u[0], g_rows, u[1] * LANES:(u[1] + 1) * LANES])[SUBLANES - 1:SUBLANES, :])
            sp_ref[i] = sp[i] * sd_sw + _mm(sp[i], gg[i] * diag_f) + gg[i] * anti_f
        y = cat0([cat1(ys[b * n_pairs:(b + 1) * n_pairs]) for b in range(nb)])
        mean = _seg_sum(y, seg_ref[...]) * (1.0 / N)
        dy = y - mean
        var = _seg_sum(dy * dy, seg_ref[...]) * (1.0 / N)
        y = dy * lax.rsqrt(var + RWKV_GN_EPS) * lng_ref[...] + lnb_ref[...]
        for b in range(nb):
            yb = y[b * L:(b + 1) * L, :]
            o_ref[b, rows, :] = (yb + sf_ref[b, rows, W:2 * W]) * sf_ref[b, rows, 0:W]
        return carry

    lax.fori_loop(0, n_chunks, chunk, 0)


def rwkv_recurrence(sb, sf, gl, ln_g, ln_b):
    b, t, _ = sb.shape
    tt = min(RWKV_TILE, t)
    nb = min(RWKV_BATCH_PER_STEP, b)
    W = RWKV_WIDTH
    row = lambda x: x.reshape(1, -1)
    head_of = jnp.arange(W // 2) // RWKV_N
    seg = (head_of[:, None] == head_of[None, :]).astype(BF16)
    t_idx = jnp.arange(RWKV_CHUNK)[:, None]
    s_idx = jnp.arange(LANES)[None, :] % RWKV_N
    tri = jnp.stack([t_idx > s_idx, t_idx >= s_idx, t_idx == s_idx]).astype(F32)
    half = jnp.arange(LANES) // RWKV_N
    same_half = half[:, None] == half[None, :]
    quad = jnp.stack([same_half, ~same_half]).astype(F32)
    consts = [row(ln_g), row(ln_b), seg, tri, quad]
    tok = lambda w: pl.BlockSpec((nb, tt, w), lambda i, j: (i, j, 0))
    return pl.pallas_call(
        _rwkv_rec_body,
        out_shape=jax.ShapeDtypeStruct((b, t, W), F32),
        grid=(b // nb, t // tt),
        in_specs=[tok(sb.shape[2]), tok(sf.shape[2]),
                  pl.BlockSpec((nb, tt // RWKV_CHUNK * SUBLANES, W), lambda i, j: (i, j, 0))]
        + [_const_spec(x.shape) for x in consts],
        out_specs=tok(W),
        scratch_shapes=[pltpu.VMEM((nb * RWKV_HEADS // 2, LANES, LANES), F32)],
        compiler_params=_params(("parallel", "arbitrary")),
        name="rwkv_mixer",
    )(sb, sf, gl, *consts)


def _proj_cd_body(h_ref, g_ref, wc_ref, wd_ref, lcw_ref, lcb_ref, gw_ref, gb_ref, lam_ref, mcw_ref, mcb_ref,
                  wq_ref, wk_ref, wv_ref, bif_ref, lr_ref, md_ref, lwin_ref, mwin_ref):
    tm = h_ref.shape[1]
    W = LRU_WIDTH

    @pl.when(pl.program_id(1) == 0)
    def _():
        lwin_ref[0:SUBLANES, :] = jnp.zeros((SUBLANES, W), F32)
        mwin_ref[0:SUBLANES, :] = jnp.zeros((SUBLANES, W), F32)

    ng = PROJ_ROW_GROUPS
    gm = tm // ng
    G = range(ng)
    rg = [slice(i * gm, (i + 1) * gm) for i in G]
    xn = [_rmsnorm(h_ref[0, rg[i], :], g_ref[...]).astype(BF16) for i in G]
    pc = [jnp.dot(xn[i], wc_ref[...], preferred_element_type=F32) for i in G]
    pd = [jnp.dot(xn[i], wd_ref[...], preferred_element_type=F32) for i in G]
    half = W // 2
    for i in G:
        x = pc[i][:, 0:W]
        xc = _causal_conv(x, lwin_ref, lcw_ref[...], lcb_ref[...], LRU_CONV)
        xcb = xc.astype(BF16)
        pre = []
        for gi in range(2):
            pre.append(jnp.concatenate(
                [jnp.dot(xcb[:, n * LRU_BLOCK:(n + 1) * LRU_BLOCK], gw_ref[gi, n], preferred_element_type=F32)
                 for n in range(LRU_BLOCKS)], axis=1) + gb_ref[gi:gi + 1, :])
        r_gate = jax.nn.sigmoid(pre[0])
        i_gate = jax.nn.sigmoid(pre[1])
        log_a = -LRU_C * r_gate * _softplus(-lam_ref[...])
        a = jnp.exp(log_a)
        w2 = -jnp.tanh(log_a) * (a * a + 1.0)
        lr_ref[0, rg[i], 0:W] = a
        lr_ref[0, rg[i], W:2 * W] = jnp.where(w2 > 0.0, w2 * lax.rsqrt(w2), 0.0) * (i_gate * xc)
        lr_ref[0, rg[i], 2 * W:3 * W] = jax.nn.gelu(pc[i][:, W:2 * W])
        mx = pd[i][:, 0:W]
        mxc = _silu(_causal_conv(mx, mwin_ref, mcw_ref[...], mcb_ref[...], MLSTM_CONV)).astype(BF16)
        mxb = mx.astype(BF16)
        for s in range(2):
            cs = slice(s * half, (s + 1) * half)
            md_ref[0, rg[i], s * half:(s + 1) * half] = jnp.dot(mxc[:, cs], wq_ref[s], preferred_element_type=F32)
            md_ref[0, rg[i], W + s * half:W + (s + 1) * half] = (
                jnp.dot(mxc[:, cs], wk_ref[s], preferred_element_type=F32) * MLSTM_DH ** -0.5)
            md_ref[0, rg[i], 2 * W + s * half:2 * W + (s + 1) * half] = jnp.dot(
                mxb[:, cs], wv_ref[s], preferred_element_type=F32)
        md_ref[0, rg[i], 3 * W:4 * W] = jax.nn.sigmoid(pd[i][:, W:2 * W])
        md_ref[0, rg[i], 4 * W:4 * W + LANES] = pd[i][:, 2 * W:2 * W + LANES] + bif_ref[...]


def mlstm_qkv_tiles(qkv_w):
    *lead, three, nblk, d, e = qkv_w.shape
    return _block_diag(qkv_w.reshape(*lead, three, 2, nblk // 2, d, e)).astype(BF16)


def proj_cd(h, g, wc, wd, lru_conv_w, lru_conv_b, gate_w, gate_b, lam, m_conv_w, m_conv_b, qkv_tiles, b_if):
    b, t, d = h.shape
    tm = min(ROW_TILE, t)
    W = LRU_WIDTH
    row = lambda x: x.reshape(1, -1)
    pad_taps = lambda w: jnp.zeros((SUBLANES, W), F32).at[:w.shape[0]].set(w)
    bif = jnp.zeros((1, LANES), F32).at[0, :2 * MLSTM_HEADS].set(b_if)
    consts = [row(g), wc, wd, pad_taps(lru_conv_w), row(lru_conv_b), gate_w.astype(BF16), gate_b, row(lam),
              pad_taps(m_conv_w), row(m_conv_b)] + [qkv_tiles[i] for i in range(3)] + [bif]
    gm = tm // PROJ_ROW_GROUPS
    tok = lambda w: pl.BlockSpec((1, tm, w), lambda i, j: (i, j, 0))
    return pl.pallas_call(
        _proj_cd_body,
        out_shape=[jax.ShapeDtypeStruct((b, t, 3 * W), F32), jax.ShapeDtypeStruct((b, t, MLSTM_IN_COLS), F32)],
        grid=(b, t // tm),
        in_specs=[tok(d)] + [_const_spec(x.shape) for x in consts],
        out_specs=[tok(3 * W), tok(MLSTM_IN_COLS)],
        scratch_shapes=[pltpu.VMEM((SUBLANES + gm, W), F32), pltpu.VMEM((SUBLANES + gm, W), F32)],
        compiler_params=_params(("parallel", "arbitrary")),
        name="proj_cd",
    )(h, *consts)


def _lru_scan_body(p_ref, o_ref, hprev_ref):
    tt = p_ref.shape[1]
    W = LRU_WIDTH

    @pl.when(pl.program_id(1) == 0)
    def _():
        hprev_ref[...] = jnp.zeros_like(hprev_ref)

    a = p_ref[0, :, 0:W]
    u = p_ref[0, :, W:2 * W]
    row_in_group = jnp.bitwise_and(lax.broadcasted_iota(jnp.int32, (tt, W), 0), SUBLANES - 1)
    d = 1
    while d < SUBLANES:
        keep = row_in_group >= d
        u = u + a * jnp.where(keep, pltpu.roll(u, d, 0), 0.0)
        a = a * jnp.where(keep, pltpu.roll(a, d, 0), 1.0)
        d *= 2
    carry = hprev_ref[0:1, :]
    for g in range(tt // SUBLANES):
        rs = slice(g * SUBLANES, (g + 1) * SUBLANES)
        hg = u[rs, :] + a[rs, :] * carry
        carry = hg[SUBLANES - 1:SUBLANES, :]
        o_ref[0, rs, :] = hg * p_ref[0, rs, 2 * W:3 * W]
    hprev_ref[0:1, :] = carry


def lru_scan(lr):
    b, t, cols = lr.shape
    tt = min(ROW_TILE, t)
    W = LRU_WIDTH
    return pl.pallas_call(
        _lru_scan_body,
        out_shape=jax.ShapeDtypeStruct((b, t, W), F32),
        grid=(b, t // tt),
        in_specs=[pl.BlockSpec((1, tt, cols), lambda i, j: (i, j, 0))],
        out_specs=pl.BlockSpec((1, tt, W), lambda i, j: (i, j, 0)),
        scratch_shapes=[pltpu.VMEM((SUBLANES, W), F32)],
        compiler_params=_params(("parallel", "arbitrary")),
        name="rglru_mixer",
    )(lr)


def _mlstm_body(p_ref, ng_ref, o_ref, c_ref, m_ref):
    tt = p_ref.shape[1]
    L = MLSTM_CHUNK
    W = MLSTM_WIDTH
    DH = MLSTM_DH
    H = MLSTM_HEADS
    n_chunks = tt // L
    nb = p_ref.shape[0]

    @pl.when(pl.program_id(1) == 0)
    def _():
        c_ref[...] = jnp.zeros_like(c_ref)
        m_ref[...] = jnp.zeros_like(m_ref)

    def chunk(c, carry):
        incl, _ = _tri_masks(L)
        tri01 = jnp.where(incl, 1.0, 0.0).astype(BF16)
        t0 = pl.multiple_of(c * L, L)
        rows = pl.ds(t0, L)
        nt = lambda y: lax.dot_general(y, tri01, (((1,), (1,)), ((), ())), preferred_element_type=F32)
        gates, lsg, g_t, b_rows = [], [], [], []
        for b in range(nb):
            gates.append(p_ref[b, rows, 4 * W:4 * W + LANES])
            lsg.append(_log_sigmoid(gates[b]))
            g_t.append(gates[b].T[0:SUBLANES, :])
            x1, x2, x3 = _split3(_log_sigmoid(g_t[b]))
            b_rows.append(nt(x1) + nt(x2) + nt(x3))
        ones_b = jnp.ones((L, LANES), BF16)
        U = [(b, h) for b in range(nb) for h in range(H)]
        nu = range(len(U))
        hs = [slice(h * DH, (h + 1) * DH) for h in range(H)]
        col = lambda j, h: slice(j * W + h * DH, j * W + (h + 1) * DH)
        qh = [p_ref[b, rows, col(0, h)] for b, h in U]
        kh = [p_ref[b, rows, col(1, h)] for b, h in U]
        v_aug = [jnp.concatenate([p_ref[b, rows, col(2, h)].astype(BF16), ones_b], axis=1) for b, h in U]
        cm = [c_ref[i] for i in nu]
        m = [m_ref[i, 0:1, :] for i in nu]
        i_rep = [jnp.broadcast_to(gates[b][:, h:h + 1], (L, LANES)) for b, h in U]
        lf_rep = [jnp.broadcast_to(lsg[b][:, H + h:H + h + 1], (L, LANES)) for b, h in U]
        b_rep = [_mm_exact_lhs01(tri01, lf_rep[i]) for i in nu]
        qk = [_mm_nt(qh[i], kh[i]) for i in nu]
        b_last = [b_rep[i][L - 1:L, :] for i in nu]
        log_e = [b_last[i] - b_rep[i] + i_rep[i] for i in nu]
        m_end = [jnp.max(log_e[i], axis=0, keepdims=True) for i in nu]
        m_new = [jnp.maximum(b_last[i] + m[i], m_end[i]) for i in nu]
        kin = [kh[i] * (jnp.exp(m_end[i] - m_new[i]) * jnp.exp(log_e[i] - m_end[i])) for i in nu]
        c_upd = [_mm_tn(kin[i], v_aug[i]) for i in nu]
        run = [i_rep[i] - b_rep[i] for i in nu]
        d = 1
        while d < L:
            run = [jnp.maximum(run[i], _shift_rows(run[i], d, -jnp.inf)) for i in nu]
            d *= 2
        m_t = [jnp.maximum(b_rep[i] + m[i], b_rep[i] + run[i]) for i in nu]
        log_d = [jnp.where(incl, b_rep[i][:, 0:L] - b_rows[b][H + h:H + h + 1, :] + g_t[b][h:h + 1, :], -jnp.inf)
                 for i, (b, h) in enumerate(U)]
        w_loc = [jnp.exp(log_d[i] - m_t[i][:, 0:L]) * qk[i] for i in nu]
        lhs = [jnp.concatenate([(jnp.exp(b_rep[i] + m[i] - m_t[i]) * qh[i]).astype(BF16), w_loc[i].astype(BF16)],
                               axis=1) for i in nu]
        nd = [jnp.dot(lhs[i], jnp.concatenate([cm[i].astype(BF16), v_aug[i]], axis=0),
                      preferred_element_type=F32) for i in nu]
        for i, (b, h) in enumerate(U):
            c_state = jnp.exp(b_last[i] + m[i] - m_new[i])
            c_ref[i] = jnp.concatenate([c_state, c_state], axis=1) * cm[i] + c_upd[i]
            m_ref[i] = jnp.broadcast_to(m_new[i], (SUBLANES, LANES))
            hv = nd[i][:, 0:DH] / jnp.maximum(jnp.abs(nd[i][:, DH:2 * DH]), jnp.exp(-m_t[i]))
            hv = hv * lax.rsqrt(jnp.mean(hv * hv, axis=-1, keepdims=True) + NORM_EPS)
            o_ref[b, rows, hs[h]] = p_ref[b, rows, col(3, h)] * hv * ng_ref[:, hs[h]]
        return carry

    lax.fori_loop(0, n_chunks, chunk, 0)


def _block_diag(w):
    *lead, nb, d, e = w.shape
    rows = jnp.tile(w.reshape(*lead, nb * d, e), (1,) * len(lead) + (1, nb))
    same_block = (jnp.arange(nb * d)[:, None] // d) == (jnp.arange(nb * e)[None, :] // e)
    return jnp.where(same_block, rows, 0.0)


def mlstm_mixer(md, norm_g):
    b, t, cols = md.shape
    tt = min(MIX_TILE, t)
    nb = min(MLSTM_BATCH_PER_STEP, b)
    W = MLSTM_WIDTH
    return pl.pallas_call(
        _mlstm_body,
        out_shape=jax.ShapeDtypeStruct((b, t, W), F32),
        grid=(b // nb, t // tt),
        in_specs=[pl.BlockSpec((nb, tt, cols), lambda i, j: (i, j, 0)), _const_spec((1, W))],
        out_specs=pl.BlockSpec((nb, tt, W), lambda i, j: (i, j, 0)),
        scratch_shapes=[pltpu.VMEM((nb * MLSTM_HEADS, MLSTM_DH, 2 * MLSTM_DH), F32),
                        pltpu.VMEM((nb * MLSTM_HEADS, SUBLANES, LANES), F32)],
        compiler_params=_params(("parallel", "arbitrary")),
        name="mlstm_mixer",
    )(md, norm_g.reshape(1, W))


def _mix_xattn_body(h_ref, ya_ref, yb_ref, wm_ref, g_ref, wq_ref, k_ref, v_ref, wo_ref, o_ref):
    tm = h_ref.shape[1]
    ng = XATTN_ROW_GROUPS
    rg = [slice(i * (tm // ng), (i + 1) * (tm // ng)) for i in range(ng)]
    G = range(ng)
    HD = range(XA_HEADS)
    hs = [slice(hd * XA_DH, (hd + 1) * XA_DH) for hd in HD]
    mixed = [jnp.concatenate([ya_ref[0, rg[i], :], yb_ref[0, rg[i], :]], axis=1).astype(BF16) for i in G]
    h = [h_ref[0, rg[i], :] + jnp.dot(mixed[i], wm_ref[...], preferred_element_type=F32) for i in G]
    xn = [_rmsnorm(h[i], g_ref[...]).astype(BF16) for i in G]
    q = [jnp.dot(xn[i], wq_ref[...], preferred_element_type=F32) for i in G]
    s = [[_mm_nt(q[i][:, hs[hd]], k_ref[0, :, hs[hd]]) * XA_DH ** -0.5 for hd in HD] for i in G]
    e = [[jnp.exp(s[i][hd] - jnp.max(s[i][hd], axis=-1, keepdims=True)) for hd in HD] for i in G]
    p = [[e[i][hd] / jnp.sum(e[i][hd], axis=-1, keepdims=True) for hd in HD] for i in G]
    pv = [[_mm(p[i][hd], v_ref[0, :, hs[hd]]) for hd in HD] for i in G]
    for i in G:
        o = jnp.concatenate(pv[i], axis=1).astype(BF16)
        o_ref[0, rg[i], :] = h[i] + jnp.dot(o, wo_ref[...], preferred_element_type=F32)


def mix_xattn(h, ya, yb, w_mix, g, wq, kv, wo):
    b, t, d = h.shape
    tm = min(XATTN_TILE, t)
    tok = lambda w: pl.BlockSpec((1, tm, w), lambda i, j: (i, j, 0))
    mem_k = pl.BlockSpec((1, MEM_LEN, d), lambda i, j: (i, 0, 0))
    mem_v = pl.BlockSpec((1, MEM_LEN, d), lambda i, j: (i, 0, 1))
    return pl.pallas_call(
        _mix_xattn_body,
        out_shape=jax.ShapeDtypeStruct((b, t, d), F32),
        grid=(b, t // tm),
        in_specs=[tok(d), tok(ya.shape[-1]), tok(yb.shape[-1]), _const_spec(w_mix.shape), _const_spec((1, d)),
                  _const_spec(wq.shape), mem_k, mem_v, _const_spec(wo.shape)],
        out_specs=tok(d),
        compiler_params=_params(("parallel", "parallel")),
        name="mix_xattn",
    )(h, ya, yb, w_mix, g.reshape(1, d), wq, kv, kv, wo)


def _ffn_body(final_norm, h_ref, g_ref, wu_ref, wg_ref, cw_ref, cb_ref, wd_ref, fg_ref, o_ref, uprev_ref):
    tm = h_ref.shape[1]

    @pl.when(pl.program_id(1) == 0)
    def _():
        uprev_ref[...] = jnp.zeros_like(uprev_ref)

    h = h_ref[0]
    xn = _rmsnorm(h, g_ref[...]).astype(BF16)
    acc = h
    start = 0
    for width in FFN_COL_GROUPS:
        cs = slice(start, start + width)
        start += width
        u = jnp.dot(xn, wu_ref[:, cs], preferred_element_type=F32)
        gt = jnp.dot(xn, wg_ref[:, cs], preferred_element_type=F32)
        uu = jnp.concatenate([uprev_ref[:, cs], u], axis=0)
        uprev_ref[:, cs] = u[tm - SUBLANES:tm, :]
        c = cb_ref[:, cs] + cw_ref[2:3, cs] * u
        c = c + cw_ref[1:2, cs] * pltpu.roll(uu, 1, 0)[SUBLANES:SUBLANES + tm, :]
        c = c + cw_ref[0:1, cs] * pltpu.roll(uu, 2, 0)[SUBLANES:SUBLANES + tm, :]
        act = (_silu(c) * gt).astype(BF16)
        acc = acc + jnp.dot(act, wd_ref[cs, :], preferred_element_type=F32)
    if final_norm:
        acc = _rmsnorm(acc, fg_ref[...])
    o_ref[0] = acc


def ffn(h, g, wu, wg, conv_w, conv_b, wd, final_g, final_norm):
    b, t, d = h.shape
    tm = min(ROW_TILE, t)
    tok = pl.BlockSpec((1, tm, d), lambda i, j: (i, j, 0))
    consts = [g.reshape(1, d), wu, wg, conv_w, conv_b, wd, final_g.reshape(1, d)]
    return pl.pallas_call(
        functools.partial(_ffn_body, final_norm),
        out_shape=jax.ShapeDtypeStruct((b, t, d), F32),
        grid=(b, t // tm),
        in_specs=[tok] + [_const_spec(x.shape) for x in consts],
        out_specs=tok,
        scratch_shapes=[pltpu.VMEM((SUBLANES, D_FF_PAD), F32)],
        compiler_params=_params(("parallel", "arbitrary")),
        name="ffn",
    )(h, *consts)


def _pad_cols(w, n):
    return jnp.pad(w, ((0, 0), (0, n - w.shape[1])))


def _rwkv_perm(x):
    W = RWKV_WIDTH
    o_w, o_k, o_v = W, W + RWKV_LORA_W, 2 * W + RWKV_LORA_W
    o_a = 3 * W + RWKV_LORA_W
    o_g = o_a + RWKV_LORA_A
    return jnp.concatenate([x[..., :W], x[..., o_k:o_k + W], x[..., o_v:o_v + W], x[..., o_w:o_w + RWKV_LORA_W],
                            x[..., o_a:o_a + RWKV_LORA_A], x[..., o_g:]], axis=-1)


def kernel(x, mem, mem_norm_g, norm_mix_g, ab_w_in, gla_w_alpha2, gla_b_alpha, gla_norm_g, rwkv_mu, rwkv_w0, rwkv_w2, rwkv_a0, rwkv_a2, rwkv_g2, rwkv_k_k, rwkv_k_a, rwkv_r_k, rwkv_ln_g, rwkv_ln_b, cd_w_in, lru_conv_w, lru_conv_b, lru_gate_w, lru_gate_b, lru_lambda, mlstm_conv_w, mlstm_conv_b, mlstm_qkv_w, mlstm_b_if, mlstm_norm_g, w_mix_out, norm_xattn_g, xattn_wq, xattn_wkv, xattn_wo, norm_ffn_g, ffn_w_up, ffn_conv_w, ffn_conv_b, ffn_w_down, final_norm_g):
    b, t, d = x.shape
    depth = norm_mix_g.shape[0]
    n = b * t
    h = x
    mem2d = mem.reshape(b * MEM_LEN, d)
    qkv_tiles = mlstm_qkv_tiles(mlstm_qkv_w)
    wkv = xattn_wkv.astype(BF16)
    kvs = norm_matmul(mem2d, mem_norm_g, [wkv[layer] for layer in range(depth)], out_dtype=BF16, name="proj_kv")
    for layer in range(depth):
        j = layer // 2
        h2d = h.reshape(n, d)
        if layer % 2 == 0:
            w = ab_w_in[j]
            wa = _pad_cols(w[:, :GLA_COLS], GLA_COLS_PAD).astype(BF16)
            wb = _rwkv_perm(w[:, GLA_COLS:]).astype(BF16)
            pa, sb, sf, gl = proj_ab(h, norm_mix_g[layer], wa, wb, _rwkv_perm(rwkv_mu[j]), rwkv_w0[j], rwkv_w2[j],
                                     rwkv_a0[j], rwkv_a2[j], rwkv_g2[j], rwkv_k_k[j], rwkv_k_a[j], rwkv_r_k[j])
            ya = gla_mixer(pa, gla_w_alpha2[j], gla_b_alpha[j], gla_norm_g[j])
            yb = rwkv_recurrence(sb, sf, gl, rwkv_ln_g[j], rwkv_ln_b[j])
        else:
            w = cd_w_in[j]
            wc = w[:, :2 * LRU_WIDTH].astype(BF16)
            wd_ = _pad_cols(w[:, 2 * LRU_WIDTH:], MLSTM_COLS_PAD).astype(BF16)
            lr, md = proj_cd(h, norm_mix_g[layer], wc, wd_, lru_conv_w[j], lru_conv_b[j], lru_gate_w[j],
                             lru_gate_b[j], lru_lambda[j], mlstm_conv_w[j], mlstm_conv_b[j], qkv_tiles[j],
                             mlstm_b_if[j])
            ya = lru_scan(lr)
            yb = mlstm_mixer(md, mlstm_norm_g[j])
        kv = kvs[layer].reshape(b, MEM_LEN, 2 * d)
        h = mix_xattn(h, ya, yb, w_mix_out[layer].astype(BF16), norm_xattn_g[layer], xattn_wq[layer].astype(BF16),
                      kv, xattn_wo[layer].astype(BF16))
        wup = ffn_w_up[layer]
        wu = _pad_cols(wup[:, :D_FF], D_FF_PAD).astype(BF16)
        wg = _pad_cols(wup[:, D_FF:], D_FF_PAD).astype(BF16)
        cw = _pad_cols(jnp.pad(ffn_conv_w[layer], ((0, SUBLANES - FFN_CONV), (0, 0))), D_FF_PAD)
        cb = _pad_cols(ffn_conv_b[layer].reshape(1, D_FF), D_FF_PAD)
        wdn = jnp.pad(ffn_w_down[layer], ((0, D_FF_PAD - D_FF), (0, 0))).astype(BF16)
        h = ffn(h, norm_ffn_g[layer], wu, wg, cw, cb, wdn, final_norm_g, layer == depth - 1)
    return h
```

```python
import functools

import jax
import jax.numpy as jnp
from jax import lax
from jax.experimental import pallas as pl
from jax.experimental.pallas import tpu as pltpu

F32 = jnp.float32
BF16 = jnp.bfloat16

D_MODEL = 1024
NORM_EPS = 1e-6
LANES = 128
SUBLANES = 8
VMEM_LIMIT_BYTES = 56 * 1024 * 1024

GLA_HEADS, GLA_DK, GLA_DV, GLA_RANK, GLA_TAU, GLA_CHUNK = 4, 64, 128, 16, 16.0, 64
GLA_COLS = 2 * GLA_HEADS * GLA_DK + 2 * GLA_HEADS * GLA_DV + GLA_RANK
GLA_COLS_PAD = 13 * LANES

RWKV_HEADS, RWKV_N, RWKV_WIDTH = 8, 64, 512
RWKV_LORA_W, RWKV_LORA_A, RWKV_LORA_G = 64, 64, 128
RWKV_CHUNK = 64
RWKV_DECAY_SCALE = 0.6065306597126334
RWKV_GN_EPS = RWKV_N * 1e-5
RWKV_L2_EPS = 1e-12
RWKV_COLS = 3 * RWKV_WIDTH + RWKV_LORA_W + RWKV_LORA_A + RWKV_LORA_G

LRU_WIDTH, LRU_BLOCKS, LRU_BLOCK, LRU_C, LRU_CONV = 512, 4, 128, 8.0, 4
MLSTM_HEADS, MLSTM_DH, MLSTM_WIDTH, MLSTM_CONV, MLSTM_CHUNK = 4, 128, 512, 4, 64
MLSTM_COLS_PAD = 2 * MLSTM_WIDTH + LANES
MLSTM_IN_COLS = 4 * MLSTM_WIDTH + LANES

XA_HEADS, XA_DH, MEM_LEN = 4, 256, 256
D_FF, FFN_CONV = 2752, 3
D_FF_PAD = 22 * LANES
MXU_K_TILE = 256
FFN_COL_GROUPS = (6 * MXU_K_TILE, 5 * MXU_K_TILE)

MIX_TILE = 256
RWKV_TILE = 256
GLA_BATCH_PER_STEP = 4
MLSTM_BATCH_PER_STEP = 2
RWKV_BATCH_PER_STEP = 4
ROW_TILE = 512
XATTN_TILE = 1024
XATTN_ROW_GROUPS = 2
PROJ_ROW_GROUPS = 2


def _mm(a, b):
    return jnp.dot(a.astype(BF16), b.astype(BF16), preferred_element_type=F32)


def _mm_nt(a, b):
    return lax.dot_general(a.astype(BF16), b.astype(BF16), (((1,), (1,)), ((), ())), preferred_element_type=F32)


def _mm_tn(a, b):
    return lax.dot_general(a.astype(BF16), b.astype(BF16), (((0,), (0,)), ((), ())), preferred_element_type=F32)


def _split3(x):
    x1 = x.astype(BF16)
    r1 = x - x1.astype(F32)
    x2 = r1.astype(BF16)
    x3 = (r1 - x2.astype(F32)).astype(BF16)
    return x1, x2, x3


def _mm_exact_lhs01(m01, x):
    x1, x2, x3 = _split3(x)
    d = lambda y: jnp.dot(m01, y, preferred_element_type=F32)
    return d(x1) + d(x2) + d(x3)


def _rmsnorm(x, g):
    return x * lax.rsqrt(jnp.mean(x * x, axis=-1, keepdims=True) + NORM_EPS) * g


def _log_sigmoid(x):
    return jnp.minimum(x, 0.0) - jnp.log1p(jnp.exp(-jnp.abs(x)))


def _softplus(x):
    return jnp.maximum(x, 0.0) + jnp.log1p(jnp.exp(-jnp.abs(x)))


def _silu(x):
    return x * jax.nn.sigmoid(x)


def _tri_masks(n):
    r = lax.broadcasted_iota(jnp.int32, (n, n), 0)
    c = lax.broadcasted_iota(jnp.int32, (n, n), 1)
    return r >= c, r > c


def _shift_rows(x, s, fill):
    if s == 0:
        return x
    rolled = pltpu.roll(x, s, 0)
    row = lax.broadcasted_iota(jnp.int32, x.shape, 0)
    return jnp.where(row >= s, rolled, fill)


def _causal_conv(x, buf_ref, w, b, width):
    t = x.shape[0]
    buf_ref[SUBLANES:SUBLANES + t, :] = x
    y = b + w[width - 1:width, :] * x
    for j in range(width - 1):
        s = width - 1 - j
        y = y + w[j:j + 1, :] * buf_ref[SUBLANES - s:SUBLANES - s + t, :]
    buf_ref[0:SUBLANES, :] = x[t - SUBLANES:t, :]
    return y


def _const_spec(shape):
    nd = len(shape)
    return pl.BlockSpec(shape, lambda *_: (0,) * nd, pipeline_mode=pl.Buffered(1))


def _params(sem):
    return pltpu.CompilerParams(dimension_semantics=sem, vmem_limit_bytes=VMEM_LIMIT_BYTES)


def _norm_matmul_body(n_out, x_ref, g_ref, *refs):
    xn = _rmsnorm(x_ref[...], g_ref[...]).astype(BF16)
    for w_ref, o_ref in zip(refs[:n_out], refs[n_out:]):
        o_ref[...] = jnp.dot(xn, w_ref[...], preferred_element_type=F32).astype(o_ref.dtype)


def norm_matmul(x2d, g, ws, out_dtype=F32, name="norm_matmul"):
    n, d = x2d.shape
    tm = min(ROW_TILE, n)
    assert n % tm == 0
    return pl.pallas_call(
        functools.partial(_norm_matmul_body, len(ws)),
        out_shape=[jax.ShapeDtypeStruct((n, w.shape[1]), out_dtype) for w in ws],
        grid=(n // tm,),
        in_specs=[pl.BlockSpec((tm, d), lambda i: (i, 0)), _const_spec((1, d))]
        + [_const_spec(w.shape) for w in ws],
        out_specs=[pl.BlockSpec((tm, w.shape[1]), lambda i: (i, 0)) for w in ws],
        compiler_params=_params(("parallel",)),
        name=name,
    )(x2d, g.reshape(1, d), *ws)


def _gla_body(p_ref, wa_ref, ba_ref, ng_ref, o_ref, st_ref):
    L = GLA_CHUNK
    nb = p_ref.shape[0]
    n_chunks = p_ref.shape[1] // L
    M = nb * L
    hk = GLA_HEADS * GLA_DK
    hv = GLA_HEADS * GLA_DV

    @pl.when(pl.program_id(1) == 0)
    def _():
        st_ref[...] = jnp.zeros_like(st_ref)

    def chunk(c, carry):
        incl, _ = _tri_masks(L)
        ri = lax.broadcasted_iota(jnp.int32, (M, M), 0)
        ci = lax.broadcasted_iota(jnp.int32, (M, M), 1)
        same_seq = jnp.bitwise_and(ri, -L) == jnp.bitwise_and(ci, -L)
        tri01 = jnp.where(same_seq & (ri >= ci), 1.0, 0.0).astype(BF16)
        t0 = pl.multiple_of(c * L, L)
        rows = pl.ds(t0, L)
        cols = lambda lo, hi: jnp.concatenate([p_ref[b, rows, lo:hi] for b in range(nb)], axis=0)
        q = cols(0, hk)
        k = cols(hk, 2 * hk)
        v = cols(2 * hk, 2 * hk + hv)
        gt = cols(2 * hk + hv, 2 * hk + 2 * hv)
        a_lr = cols(2 * hk + 2 * hv, GLA_COLS_PAD)
        la = _log_sigmoid(_mm(a_lr, wa_ref[...]) + ba_ref[...]) * (1.0 / GLA_TAU)
        g = _mm_exact_lhs01(tri01, la)
        g_last = jnp.concatenate([jnp.broadcast_to(g[(b + 1) * L - 1:(b + 1) * L, :], (L, hk)) for b in range(nb)],
                                 axis=0)
        q_dec = (q * GLA_DK ** -0.5) * jnp.exp(g)
        k_inv = k * jnp.exp(-g)
        k_end = k * jnp.exp(g_last - g)
        sd = [jnp.exp(g[(b + 1) * L - 1:(b + 1) * L, :]) for b in range(nb)]
        U = [(b, h) for b in range(nb) for h in range(GLA_HEADS)]
        nu = range(len(U))
        rb = [slice(b * L, (b + 1) * L) for b in range(nb)]
        ks = [slice(h * GLA_DK, (h + 1) * GLA_DK) for h in range(GLA_HEADS)]
        vs = [slice(h * GLA_DV, (h + 1) * GLA_DV) for h in range(GLA_HEADS)]
        qd = [q_dec[rb[b], ks[h]] for b, h in U]
        vh = [v[rb[b], vs[h]] for b, h in U]
        st = [st_ref[i] for i in nu]
        sc = [jnp.where(incl, _mm_nt(qd[i], k_inv[rb[b], ks[h]]), 0.0) for i, (b, h) in enumerate(U)]
        o_state = [_mm_nt(qd[i], st[i]) for i in nu]
        upd = [_mm_tn(vh[i], k_end[rb[b], ks[h]]) for i, (b, h) in enumerate(U)]
        o_loc = [_mm(sc[i], vh[i]) for i in nu]
        for i, (b, h) in enumerate(U):
            st_ref[i] = st[i] * sd[b][:, ks[h]] + upd[i]
            o = o_loc[i] + o_state[i]
            o = o * lax.rsqrt(jnp.mean(o * o, axis=-1, keepdims=True) + NORM_EPS)
            o_ref[b, rows, vs[h]] = o * ng_ref[:, vs[h]] * _silu(gt[rb[b], vs[h]])
        return carry

    lax.fori_loop(0, n_chunks, chunk, 0)


def gla_mixer(pa, w_alpha2, b_alpha, norm_g):
    b, t, _ = pa.shape
    tt = min(MIX_TILE, t)
    nb = min(GLA_BATCH_PER_STEP, b)
    hk, hv = GLA_HEADS * GLA_DK, GLA_HEADS * GLA_DV
    wa = jnp.zeros((LANES, hk), BF16).at[:GLA_RANK].set(w_alpha2.astype(BF16))
    return pl.pallas_call(
        _gla_body,
        out_shape=jax.ShapeDtypeStruct((b, t, hv), F32),
        grid=(b // nb, t // tt),
        in_specs=[pl.BlockSpec((nb, tt, GLA_COLS_PAD), lambda i, j: (i, j, 0)),
                  _const_spec((LANES, hk)), _const_spec((1, hk)), _const_spec((1, hv))],
        out_specs=pl.BlockSpec((nb, tt, hv), lambda i, j: (i, j, 0)),
        scratch_shapes=[pltpu.VMEM((nb * GLA_HEADS, GLA_DV, GLA_DK), F32)],
        compiler_params=_params(("parallel", "arbitrary")),
        name="gla_mixer",
    )(pa, wa, b_alpha.reshape(1, hk), norm_g.reshape(1, hv))


_RWKV_SB = ("a_dec", "r_dec", "b_inv", "k_inv", "b_end", "k_end", "v")


def _seg_sum(x, seg):
    hw = seg.shape[0]
    x1 = x.astype(BF16)
    x2 = (x - x1.astype(F32)).astype(BF16)
    parts = [piece[:, s * hw:(s + 1) * hw] for piece in (x1, x2) for s in range(2)]
    out = jnp.dot(jnp.concatenate(parts, axis=0), seg, preferred_element_type=F32)
    m = x.shape[0]
    halves = [out[s * m:(s + 1) * m] + out[(2 + s) * m:(3 + s) * m] for s in range(2)]
    return jnp.concatenate(halves, axis=1)


def _proj_ab_body(h_ref, g_ref, wa_ref, wb_ref, mu_ref, w0_ref, w2_ref, a0_ref, a2_ref, g2_ref, kk_ref, ka_ref,
                  rk_ref, seg_ref, pa_ref, sb_ref, sf_ref, gl_ref, prev_ref):
    L = RWKV_CHUNK
    W = RWKV_WIDTH
    tm = h_ref.shape[1]
    n_blocks = tm // L

    @pl.when(pl.program_id(1) == 0)
    def _():
        prev_ref[...] = jnp.zeros_like(prev_ref)

    ng = PROJ_ROW_GROUPS
    gm = tm // ng
    G = range(ng)
    rg = [slice(i * gm, (i + 1) * gm) for i in G]
    xn = [_rmsnorm(h_ref[0, rg[i], :], g_ref[...]).astype(BF16) for i in G]
    p = [jnp.dot(xn[i], wb_ref[...], preferred_element_type=F32) for i in G]

    n_a = wa_ref.shape[1]
    a_cols = [(0, 4 * LANES), (4 * LANES, 8 * LANES), (8 * LANES, n_a)]
    pending = [(i, c) for c in a_cols for i in G]

    def project_a(count):
        for _ in range(count):
            if pending:
                i, (lo_c, hi_c) = pending.pop(0)
                pa_ref[0, rg[i], lo_c:hi_c] = jnp.dot(xn[i], wa_ref[:, lo_c:hi_c], preferred_element_type=F32)

    first = lax.broadcasted_iota(jnp.int32, (SUBLANES, p[0].shape[1]), 0) == 0
    before = [prev_ref[0:1, :]] + [p[i][gm - 1:gm, :] for i in range(ng - 1)]
    rolled = [pltpu.roll(p[i], 1, 0) for i in G]
    sh = [jnp.concatenate([jnp.where(first, before[i], rolled[i][0:SUBLANES, :]), rolled[i][SUBLANES:gm, :]],
                          axis=0) for i in G]
    prev_ref[0:1, :] = p[ng - 1][gm - 1:gm, :]
    pf = [p[i] + (sh[i] - p[i]) * mu_ref[...] for i in G]
    r = [pf[i][:, 0:W] for i in G]
    k = [pf[i][:, W:2 * W] for i in G]
    v = [pf[i][:, 2 * W:3 * W] for i in G]
    wa = [pf[i][:, 3 * W:3 * W + LANES] for i in G]
    g_lr = [pf[i][:, 3 * W + LANES:3 * W + 2 * LANES] for i in G]
    log_w = [-RWKV_DECAY_SCALE * jax.nn.sigmoid(w0_ref[...] + _mm(jnp.tanh(wa[i]), w2_ref[...])) for i in G]
    a = [jax.nn.sigmoid(a0_ref[...] + _mm(wa[i], a2_ref[...])) for i in G]
    gate = [_mm(jax.nn.sigmoid(g_lr[i]), g2_ref[...]) for i in G]
    project_a(2)
    kk = [k[i] * kk_ref[...] for i in G]
    kk = [kk[i] * lax.rsqrt(_seg_sum(kk[i] * kk[i], seg_ref[...]) + RWKV_L2_EPS) for i in G]
    project_a(2)
    k = [k[i] * (1.0 + (a[i] - 1.0) * ka_ref[...]) for i in G]
    b_vec = [kk[i] * a[i] for i in G]
    bonus_sum = [_seg_sum(r[i] * k[i] * rk_ref[...], seg_ref[...]) for i in G]
    project_a(2)
    row_in_block = jnp.bitwise_and(lax.broadcasted_iota(jnp.int32, (gm, W), 0), L - 1)
    nbk = gm // L
    for i in G:
        g = log_w[i]
        d = 1
        while d < L:
            g = g + jnp.where(row_in_block >= d, pltpu.roll(g, d, 0), 0.0)
            d *= 2
        g_last = jnp.concatenate([jnp.broadcast_to(g[(j + 1) * L - 1:(j + 1) * L, :], (L, W)) for j in range(nbk)],
                                 axis=0)
        e_neg = jnp.exp(-g)
        e_end = jnp.exp(g_last - g)
        vals = dict(a_dec=-kk[i] * jnp.exp(g - log_w[i]), r_dec=r[i] * jnp.exp(g), b_inv=b_vec[i] * e_neg,
                    k_inv=k[i] * e_neg, b_end=b_vec[i] * e_end, k_end=k[i] * e_end, v=v[i])
        for j, name in enumerate(_RWKV_SB):
            sb_ref[0, rg[i], j * W:(j + 1) * W] = vals[name].astype(BF16)
        sf_ref[0, rg[i], 0:W] = gate[i]
        sf_ref[0, rg[i], W:2 * W] = bonus_sum[i] * v[i]
        for j in range(nbk):
            row0 = (i * nbk + j) * SUBLANES
            gl_ref[0, row0:row0 + SUBLANES, :] = g[(j + 1) * L - SUBLANES:(j + 1) * L, :]
        project_a(1)
    project_a(len(pending))


def proj_ab(h, g, wa, wb, mu, w0, w2, a0, a2, g2, k_k, k_a, r_k):
    b, t, d = h.shape
    tm = min(ROW_TILE, t)
    W = RWKV_WIDTH
    row = lambda x: x.reshape(1, -1)
    w2p = jnp.zeros((LANES, W), BF16).at[:RWKV_LORA_W].set(w2.astype(BF16))
    a2p = jnp.zeros((LANES, W), BF16).at[RWKV_LORA_W:].set(a2.astype(BF16))
    head_of = jnp.arange(W // 2) // RWKV_N
    seg = (head_of[:, None] == head_of[None, :]).astype(BF16)
    consts = [row(g), wa, wb, row(mu), row(w0), w2p, row(a0), a2p, g2.astype(BF16), row(k_k), row(k_a), row(r_k),
              seg]
    nsb = len(_RWKV_SB) * W
    tok = lambda w: pl.BlockSpec((1, tm, w), lambda i, j: (i, j, 0))
    return pl.pallas_call(
        _proj_ab_body,
        out_shape=[jax.ShapeDtypeStruct((b, t, wa.shape[1]), F32), jax.ShapeDtypeStruct((b, t, nsb), BF16),
                   jax.ShapeDtypeStruct((b, t, 2 * W), F32),
                   jax.ShapeDtypeStruct((b, t // RWKV_CHUNK * SUBLANES, W), F32)],
        grid=(b, t // tm),
        in_specs=[tok(d)] + [_const_spec(x.shape) for x in consts],
        out_specs=[tok(wa.shape[1]), tok(nsb), tok(2 * W),
                   pl.BlockSpec((1, tm // RWKV_CHUNK * SUBLANES, W), lambda i, j: (i, j, 0))],
        scratch_shapes=[pltpu.VMEM((SUBLANES, wb.shape[1]), F32)],
        compiler_params=_params(("parallel", "arbitrary")),
        name="proj_ab",
    )(h, *consts)


def _rwkv_rec_body(sb_ref, sf_ref, gl_ref, lng_ref, lnb_ref, seg_ref, tri_ref, quad_ref, o_ref, sp_ref):
    L = RWKV_CHUNK
    W = RWKV_WIDTH
    N = RWKV_N
    nb = sb_ref.shape[0]
    n_chunks = sb_ref.shape[1] // L
    n_pairs = RWKV_HEADS // 2
    idx = {name: j for j, name in enumerate(_RWKV_SB)}

    @pl.when(pl.program_id(1) == 0)
    def _():
        sp_ref[...] = jnp.zeros_like(sp_ref)

    def chunk(c, carry):
        lane = lax.broadcasted_iota(jnp.int32, (L, LANES), 1)
        lo = lane < N
        hi = lane >= N
        nat = (lo, hi)
        strict_f = tri_ref[0]
        incl_f = tri_ref[1]
        eye2 = tri_ref[2]
        diag_f = quad_ref[0]
        anti_f = quad_ref[1]
        zb = jnp.zeros((L, LANES), BF16)
        sel = lambda m, x: jnp.where(m, x, jnp.zeros_like(x))
        swap = lambda x: pltpu.roll(x, N, 1)
        cat0 = lambda xs: jnp.concatenate(xs, axis=0)
        cat1 = lambda xs: jnp.concatenate(xs, axis=1)
        dot = lambda x, y: jnp.dot(x, y, preferred_element_type=F32)
        nt = lambda x, y: lax.dot_general(x, y, (((1,), (1,)), ((), ())), preferred_element_type=F32)
        rows = pl.ds(pl.multiple_of(c * L, L), L)
        U = [(b, p) for b in range(nb) for p in range(n_pairs)]
        nu = range(len(U))

        def blk(name, u):
            col = idx[name] * W + u[1] * LANES
            return sb_ref[u[0], rows, col:col + LANES]

        vn = lambda i, h: sel(nat[h], blk("v", U[i]))
        ad = [blk("a_dec", u) for u in U]
        rd = [blk("r_dec", u) for u in U]
        b_inv = [blk("b_inv", u) for u in U]
        k_inv = [blk("k_inv", u) for u in U]
        pe = [nt(cat0([sel(lo, ad[i]), sel(lo, rd[i])]), cat0([b_inv[i], k_inv[i]])) for i in nu]
        po = [nt(cat0([sel(hi, ad[i]), sel(hi, rd[i])]), cat0([k_inv[i], b_inv[i]])) for i in nu]
        aa = [[pe[i][0:L, :] * strict_f, po[i][0:L, :] * strict_f] for i in nu]
        ar = [[(pe[i][L:2 * L, :] * incl_f).astype(BF16), (po[i][L:2 * L, :] * incl_f).astype(BF16)] for i in nu]
        t0 = [dot(jnp.where(lo, aa[i][1], aa[i][0]).astype(BF16),
                  cat0([cat1([vn(i, 1), zb]), cat1([zb, vn(i, 0)])])) for i in nu]
        ad_sw = [swap(x) for x in ad]
        x0 = [[jnp.where(hi, ad_sw[i], t0[i][:, LANES:2 * LANES].astype(BF16)),
               jnp.where(lo, ad_sw[i], t0[i][:, 0:LANES].astype(BF16))] for i in nu]
        q0 = [jnp.where(lo, aa[i][0], eye2) for i in nu]
        q1 = [jnp.where(lo, eye2, aa[i][1]) for i in nu]
        for _ in range(6):
            out = [dot(jnp.where(lo, q0[i], q1[i]).astype(BF16),
                       cat0([cat1([q0[i].astype(BF16), zb]), cat1([zb, q1[i].astype(BF16)])])) for i in nu]
            q0 = [out[i][:, 0:LANES] + sel(hi, q0[i]) for i in nu]
            q1 = [out[i][:, LANES:2 * LANES] + sel(lo, q1[i]) for i in nu]
        tx = [dot(jnp.where(lo, q1[i], q0[i]).astype(BF16),
                  cat0([cat1([x0[i][1], zb]), cat1([zb, x0[i][0]])])) for i in nu]
        xb = [[tx[i][:, LANES:2 * LANES].astype(BF16), tx[i][:, 0:LANES].astype(BF16)] for i in nu]
        rd_sw = [swap(blk("r_dec", u)).astype(F32) for u in U]
        rmy = [[sel(hi, rd_sw[i]) + dot(ar[i][0], cat0([xb[i][0], vn(i, 0)])),
                sel(lo, rd_sw[i]) + dot(ar[i][1], cat0([vn(i, 1), xb[i][1]]))] for i in nu]
        be_sw = [swap(blk("b_end", u)) for u in U]
        ke_sw = [swap(blk("k_end", u)) for u in U]
        gg = [lax.dot_general(cat0([xb[i][0], xb[i][1], vn(i, 0), vn(i, 1)]),
                              cat0([sel(hi, be_sw[i]), sel(lo, be_sw[i]), sel(hi, ke_sw[i]), sel(lo, ke_sw[i])]),
                              (((0,), (0,)), ((), ())), preferred_element_type=F32) for i in nu]
        sp = [sp_ref[i] for i in nu]
        ys = [_mm_nt(jnp.where(lo, rmy[i][1], rmy[i][0]), sp[i]) + jnp.where(lo, rmy[i][0], rmy[i][1]) for i in nu]
        g_rows = pl.ds(pl.multiple_of(c * SUBLANES, SUBLANES), SUBLANES)
        for i, u in enumerate(U):
            sd_sw = jnp.exp(swap(gl_ref[u[0], g_rows, u[1] * LANES:(u[1] + 1) * LANES])[SUBLANES - 1:SUBLANES, :])
            sp_ref[i] = sp[i] * sd_sw + _mm(sp[i], gg[i] * diag_f) + gg[i] * anti_f
        y = cat0([cat1(ys[b * n_pairs:(b + 1) * n_pairs]) for b in range(nb)])
        mean = _seg_sum(y, seg_ref[...]) * (1.0 / N)
        dy = y - mean
        var = _seg_sum(dy * dy, seg_ref[...]) * (1.0 / N)
        y = dy * lax.rsqrt(var + RWKV_GN_EPS) * lng_ref[...] + lnb_ref[...]
        for b in range(nb):
            yb = y[b * L:(b + 1) * L, :]
            o_ref[b, rows, :] = (yb + sf_ref[b, rows, W:2 * W]) * sf_ref[b, rows, 0:W]
        return carry

    lax.fori_loop(0, n_chunks, chunk, 0)


def rwkv_recurrence(sb, sf, gl, ln_g, ln_b):
    b, t, _ = sb.shape
    tt = min(RWKV_TILE, t)
    nb = min(RWKV_BATCH_PER_STEP, b)
    W = RWKV_WIDTH
    row = lambda x: x.reshape(1, -1)
    head_of = jnp.arange(W // 2) // RWKV_N
    seg = (head_of[:, None] == head_of[None, :]).astype(BF16)
    t_idx = jnp.arange(RWKV_CHUNK)[:, None]
    s_idx = jnp.arange(LANES)[None, :] % RWKV_N
    tri = jnp.stack([t_idx > s_idx, t_idx >= s_idx, t_idx == s_idx]).astype(F32)
    half = jnp.arange(LANES) // RWKV_N
    same_half = half[:, None] == half[None, :]
    quad = jnp.stack([same_half, ~same_half]).astype(F32)
    consts = [row(ln_g), row(ln_b), seg, tri, quad]
    tok = lambda w: pl.BlockSpec((nb, tt, w), lambda i, j: (i, j, 0))
    return pl.pallas_call(
        _rwkv_rec_body,
        out_shape=jax.ShapeDtypeStruct((b, t, W), F32),
        grid=(b // nb, t // tt),
        in_specs=[tok(sb.shape[2]), tok(sf.shape[2]),
                  pl.BlockSpec((nb, tt // RWKV_CHUNK * SUBLANES, W), lambda i, j: (i, j, 0))]
        + [_const_spec(x.shape) for x in consts],
        out_specs=tok(W),
        scratch_shapes=[pltpu.VMEM((nb * RWKV_HEADS // 2, LANES, LANES), F32)],
        compiler_params=_params(("parallel", "arbitrary")),
        name="rwkv_mixer",
    )(sb, sf, gl, *consts)


def _proj_cd_body(h_ref, g_ref, wc_ref, wd_ref, lcw_ref, lcb_ref, gw_ref, gb_ref, lam_ref, mcw_ref, mcb_ref,
                  wq_ref, wk_ref, wv_ref, bif_ref, lr_ref, md_ref, lwin_ref, mwin_ref):
    tm = h_ref.shape[1]
    W = LRU_WIDTH

    @pl.when(pl.program_id(1) == 0)
    def _():
        lwin_ref[0:SUBLANES, :] = jnp.zeros((SUBLANES, W), F32)
        mwin_ref[0:SUBLANES, :] = jnp.zeros((SUBLANES, W), F32)

    ng = PROJ_ROW_GROUPS
    gm = tm // ng
    G = range(ng)
    rg = [slice(i * gm, (i + 1) * gm) for i in G]
    xn = [_rmsnorm(h_ref[0, rg[i], :], g_ref[...]).astype(BF16) for i in G]
    pc = [jnp.dot(xn[i], wc_ref[...], preferred_element_type=F32) for i in G]
    pd = [jnp.dot(xn[i], wd_ref[...], preferred_element_type=F32) for i in G]
    half = W // 2
    for i in G:
        x = pc[i][:, 0:W]
        xc = _causal_conv(x, lwin_ref, lcw_ref[...], lcb_ref[...], LRU_CONV)
        xcb = xc.astype(BF16)
        pre = []
        for gi in range(2):
            pre.append(jnp.concatenate(
                [jnp.dot(xcb[:, n * LRU_BLOCK:(n + 1) * LRU_BLOCK], gw_ref[gi, n], preferred_element_type=F32)
                 for n in range(LRU_BLOCKS)], axis=1) + gb_ref[gi:gi + 1, :])
        r_gate = jax.nn.sigmoid(pre[0])
        i_gate = jax.nn.sigmoid(pre[1])
        log_a = -LRU_C * r_gate * _softplus(-lam_ref[...])
        a = jnp.exp(log_a)
        w2 = -jnp.tanh(log_a) * (a * a + 1.0)
        lr_ref[0, rg[i], 0:W] = a
        lr_ref[0, rg[i], W:2 * W] = jnp.where(w2 > 0.0, w2 * lax.rsqrt(w2), 0.0) * (i_gate * xc)
        lr_ref[0, rg[i], 2 * W:3 * W] = jax.nn.gelu(pc[i][:, W:2 * W])
        mx = pd[i][:, 0:W]
        mxc = _silu(_causal_conv(mx, mwin_ref, mcw_ref[...], mcb_ref[...], MLSTM_CONV)).astype(BF16)
        mxb = mx.astype(BF16)
        for s in range(2):
            cs = slice(s * half, (s + 1) * half)
            md_ref[0, rg[i], s * half:(s + 1) * half] = jnp.dot(mxc[:, cs], wq_ref[s], preferred_element_type=F32)
            md_ref[0, rg[i], W + s * half:W + (s + 1) * half] = (
                jnp.dot(mxc[:, cs], wk_ref[s], preferred_element_type=F32) * MLSTM_DH ** -0.5)
            md_ref[0, rg[i], 2 * W + s * half:2 * W + (s + 1) * half] = jnp.dot(
                mxb[:, cs], wv_ref[s], preferred_element_type=F32)
        md_ref[0, rg[i], 3 * W:4 * W] = jax.nn.sigmoid(pd[i][:, W:2 * W])
        md_ref[0, rg[i], 4 * W:4 * W + LANES] = pd[i][:, 2 * W:2 * W + LANES] + bif_ref[...]


def mlstm_qkv_tiles(qkv_w):
    *lead, three, nblk, d, e = qkv_w.shape
    return _block_diag(qkv_w.reshape(*lead, three, 2, nblk // 2, d, e)).astype(BF16)


def proj_cd(h, g, wc, wd, lru_conv_w, lru_conv_b, gate_w, gate_b, lam, m_conv_w, m_conv_b, qkv_tiles, b_if):
    b, t, d = h.shape
    tm = min(ROW_TILE, t)
    W = LRU_WIDTH
    row = lambda x: x.reshape(1, -1)
    pad_taps = lambda w: jnp.zeros((SUBLANES, W), F32).at[:w.shape[0]].set(w)
    bif = jnp.zeros((1, LANES), F32).at[0, :2 * MLSTM_HEADS].set(b_if)
    consts = [row(g), wc, wd, pad_taps(lru_conv_w), row(lru_conv_b), gate_w.astype(BF16), gate_b, row(lam),
              pad_taps(m_conv_w), row(m_conv_b)] + [qkv_tiles[i] for i in range(3)] + [bif]
    gm = tm // PROJ_ROW_GROUPS
    tok = lambda w: pl.BlockSpec((1, tm, w), lambda i, j: (i, j, 0))
    return pl.pallas_call(
        _proj_cd_body,
        out_shape=[jax.ShapeDtypeStruct((b, t, 3 * W), F32), jax.ShapeDtypeStruct((b, t, MLSTM_IN_COLS), F32)],
        grid=(b, t // tm),
        in_specs=[tok(d)] + [_const_spec(x.shape) for x in consts],
        out_specs=[tok(3 * W), tok(MLSTM_IN_COLS)],
        scratch_shapes=[pltpu.VMEM((SUBLANES + gm, W), F32), pltpu.VMEM((SUBLANES + gm, W), F32)],
        compiler_params=_params(("parallel", "arbitrary")),
        name="proj_cd",
    )(h, *consts)


def _lru_scan_body(p_ref, o_ref, hprev_ref):
    tt = p_ref.shape[1]
    W = LRU_WIDTH

    @pl.when(pl.program_id(1) == 0)
    def _():
        hprev_ref[...] = jnp.zeros_like(hprev_ref)

    a = p_ref[0, :, 0:W]
    u = p_ref[0, :, W:2 * W]
    row_in_group = jnp.bitwise_and(lax.broadcasted_iota(jnp.int32, (tt, W), 0), SUBLANES - 1)
    d = 1
    while d < SUBLANES:
        keep = row_in_group >= d
        u = u + a * jnp.where(keep, pltpu.roll(u, d, 0), 0.0)
        a = a * jnp.where(keep, pltpu.roll(a, d, 0), 1.0)
        d *= 2
    carry = hprev_ref[0:1, :]
    for g in range(tt // SUBLANES):
        rs = slice(g * SUBLANES, (g + 1) * SUBLANES)
        hg = u[rs, :] + a[rs, :] * carry
        carry = hg[SUBLANES - 1:SUBLANES, :]
        o_ref[0, rs, :] = hg * p_ref[0, rs, 2 * W:3 * W]
    hprev_ref[0:1, :] = carry


def lru_scan(lr):
    b, t, cols = lr.shape
    tt = min(ROW_TILE, t)
    W = LRU_WIDTH
    return pl.pallas_call(
        _lru_scan_body,
        out_shape=jax.ShapeDtypeStruct((b, t, W), F32),
        grid=(b, t // tt),
        in_specs=[pl.BlockSpec((1, tt, cols), lambda i, j: (i, j, 0))],
        out_specs=pl.BlockSpec((1, tt, W), lambda i, j: (i, j, 0)),
        scratch_shapes=[pltpu.VMEM((SUBLANES, W), F32)],
        compiler_params=_params(("parallel", "arbitrary")),
        name="rglru_mixer",
    )(lr)


def _mlstm_body(p_ref, ng_ref, o_ref, c_ref, m_ref):
    tt = p_ref.shape[1]
    L = MLSTM_CHUNK
    W = MLSTM_WIDTH
    DH = MLSTM_DH
    H = MLSTM_HEADS
    n_chunks = tt // L
    nb = p_ref.shape[0]

    @pl.when(pl.program_id(1) == 0)
    def _():
        c_ref[...] = jnp.zeros_like(c_ref)
        m_ref[...] = jnp.zeros_like(m_ref)

    def chunk(c, carry):
        incl, _ = _tri_masks(L)
        tri01 = jnp.where(incl, 1.0, 0.0).astype(BF16)
        t0 = pl.multiple_of(c * L, L)
        rows = pl.ds(t0, L)
        nt = lambda y: lax.dot_general(y, tri01, (((1,), (1,)), ((), ())), preferred_element_type=F32)
        gates, lsg, g_t, b_rows = [], [], [], []
        for b in range(nb):
            gates.append(p_ref[b, rows, 4 * W:4 * W + LANES])
            lsg.append(_log_sigmoid(gates[b]))
            g_t.append(gates[b].T[0:SUBLANES, :])
            x1, x2, x3 = _split3(_log_sigmoid(g_t[b]))
            b_rows.append(nt(x1) + nt(x2) + nt(x3))
        ones_b = jnp.ones((L, LANES), BF16)
        U = [(b, h) for b in range(nb) for h in range(H)]
        nu = range(len(U))
        hs = [slice(h * DH, (h + 1) * DH) for h in range(H)]
        col = lambda j, h: slice(j * W + h * DH, j * W + (h + 1) * DH)
        qh = [p_ref[b, rows, col(0, h)] for b, h in U]
        kh = [p_ref[b, rows, col(1, h)] for b, h in U]
        v_aug = [jnp.concatenate([p_ref[b, rows, col(2, h)].astype(BF16), ones_b], axis=1) for b, h in U]
        cm = [c_ref[i] for i in nu]
        m = [m_ref[i, 0:1, :] for i in nu]
        i_rep = [jnp.broadcast_to(gates[b][:, h:h + 1], (L, LANES)) for b, h in U]
        lf_rep = [jnp.broadcast_to(lsg[b][:, H + h:H + h + 1], (L, LANES)) for b, h in U]
        b_rep = [_mm_exact_lhs01(tri01, lf_rep[i]) for i in nu]
        qk = [_mm_nt(qh[i], kh[i]) for i in nu]
        b_last = [b_rep[i][L - 1:L, :] for i in nu]
        log_e = [b_last[i] - b_rep[i] + i_rep[i] for i in nu]
        m_end = [jnp.max(log_e[i], axis=0, keepdims=True) for i in nu]
        m_new = [jnp.maximum(b_last[i] + m[i], m_end[i]) for i in nu]
        kin = [kh[i] * (jnp.exp(m_end[i] - m_new[i]) * jnp.exp(log_e[i] - m_end[i])) for i in nu]
        c_upd = [_mm_tn(kin[i], v_aug[i]) for i in nu]
        run = [i_rep[i] - b_rep[i] for i in nu]
        d = 1
        while d < L:
            run = [jnp.maximum(run[i], _shift_rows(run[i], d, -jnp.inf)) for i in nu]
            d *= 2
        m_t = [jnp.maximum(b_rep[i] + m[i], b_rep[i] + run[i]) for i in nu]
        log_d = [jnp.where(incl, b_rep[i][:, 0:L] - b_rows[b][H + h:H + h + 1, :] + g_t[b][h:h + 1, :], -jnp.inf)
                 for i, (b, h) in enumerate(U)]
        w_loc = [jnp.exp(log_d[i] - m_t[i][:, 0:L]) * qk[i] for i in nu]
        lhs = [jnp.concatenate([(jnp.exp(b_rep[i] + m[i] - m_t[i]) * qh[i]).astype(BF16), w_loc[i].astype(BF16)],
                               axis=1) for i in nu]
        nd = [jnp.dot(lhs[i], jnp.concatenate([cm[i].astype(BF16), v_aug[i]], axis=0),
                      preferred_element_type=F32) for i in nu]
        for i, (b, h) in enumerate(U):
            c_state = jnp.exp(b_last[i] + m[i] - m_new[i])
            c_ref[i] = jnp.concatenate([c_state, c_state], axis=1) * cm[i] + c_upd[i]
            m_ref[i] = jnp.broadcast_to(m_new[i], (SUBLANES, LANES))
            hv = nd[i][:, 0:DH] / jnp.maximum(jnp.abs(nd[i][:, DH:2 * DH]), jnp.exp(-m_t[i]))
            hv = hv * lax.rsqrt(jnp.mean(hv * hv, axis=-1, keepdims=True) + NORM_EPS)
            o_ref[b, rows, hs[h]] = p_ref[b, rows, col(3, h)] * hv * ng_ref[:, hs[h]]
        return carry

    lax.fori_loop(0, n_chunks, chunk, 0)


def _block_diag(w):
    *lead, nb, d, e = w.shape
    rows = jnp.tile(w.reshape(*lead, nb * d, e), (1,) * len(lead) + (1, nb))
    same_block = (jnp.arange(nb * d)[:, None] // d) == (jnp.arange(nb * e)[None, :] // e)
    return jnp.where(same_block, rows, 0.0)


def mlstm_mixer(md, norm_g):
    b, t, cols = md.shape
    tt = min(MIX_TILE, t)
    nb = min(MLSTM_BATCH_PER_STEP, b)
    W = MLSTM_WIDTH
    return pl.pallas_call(
        _mlstm_body,
        out_shape=jax.ShapeDtypeStruct((b, t, W), F32),
        grid=(b // nb, t // tt),
        in_specs=[pl.BlockSpec((nb, tt, cols), lambda i, j: (i, j, 0)), _const_spec((1, W))],
        out_specs=pl.BlockSpec((nb, tt, W), lambda i, j: (i, j, 0)),
        scratch_shapes=[pltpu.VMEM((nb * MLSTM_HEADS, MLSTM_DH, 2 * MLSTM_DH), F32),
                        pltpu.VMEM((nb * MLSTM_HEADS, SUBLANES, LANES), F32)],
        compiler_params=_params(("parallel", "arbitrary")),
        name="mlstm_mixer",
    )(md, norm_g.reshape(1, W))


def _mix_xattn_body(h_ref, ya_ref, yb_ref, wm_ref, g_ref, wq_ref, k_ref, v_ref, wo_ref, o_ref):
    tm = h_ref.shape[1]
    ng = XATTN_ROW_GROUPS
    rg = [slice(i * (tm // ng), (i + 1) * (tm // ng)) for i in range(ng)]
    G = range(ng)
    HD = range(XA_HEADS)
    hs = [slice(hd * XA_DH, (hd + 1) * XA_DH) for hd in HD]
    mixed = [jnp.concatenate([ya_ref[0, rg[i], :], yb_ref[0, rg[i], :]], axis=1).astype(BF16) for i in G]
    h = [h_ref[0, rg[i], :] + jnp.dot(mixed[i], wm_ref[...], preferred_element_type=F32) for i in G]
    xn = [_rmsnorm(h[i], g_ref[...]).astype(BF16) for i in G]
    q = [jnp.dot(xn[i], wq_ref[...], preferred_element_type=F32) for i in G]
    s = [[_mm_nt(q[i][:, hs[hd]], k_ref[0, :, hs[hd]]) * XA_DH ** -0.5 for hd in HD] for i in G]
    e = [[jnp.exp(s[i][hd] - jnp.max(s[i][hd], axis=-1, keepdims=True)) for hd in HD] for i in G]
    p = [[e[i][hd] / jnp.sum(e[i][hd], axis=-1, keepdims=True) for hd in HD] for i in G]
    pv = [[_mm(p[i][hd], v_ref[0, :, hs[hd]]) for hd in HD] for i in G]
    for i in G:
        o = jnp.concatenate(pv[i], axis=1).astype(BF16)
        o_ref[0, rg[i], :] = h[i] + jnp.dot(o, wo_ref[...], preferred_element_type=F32)


def mix_xattn(h, ya, yb, w_mix, g, wq, kv, wo):
    b, t, d = h.shape
    tm = min(XATTN_TILE, t)
    tok = lambda w: pl.BlockSpec((1, tm, w), lambda i, j: (i, j, 0))
    mem_k = pl.BlockSpec((1, MEM_LEN, d), lambda i, j: (i, 0, 0))
    mem_v = pl.BlockSpec((1, MEM_LEN, d), lambda i, j: (i, 0, 1))
    return pl.pallas_call(
        _mix_xattn_body,
        out_shape=jax.ShapeDtypeStruct((b, t, d), F32),
        grid=(b, t // tm),
        in_specs=[tok(d), tok(ya.shape[-1]), tok(yb.shape[-1]), _const_spec(w_mix.shape), _const_spec((1, d)),
                  _const_spec(wq.shape), mem_k, mem_v, _const_spec(wo.shape)],
        out_specs=tok(d),
        compiler_params=_params(("parallel", "parallel")),
        name="mix_xattn",
    )(h, ya, yb, w_mix, g.reshape(1, d), wq, kv, kv, wo)


def _ffn_body(final_norm, h_ref, g_ref, wu_ref, wg_ref, cw_ref, cb_ref, wd_ref, fg_ref, o_ref, uprev_ref):
    tm = h_ref.shape[1]

    @pl.when(pl.program_id(1) == 0)
    def _():
        uprev_ref[...] = jnp.zeros_like(uprev_ref)

    h = h_ref[0]
    xn = _rmsnorm(h, g_ref[...]).astype(BF16)
    acc = h
    start = 0
    for width in FFN_COL_GROUPS:
        cs = slice(start, start + width)
        start += width
        u = jnp.dot(xn, wu_ref[:, cs], preferred_element_type=F32)
        gt = jnp.dot(xn, wg_ref[:, cs], preferred_element_type=F32)
        uu = jnp.concatenate([uprev_ref[:, cs], u], axis=0)
        uprev_ref[:, cs] = u[tm - SUBLANES:tm, :]
        c = cb_ref[:, cs] + cw_ref[2:3, cs] * u
        c = c + cw_ref[1:2, cs] * pltpu.roll(uu, 1, 0)[SUBLANES:SUBLANES + tm, :]
        c = c + cw_ref[0:1, cs] * pltpu.roll(uu, 2, 0)[SUBLANES:SUBLANES + tm, :]
        act = (_silu(c) * gt).astype(BF16)
        acc = acc + jnp.dot(act, wd_ref[cs, :], preferred_element_type=F32)
    if final_norm:
        acc = _rmsnorm(acc, fg_ref[...])
    o_ref[0] = acc


def ffn(h, g, wu, wg, conv_w, conv_b, wd, final_g, final_norm):
    b, t, d = h.shape
    tm = min(ROW_TILE, t)
    tok = pl.BlockSpec((1, tm, d), lambda i, j: (i, j, 0))
    consts = [g.reshape(1, d), wu, wg, conv_w, conv_b, wd, final_g.reshape(1, d)]
    return pl.pallas_call(
        functools.partial(_ffn_body, final_norm),
        out_shape=jax.ShapeDtypeStruct((b, t, d), F32),
        grid=(b, t // tm),
        in_specs=[tok] + [_const_spec(x.shape) for x in consts],
        out_specs=tok,
        scratch_shapes=[pltpu.VMEM((SUBLANES, D_FF_PAD), F32)],
        compiler_params=_params(("parallel", "arbitrary")),
        name="ffn",
    )(h, *consts)


def _pad_cols(w, n):
    return jnp.pad(w, ((0, 0), (0, n - w.shape[1])))


def _rwkv_perm(x):
    W = RWKV_WIDTH
    o_w, o_k, o_v = W, W + RWKV_LORA_W, 2 * W + RWKV_LORA_W
    o_a = 3 * W + RWKV_LORA_W
    o_g = o_a + RWKV_LORA_A
    return jnp.concatenate([x[..., :W], x[..., o_k:o_k + W], x[..., o_v:o_v + W], x[..., o_w:o_w + RWKV_LORA_W],
                            x[..., o_a:o_a + RWKV_LORA_A], x[..., o_g:]], axis=-1)


def kernel(x, mem, mem_norm_g, norm_mix_g, ab_w_in, gla_w_alpha2, gla_b_alpha, gla_norm_g, rwkv_mu, rwkv_w0, rwkv_w2, rwkv_a0, rwkv_a2, rwkv_g2, rwkv_k_k, rwkv_k_a, rwkv_r_k, rwkv_ln_g, rwkv_ln_b, cd_w_in, lru_conv_w, lru_conv_b, lru_gate_w, lru_gate_b, lru_lambda, mlstm_conv_w, mlstm_conv_b, mlstm_qkv_w, mlstm_b_if, mlstm_norm_g, w_mix_out, norm_xattn_g, xattn_wq, xattn_wkv, xattn_wo, norm_ffn_g, ffn_w_up, ffn_conv_w, ffn_conv_b, ffn_w_down, final_norm_g):
    b, t, d = x.shape
    depth = norm_mix_g.shape[0]
    n = b * t
    h = x
    mem2d = mem.reshape(b * MEM_LEN, d)
    qkv_tiles = mlstm_qkv_tiles(mlstm_qkv_w)
    wkv = xattn_wkv.astype(BF16)
    kvs = norm_matmul(mem2d, mem_norm_g, [wkv[layer] for layer in range(depth)], out_dtype=BF16, name="proj_kv")
    for layer in range(depth):
        j = layer // 2
        h2d = h.reshape(n, d)
        if layer % 2 == 0:
            w = ab_w_in[j]
            wa = _pad_cols(w[:, :GLA_COLS], GLA_COLS_PAD).astype(BF16)
            wb = _rwkv_perm(w[:, GLA_COLS:]).astype(BF16)
            pa, sb, sf, gl = proj_ab(h, norm_mix_g[layer], wa, wb, _rwkv_perm(rwkv_mu[j]), rwkv_w0[j], rwkv_w2[j],
                                     rwkv_a0[j], rwkv_a2[j], rwkv_g2[j], rwkv_k_k[j], rwkv_k_a[j], rwkv_r_k[j])
            ya = gla_mixer(pa, gla_w_alpha2[j], gla_b_alpha[j], gla_norm_g[j])
            yb = rwkv_recurrence(sb, sf, gl, rwkv_ln_g[j], rwkv_ln_b[j])
        else:
            w = cd_w_in[j]
            wc = w[:, :2 * LRU_WIDTH].astype(BF16)
            wd_ = _pad_cols(w[:, 2 * LRU_WIDTH:], MLSTM_COLS_PAD).astype(BF16)
            lr, md = proj_cd(h, norm_mix_g[layer], wc, wd_, lru_conv_w[j], lru_conv_b[j], lru_gate_w[j],
                             lru_gate_b[j], lru_lambda[j], mlstm_conv_w[j], mlstm_conv_b[j], qkv_tiles[j],
                             mlstm_b_if[j])
            ya = lru_scan(lr)
            yb = mlstm_mixer(md, mlstm_norm_g[j])
        kv = kvs[layer].reshape(b, MEM_LEN, 2 * d)
        h = mix_xattn(h, ya, yb, w_mix_out[layer].astype(BF16), norm_xattn_g[layer], xattn_wq[layer].astype(BF16),
                      kv, xattn_wo[layer].astype(BF16))
        wup = ffn_w_up[layer]
        wu = _pad_cols(wup[:, :D_FF], D_FF_PAD).astype(BF16)
        wg = _pad_cols(wup[:, D_FF:], D_FF_PAD).astype(BF16)
        cw = _pad_cols(jnp.pad(ffn_conv_w[layer], ((0, SUBLANES - FFN_CONV), (0, 0))), D_FF_PAD)
        cb = _pad_cols(ffn_conv_b[layer].reshape(1, D_FF), D_FF_PAD)
        wdn = jnp.pad(ffn_w_down[layer], ((0, D_FF_PAD - D_FF), (0, 0))).astype(BF16)
        h = ffn(h, norm_ffn_g[layer], wu, wg, cw, cb, wdn, final_norm_g, layer == depth - 1)
    return h
```

```python
import functools

import jax
import jax.numpy as jnp
from jax import lax
from jax.experimental import pallas as pl
from jax.experimental.pallas import tpu as pltpu

F32 = jnp.float32
BF16 = jnp.bfloat16

NORM_EPS = 1e-6
LANES = 128
SUBLANES = 8
VMEM_LIMIT_BYTES = 56 * 1024 * 1024

GLA_HEADS, GLA_DK, GLA_DV, GLA_RANK, GLA_TAU, GLA_CHUNK = 4, 64, 128, 16, 16.0, 64
GLA_COLS = 2 * GLA_HEADS * GLA_DK + 2 * GLA_HEADS * GLA_DV + GLA_RANK
GLA_COLS_PAD = 13 * LANES

RWKV_HEADS, RWKV_N, RWKV_WIDTH = 8, 64, 512
RWKV_LORA_W, RWKV_LORA_A = 64, 64
RWKV_CHUNK = 64
RWKV_DECAY_SCALE = 0.6065306597126334
RWKV_GN_EPS = RWKV_N * 1e-5
RWKV_L2_EPS = 1e-12

LRU_WIDTH, LRU_BLOCKS, LRU_BLOCK, LRU_C, LRU_CONV = 512, 4, 128, 8.0, 4
MLSTM_HEADS, MLSTM_DH, MLSTM_WIDTH, MLSTM_CONV, MLSTM_CHUNK = 4, 128, 512, 4, 64
MLSTM_COLS_PAD = 2 * MLSTM_WIDTH + LANES
MLSTM_IN_COLS = 4 * MLSTM_WIDTH + LANES

XA_HEADS, XA_DH, MEM_LEN = 4, 256, 256
D_FF, FFN_CONV = 2752, 3
D_FF_PAD = 22 * LANES
MXU_K_TILE = 256
FFN_COL_GROUPS = (6 * MXU_K_TILE, 5 * MXU_K_TILE)

MIX_TILE = 256
GLA_BATCH_PER_STEP = 4
MLSTM_BATCH_PER_STEP = 2
RWKV_BATCH_PER_STEP = 4
ROW_TILE = 512
XATTN_TILE = 1024
XATTN_ROW_GROUPS = 2
PROJ_ROW_GROUPS = 2


def _mm(a, b):
    return jnp.dot(a.astype(BF16), b.astype(BF16), preferred_element_type=F32)


def _mm_nt(a, b):
    return lax.dot_general(a.astype(BF16), b.astype(BF16), (((1,), (1,)), ((), ())), preferred_element_type=F32)


def _mm_tn(a, b):
    return lax.dot_general(a.astype(BF16), b.astype(BF16), (((0,), (0,)), ((), ())), preferred_element_type=F32)


def _split3(x):
    x1 = x.astype(BF16)
    r1 = x - x1.astype(F32)
    x2 = r1.astype(BF16)
    x3 = (r1 - x2.astype(F32)).astype(BF16)
    return x1, x2, x3


def _mm_exact_lhs01(m01, x):
    x1, x2, x3 = _split3(x)
    d = lambda y: jnp.dot(m01, y, preferred_element_type=F32)
    return d(x1) + d(x2) + d(x3)


def _rmsnorm(x, g):
    return x * lax.rsqrt(jnp.mean(x * x, axis=-1, keepdims=True) + NORM_EPS) * g


def _log_sigmoid(x):
    return jnp.minimum(x, 0.0) - jnp.log1p(jnp.exp(-jnp.abs(x)))


def _softplus(x):
    return jnp.maximum(x, 0.0) + jnp.log1p(jnp.exp(-jnp.abs(x)))


def _silu(x):
    return x * jax.nn.sigmoid(x)


def _tri_masks(n):
    r = lax.broadcasted_iota(jnp.int32, (n, n), 0)
    c = lax.broadcasted_iota(jnp.int32, (n, n), 1)
    return r >= c, r > c


def _shift_rows(x, s, fill):
    if s == 0:
        return x
    rolled = pltpu.roll(x, s, 0)
    row = lax.broadcasted_iota(jnp.int32, x.shape, 0)
    return jnp.where(row >= s, rolled, fill)


def _causal_conv(x, buf_ref, w, b, width):
    t = x.shape[0]
    buf_ref[SUBLANES:SUBLANES + t, :] = x
    y = b + w[width - 1:width, :] * x
    for j in range(width - 1):
        s = width - 1 - j
        y = y + w[j:j + 1, :] * buf_ref[SUBLANES - s:SUBLANES - s + t, :]
    buf_ref[0:SUBLANES, :] = x[t - SUBLANES:t, :]
    return y


def _const_spec(shape):
    nd = len(shape)
    return pl.BlockSpec(shape, lambda *_: (0,) * nd, pipeline_mode=pl.Buffered(1))


def _params(sem):
    return pltpu.CompilerParams(dimension_semantics=sem, vmem_limit_bytes=VMEM_LIMIT_BYTES)


def _norm_matmul_body(n_out, x_ref, g_ref, *refs):
    xn = _rmsnorm(x_ref[...], g_ref[...]).astype(BF16)
    for w_ref, o_ref in zip(refs[:n_out], refs[n_out:]):
        o_ref[...] = jnp.dot(xn, w_ref[...], preferred_element_type=F32).astype(o_ref.dtype)


def norm_matmul(x2d, g, ws, out_dtype=F32, name="norm_matmul"):
    n, d = x2d.shape
    tm = min(ROW_TILE, n)
    assert n % tm == 0
    return pl.pallas_call(
        functools.partial(_norm_matmul_body, len(ws)),
        out_shape=[jax.ShapeDtypeStruct((n, w.shape[1]), out_dtype) for w in ws],
        grid=(n // tm,),
        in_specs=[pl.BlockSpec((tm, d), lambda i: (i, 0)), _const_spec((1, d))]
        + [_const_spec(w.shape) for w in ws],
        out_specs=[pl.BlockSpec((tm, w.shape[1]), lambda i: (i, 0)) for w in ws],
        compiler_params=_params(("parallel",)),
        name=name,
    )(x2d, g.reshape(1, d), *ws)


def _gla_body(p_ref, wa_ref, ba_ref, ng_ref, o_ref, st_ref):
    L = GLA_CHUNK
    nb = p_ref.shape[0]
    n_chunks = p_ref.shape[1] // L
    M = nb * L
    hk = GLA_HEADS * GLA_DK
    hv = GLA_HEADS * GLA_DV

    @pl.when(pl.program_id(1) == 0)
    def _():
        st_ref[...] = jnp.zeros_like(st_ref)

    def chunk(c, carry):
        incl, _ = _tri_masks(L)
        ri = lax.broadcasted_iota(jnp.int32, (M, M), 0)
        ci = lax.broadcasted_iota(jnp.int32, (M, M), 1)
        same_seq = jnp.bitwise_and(ri, -L) == jnp.bitwise_and(ci, -L)
        tri01 = jnp.where(same_seq & (ri >= ci), 1.0, 0.0).astype(BF16)
        t0 = pl.multiple_of(c * L, L)
        rows = pl.ds(t0, L)
        cols = lambda lo, hi: jnp.concatenate([p_ref[b, rows, lo:hi] for b in range(nb)], axis=0)
        q = cols(0, hk)
        k = cols(hk, 2 * hk)
        v = cols(2 * hk, 2 * hk + hv)
        gt = cols(2 * hk + hv, 2 * hk + 2 * hv)
        a_lr = cols(2 * hk + 2 * hv, GLA_COLS_PAD)
        la = _log_sigmoid(_mm(a_lr, wa_ref[...]) + ba_ref[...]) * (1.0 / GLA_TAU)
        g = _mm_exact_lhs01(tri01, la)
        g_last = jnp.concatenate([jnp.broadcast_to(g[(b + 1) * L - 1:(b + 1) * L, :], (L, hk)) for b in range(nb)],
                                 axis=0)
        q_dec = (q * GLA_DK ** -0.5) * jnp.exp(g)
        k_inv = k * jnp.exp(-g)
        k_end = k * jnp.exp(g_last - g)
        sd = [jnp.exp(g[(b + 1) * L - 1:(b + 1) * L, :]) for b in range(nb)]
        U = [(b, h) for b in range(nb) for h in range(GLA_HEADS)]
        nu = range(len(U))
        rb = [slice(b * L, (b + 1) * L) for b in range(nb)]
        ks = [slice(h * GLA_DK, (h + 1) * GLA_DK) for h in range(GLA_HEADS)]
        vs = [slice(h * GLA_DV, (h + 1) * GLA_DV) for h in range(GLA_HEADS)]
        qd = [q_dec[rb[b], ks[h]] for b, h in U]
        vh = [v[rb[b], vs[h]] for b, h in U]
        st = [st_ref[i] for i in nu]
        sc = [jnp.where(incl, _mm_nt(qd[i], k_inv[rb[b], ks[h]]), 0.0) for i, (b, h) in enumerate(U)]
        o_state = [_mm_nt(qd[i], st[i]) for i in nu]
        upd = [_mm_tn(vh[i], k_end[rb[b], ks[h]]) for i, (b, h) in enumerate(U)]
        o_loc = [_mm(sc[i], vh[i]) for i in nu]
        for i, (b, h) in enumerate(U):
            st_ref[i] = st[i] * sd[b][:, ks[h]] + upd[i]
            o = o_loc[i] + o_state[i]
            o = o * lax.rsqrt(jnp.mean(o * o, axis=-1, keepdims=True) + NORM_EPS)
            o_ref[b, rows, vs[h]] = o * ng_ref[:, vs[h]] * _silu(gt[rb[b], vs[h]])
        return carry

    lax.fori_loop(0, n_chunks, chunk, 0)


def gla_mixer(pa, w_alpha2, b_alpha, norm_g):
    b, t, _ = pa.shape
    tt = min(MIX_TILE, t)
    nb = min(GLA_BATCH_PER_STEP, b)
    hk, hv = GLA_HEADS * GLA_DK, GLA_HEADS * GLA_DV
    wa = jnp.zeros((LANES, hk), BF16).at[:GLA_RANK].set(w_alpha2.astype(BF16))
    return pl.pallas_call(
        _gla_body,
        out_shape=jax.ShapeDtypeStruct((b, t, hv), F32),
        grid=(b // nb, t // tt),
        in_specs=[pl.BlockSpec((nb, tt, GLA_COLS_PAD), lambda i, j: (i, j, 0)),
                  _const_spec((LANES, hk)), _const_spec((1, hk)), _const_spec((1, hv))],
        out_specs=pl.BlockSpec((nb, tt, hv), lambda i, j: (i, j, 0)),
        scratch_shapes=[pltpu.VMEM((nb * GLA_HEADS, GLA_DV, GLA_DK), F32)],
        compiler_params=_params(("parallel", "arbitrary")),
        name="gla_mixer",
    )(pa, wa, b_alpha.reshape(1, hk), norm_g.reshape(1, hv))


_RWKV_SB = ("a_dec", "r_dec", "b_inv", "k_inv", "b_end", "k_end", "v")


def _seg_sum(x, seg):
    hw = seg.shape[0]
    x1 = x.astype(BF16)
    x2 = (x - x1.astype(F32)).astype(BF16)
    parts = [piece[:, s * hw:(s + 1) * hw] for piece in (x1, x2) for s in range(2)]
    out = jnp.dot(jnp.concatenate(parts, axis=0), seg, preferred_element_type=F32)
    m = x.shape[0]
    halves = [out[s * m:(s + 1) * m] + out[(2 + s) * m:(3 + s) * m] for s in range(2)]
    return jnp.concatenate(halves, axis=1)


def _proj_ab_body(h_ref, g_ref, wa_ref, wb_ref, mu_ref, w0_ref, w2_ref, a0_ref, a2_ref, g2_ref, kk_ref, ka_ref,
                  rk_ref, seg_ref, pa_ref, sb_ref, sf_ref, gl_ref, prev_ref):
    L = RWKV_CHUNK
    W = RWKV_WIDTH
    tm = h_ref.shape[1]
    n_blocks = tm // L

    @pl.when(pl.program_id(1) == 0)
    def _():
        prev_ref[...] = jnp.zeros_like(prev_ref)

    ng = PROJ_ROW_GROUPS
    gm = tm // ng
    G = range(ng)
    rg = [slice(i * gm, (i + 1) * gm) for i in G]
    xn = [_rmsnorm(h_ref[0, rg[i], :], g_ref[...]).astype(BF16) for i in G]
    p = [jnp.dot(xn[i], wb_ref[...], preferred_element_type=F32) for i in G]

    n_a = wa_ref.shape[1]
    a_cols = [(0, 4 * LANES), (4 * LANES, 8 * LANES), (8 * LANES, n_a)]
    pending = [(i, c) for c in a_cols for i in G]

    def project_a(count):
        for _ in range(count):
            if pending:
                i, (lo_c, hi_c) = pending.pop(0)
                pa_ref[0, rg[i], lo_c:hi_c] = jnp.dot(xn[i], wa_ref[:, lo_c:hi_c], preferred_element_type=F32)

    first = lax.broadcasted_iota(jnp.int32, (SUBLANES, p[0].shape[1]), 0) == 0
    before = [prev_ref[0:1, :]] + [p[i][gm - 1:gm, :] for i in range(ng - 1)]
    rolled = [pltpu.roll(p[i], 1, 0) for i in G]
    sh = [jnp.concatenate([jnp.where(first, before[i], rolled[i][0:SUBLANES, :]), rolled[i][SUBLANES:gm, :]],
                          axis=0) for i in G]
    prev_ref[0:1, :] = p[ng - 1][gm - 1:gm, :]
    pf = [p[i] + (sh[i] - p[i]) * mu_ref[...] for i in G]
    r = [pf[i][:, 0:W] for i in G]
    k = [pf[i][:, W:2 * W] for i in G]
    v = [pf[i][:, 2 * W:3 * W] for i in G]
    wa = [pf[i][:, 3 * W:3 * W + LANES] for i in G]
    g_lr = [pf[i][:, 3 * W + LANES:3 * W + 2 * LANES] for i in G]
    log_w = [-RWKV_DECAY_SCALE * jax.nn.sigmoid(w0_ref[...] + _mm(jnp.tanh(wa[i]), w2_ref[...])) for i in G]
    a = [jax.nn.sigmoid(a0_ref[...] + _mm(wa[i], a2_ref[...])) for i in G]
    gate = [_mm(jax.nn.sigmoid(g_lr[i]), g2_ref[...]) for i in G]
    project_a(2)
    kk = [k[i] * kk_ref[...] for i in G]
    kk = [kk[i] * lax.rsqrt(_seg_sum(kk[i] * kk[i], seg_ref[...]) + RWKV_L2_EPS) for i in G]
    project_a(2)
    k = [k[i] * (1.0 + (a[i] - 1.0) * ka_ref[...]) for i in G]
    b_vec = [kk[i] * a[i] for i in G]
    bonus_sum = [_seg_sum(r[i] * k[i] * rk_ref[...], seg_ref[...]) for i in G]
    project_a(2)
    row_in_block = jnp.bitwise_and(lax.broadcasted_iota(jnp.int32, (gm, W), 0), L - 1)
    nbk = gm // L
    for i in G:
        g = log_w[i]
        d = 1
        while d < L:
            g = g + jnp.where(row_in_block >= d, pltpu.roll(g, d, 0), 0.0)
            d *= 2
        g_last = jnp.concatenate([jnp.broadcast_to(g[(j + 1) * L - 1:(j + 1) * L, :], (L, W)) for j in range(nbk)],
                                 axis=0)
        e_neg = jnp.exp(-g)
        e_end = jnp.exp(g_last - g)
        vals = dict(a_dec=-kk[i] * jnp.exp(g - log_w[i]), r_dec=r[i] * jnp.exp(g), b_inv=b_vec[i] * e_neg,
                    k_inv=k[i] * e_neg, b_end=b_vec[i] * e_end, k_end=k[i] * e_end, v=v[i])
        for j, name in enumerate(_RWKV_SB):
            sb_ref[0, rg[i], j * W:(j + 1) * W] = vals[name].astype(BF16)
        sf_ref[0, rg[i], 0:W] = gate[i]
        sf_ref[0, rg[i], W:2 * W] = bonus_sum[i] * v[i]
        for j in range(nbk):
            row0 = (i * nbk + j) * SUBLANES
            gl_ref[0, row0:row0 + SUBLANES, :] = g[(j + 1) * L - SUBLANES:(j + 1) * L, :]
        project_a(1)
    project_a(len(pending))


def proj_ab(h, g, wa, wb, mu, w0, w2, a0, a2, g2, k_k, k_a, r_k):
    b, t, d = h.shape
    tm = min(ROW_TILE, t)
    W = RWKV_WIDTH
    row = lambda x: x.reshape(1, -1)
    w2p = jnp.zeros((LANES, W), BF16).at[:RWKV_LORA_W].set(w2.astype(BF16))
    a2p = jnp.zeros((LANES, W), BF16).at[RWKV_LORA_W:].set(a2.astype(BF16))
    head_of = jnp.arange(W // 2) // RWKV_N
    seg = (head_of[:, None] == head_of[None, :]).astype(BF16)
    consts = [row(g), wa, wb, row(mu), row(w0), w2p, row(a0), a2p, g2.astype(BF16), row(k_k), row(k_a), row(r_k),
              seg]
    nsb = len(_RWKV_SB) * W
    tok = lambda w: pl.BlockSpec((1, tm, w), lambda i, j: (i, j, 0))
    return pl.pallas_call(
        _proj_ab_body,
        out_shape=[jax.ShapeDtypeStruct((b, t, wa.shape[1]), F32), jax.ShapeDtypeStruct((b, t, nsb), BF16),
                   jax.ShapeDtypeStruct((b, t, 2 * W), F32),
                   jax.ShapeDtypeStruct((b, t // RWKV_CHUNK * SUBLANES, W), F32)],
        grid=(b, t // tm),
        in_specs=[tok(d)] + [_const_spec(x.shape) for x in consts],
        out_specs=[tok(wa.shape[1]), tok(nsb), tok(2 * W),
                   pl.BlockSpec((1, tm // RWKV_CHUNK * SUBLANES, W), lambda i, j: (i, j, 0))],
        scratch_shapes=[pltpu.VMEM((SUBLANES, wb.shape[1]), F32)],
        compiler_params=_params(("parallel", "arbitrary")),
        name="proj_ab",
    )(h, *consts)


def _rwkv_rec_body(sb_ref, sf_ref, gl_ref, lng_ref, lnb_ref, seg_ref, tri_ref, quad_ref, o_ref, sp_ref):
    L = RWKV_CHUNK
    W = RWKV_WIDTH
    N = RWKV_N
    nb = sb_ref.shape[0]
    n_chunks = sb_ref.shape[1] // L
    n_pairs = RWKV_HEADS // 2
    idx = {name: j for j, name in enumerate(_RWKV_SB)}

    @pl.when(pl.program_id(1) == 0)
    def _():
        sp_ref[...] = jnp.zeros_like(sp_ref)

    def chunk(c, carry):
        lane = lax.broadcasted_iota(jnp.int32, (L, LANES), 1)
        lo = lane < N
        hi = lane >= N
        nat = (lo, hi)
        strict_f = tri_ref[0]
        incl_f = tri_ref[1]
        eye2 = tri_ref[2]
        diag_f = quad_ref[0]
        anti_f = quad_ref[1]
        zb = jnp.zeros((L, LANES), BF16)
        sel = lambda m, x: jnp.where(m, x, jnp.zeros_like(x))
        swap = lambda x: pltpu.roll(x, N, 1)
        cat0 = lambda xs: jnp.concatenate(xs, axis=0)
        cat1 = lambda xs: jnp.concatenate(xs, axis=1)
        dot = lambda x, y: jnp.dot(x, y, preferred_element_type=F32)
        nt = lambda x, y: lax.dot_general(x, y, (((1,), (1,)), ((), ())), preferred_element_type=F32)
        rows = pl.ds(pl.multiple_of(c * L, L), L)
        U = [(b, p) for b in range(nb) for p in range(n_pairs)]
        nu = range(len(U))

        def blk(name, u):
            col = idx[name] * W + u[1] * LANES
            return sb_ref[u[0], rows, col:col + LANES]

        vn = lambda i, h: sel(nat[h], blk("v", U[i]))
        ad = [blk("a_dec", u) for u in U]
        rd = [blk("r_dec", u) for u in U]
        b_inv = [blk("b_inv", u) for u in U]
        k_inv = [blk("k_inv", u) for u in U]
        pe = [nt(cat0([sel(lo, ad[i]), sel(lo, rd[i])]), cat0([b_inv[i], k_inv[i]])) for i in nu]
        po = [nt(cat0([sel(hi, ad[i]), sel(hi, rd[i])]), cat0([k_inv[i], b_inv[i]])) for i in nu]
        aa = [[pe[i][0:L, :] * strict_f, po[i][0:L, :] * strict_f] for i in nu]
        ar = [[(pe[i][L:2 * L, :] * incl_f).astype(BF16), (po[i][L:2 * L, :] * incl_f).astype(BF16)] for i in nu]
        t0 = [dot(jnp.where(lo, aa[i][1], aa[i][0]).astype(BF16),
                  cat0([cat1([vn(i, 1), zb]), cat1([zb, vn(i, 0)])])) for i in nu]
        ad_sw = [swap(x) for x in ad]
        x0 = [[jnp.where(hi, ad_sw[i], t0[i][:, LANES:2 * LANES].astype(BF16)),
               jnp.where(lo, ad_sw[i], t0[i][:, 0:LANES].astype(BF16))] for i in nu]
        q0 = [jnp.where(lo, aa[i][0], eye2) for i in nu]
        q1 = [jnp.where(lo, eye2, aa[i][1]) for i in nu]
        for _ in range(6):
            out = [dot(jnp.where(lo, q0[i], q1[i]).astype(BF16),
                       cat0([cat1([q0[i].astype(BF16), zb]), cat1([zb, q1[i].astype(BF16)])])) for i in nu]
            q0 = [out[i][:, 0:LANES] + sel(hi, q0[i]) for i in nu]
            q1 = [out[i][:, LANES:2 * LANES] + sel(lo, q1[i]) for i in nu]
        tx = [dot(jnp.where(lo, q1[i], q0[i]).astype(BF16),
                  cat0([cat1([x0[i][1], zb]), cat1([zb, x0[i][0]])])) for i in nu]
        xb = [[tx[i][:, LANES:2 * LANES].astype(BF16), tx[i][:, 0:LANES].astype(BF16)] for i in nu]
        rd_sw = [swap(blk("r_dec", u)).astype(F32) for u in U]
        rmy = [[sel(hi, rd_sw[i]) + dot(ar[i][0], cat0([xb[i][0], vn(i, 0)])),
                sel(lo, rd_sw[i]) + dot(ar[i][1], cat0([vn(i, 1), xb[i][1]]))] for i in nu]
        be_sw = [swap(blk("b_end", u)) for u in U]
        ke_sw = [swap(blk("k_end", u)) for u in U]
        gg = [lax.dot_general(cat0([xb[i][0], xb[i][1], vn(i, 0), vn(i, 1)]),
                              cat0([sel(hi, be_sw[i]), sel(lo, be_sw[i]), sel(hi, ke_sw[i]), sel(lo, ke_sw[i])]),
                              (((0,), (0,)), ((), ())), preferred_element_type=F32) for i in nu]
        sp = [sp_ref[i] for i in nu]
        ys = [_mm_nt(jnp.where(lo, rmy[i][1], rmy[i][0]), sp[i]) + jnp.where(lo, rmy[i][0], rmy[i][1]) for i in nu]
        g_rows = pl.ds(pl.multiple_of(c * SUBLANES, SUBLANES), SUBLANES)
        for i, u in enumerate(U):
            sd_sw = jnp.exp(swap(gl_ref[u[0], g_rows, u[1] * LANES:(u[1] + 1) * LANES])[SUBLANES - 1:SUBLANES, :])
            sp_ref[i] = sp[i] * sd_sw + _mm(sp[i], gg[i] * diag_f) + gg[i] * anti_f
        y = cat0([cat1(ys[b * n_pairs:(b + 1) * n_pairs]) for b in range(nb)])
        mean = _seg_sum(y, seg_ref[...]) * (1.0 / N)
        dy = y - mean
        var = _seg_sum(dy * dy, seg_ref[...]) * (1.0 / N)
        y = dy * lax.rsqrt(var + RWKV_GN_EPS) * lng_ref[...] + lnb_ref[...]
        for b in range(nb):
            yb = y[b * L:(b + 1) * L, :]
            o_ref[b, rows, :] = (yb + sf_ref[b, rows, W:2 * W]) * sf_ref[b, rows, 0:W]
        return carry

    lax.fori_loop(0, n_chunks, chunk, 0)


def rwkv_recurrence(sb, sf, gl, ln_g, ln_b):
    b, t, _ = sb.shape
    tt = min(MIX_TILE, t)
    nb = min(RWKV_BATCH_PER_STEP, b)
    W = RWKV_WIDTH
    row = lambda x: x.reshape(1, -1)
    head_of = jnp.arange(W // 2) // RWKV_N
    seg = (head_of[:, None] == head_of[None, :]).astype(BF16)
    t_idx = jnp.arange(RWKV_CHUNK)[:, None]
    s_idx = jnp.arange(LANES)[None, :] % RWKV_N
    tri = jnp.stack([t_idx > s_idx, t_idx >= s_idx, t_idx == s_idx]).astype(F32)
    half = jnp.arange(LANES) // RWKV_N
    same_half = half[:, None] == half[None, :]
    quad = jnp.stack([same_half, ~same_half]).astype(F32)
    consts = [row(ln_g), row(ln_b), seg, tri, quad]
    tok = lambda w: pl.BlockSpec((nb, tt, w), lambda i, j: (i, j, 0))
    return pl.pallas_call(
        _rwkv_rec_body,
        out_shape=jax.ShapeDtypeStruct((b, t, W), F32),
        grid=(b // nb, t // tt),
        in_specs=[tok(sb.shape[2]), tok(sf.shape[2]),
                  pl.BlockSpec((nb, tt // RWKV_CHUNK * SUBLANES, W), lambda i, j: (i, j, 0))]
        + [_const_spec(x.shape) for x in consts],
        out_specs=tok(W),
        scratch_shapes=[pltpu.VMEM((nb * RWKV_HEADS // 2, LANES, LANES), F32)],
        compiler_params=_params(("parallel", "arbitrary")),
        name="rwkv_mixer",
    )(sb, sf, gl, *consts)


def _proj_cd_body(h_ref, g_ref, wc_ref, wd_ref, lcw_ref, lcb_ref, gw_ref, gb_ref, lam_ref, mcw_ref, mcb_ref,
                  wq_ref, wk_ref, wv_ref, bif_ref, lr_ref, md_ref, lwin_ref, mwin_ref):
    tm = h_ref.shape[1]
    W = LRU_WIDTH

    @pl.when(pl.program_id(1) == 0)
    def _():
        lwin_ref[0:SUBLANES, :] = jnp.zeros((SUBLANES, W), F32)
        mwin_ref[0:SUBLANES, :] = jnp.zeros((SUBLANES, W), F32)

    ng = PROJ_ROW_GROUPS
    gm = tm // ng
    G = range(ng)
    rg = [slice(i * gm, (i + 1) * gm) for i in G]
    xn = [_rmsnorm(h_ref[0, rg[i], :], g_ref[...]).astype(BF16) for i in G]
    pc = [jnp.dot(xn[i], wc_ref[...], preferred_element_type=F32) for i in G]
    pd = [jnp.dot(xn[i], wd_ref[...], preferred_element_type=F32) for i in G]
    half = W // 2
    for i in G:
        x = pc[i][:, 0:W]
        xc = _causal_conv(x, lwin_ref, lcw_ref[...], lcb_ref[...], LRU_CONV)
        xcb = xc.astype(BF16)
        pre = []
        for gi in range(2):
            pre.append(jnp.concatenate(
                [jnp.dot(xcb[:, n * LRU_BLOCK:(n + 1) * LRU_BLOCK], gw_ref[gi, n], preferred_element_type=F32)
                 for n in range(LRU_BLOCKS)], axis=1) + gb_ref[gi:gi + 1, :])
        r_gate = jax.nn.sigmoid(pre[0])
        i_gate = jax.nn.sigmoid(pre[1])
        log_a = -LRU_C * r_gate * _softplus(-lam_ref[...])
        a = jnp.exp(log_a)
        w2 = -jnp.tanh(log_a) * (a * a + 1.0)
        lr_ref[0, rg[i], 0:W] = a
        lr_ref[0, rg[i], W:2 * W] = jnp.where(w2 > 0.0, w2 * lax.rsqrt(w2), 0.0) * (i_gate * xc)
        lr_ref[0, rg[i], 2 * W:3 * W] = jax.nn.gelu(pc[i][:, W:2 * W])
        mx = pd[i][:, 0:W]
        mxc = _silu(_causal_conv(mx, mwin_ref, mcw_ref[...], mcb_ref[...], MLSTM_CONV)).astype(BF16)
        mxb = mx.astype(BF16)
        for s in range(2):
            cs = slice(s * half, (s + 1) * half)
            md_ref[0, rg[i], s * half:(s + 1) * half] = jnp.dot(mxc[:, cs], wq_ref[s], preferred_element_type=F32)
            md_ref[0, rg[i], W + s * half:W + (s + 1) * half] = (
                jnp.dot(mxc[:, cs], wk_ref[s], preferred_element_type=F32) * MLSTM_DH ** -0.5)
            md_ref[0, rg[i], 2 * W + s * half:2 * W + (s + 1) * half] = jnp.dot(
                mxb[:, cs], wv_ref[s], preferred_element_type=F32)
        md_ref[0, rg[i], 3 * W:4 * W] = jax.nn.sigmoid(pd[i][:, W:2 * W])
        md_ref[0, rg[i], 4 * W:4 * W + LANES] = pd[i][:, 2 * W:2 * W + LANES] + bif_ref[...]


def mlstm_qkv_tiles(qkv_w):
    *lead, three, nblk, d, e = qkv_w.shape
    return _block_diag(qkv_w.reshape(*lead, three, 2, nblk // 2, d, e)).astype(BF16)


def proj_cd(h, g, wc, wd, lru_conv_w, lru_conv_b, gate_w, gate_b, lam, m_conv_w, m_conv_b, qkv_tiles, b_if):
    b, t, d = h.shape
    tm = min(ROW_TILE, t)
    W = LRU_WIDTH
    row = lambda x: x.reshape(1, -1)
    pad_taps = lambda w: jnp.zeros((SUBLANES, W), F32).at[:w.shape[0]].set(w)
    bif = jnp.zeros((1, LANES), F32).at[0, :2 * MLSTM_HEADS].set(b_if)
    consts = [row(g), wc, wd, pad_taps(lru_conv_w), row(lru_conv_b), gate_w.astype(BF16), gate_b, row(lam),
              pad_taps(m_conv_w), row(m_conv_b)] + [qkv_tiles[i] for i in range(3)] + [bif]
    gm = tm // PROJ_ROW_GROUPS
    tok = lambda w: pl.BlockSpec((1, tm, w), lambda i, j: (i, j, 0))
    return pl.pallas_call(
        _proj_cd_body,
        out_shape=[jax.ShapeDtypeStruct((b, t, 3 * W), F32), jax.ShapeDtypeStruct((b, t, MLSTM_IN_COLS), F32)],
        grid=(b, t // tm),
        in_specs=[tok(d)] + [_const_spec(x.shape) for x in consts],
        out_specs=[tok(3 * W), tok(MLSTM_IN_COLS)],
        scratch_shapes=[pltpu.VMEM((SUBLANES + gm, W), F32), pltpu.VMEM((SUBLANES + gm, W), F32)],
        compiler_params=_params(("parallel", "arbitrary")),
        name="proj_cd",
    )(h, *consts)


def _lru_scan_body(p_ref, o_ref, hprev_ref):
    tt = p_ref.shape[1]
    W = LRU_WIDTH

    @pl.when(pl.program_id(1) == 0)
    def _():
        hprev_ref[...] = jnp.zeros_like(hprev_ref)

    a = p_ref[0, :, 0:W]
    u = p_ref[0, :, W:2 * W]
    row_in_group = jnp.bitwise_and(lax.broadcasted_iota(jnp.int32, (tt, W), 0), SUBLANES - 1)
    d = 1
    while d < SUBLANES:
        keep = row_in_group >= d
        u = u + a * jnp.where(keep, pltpu.roll(u, d, 0), 0.0)
        a = a * jnp.where(keep, pltpu.roll(a, d, 0), 1.0)
        d *= 2
    carry = hprev_ref[0:1, :]
    for g in range(tt // SUBLANES):
        rs = slice(g * SUBLANES, (g + 1) * SUBLANES)
        hg = u[rs, :] + a[rs, :] * carry
        carry = hg[SUBLANES - 1:SUBLANES, :]
        o_ref[0, rs, :] = hg * p_ref[0, rs, 2 * W:3 * W]
    hprev_ref[0:1, :] = carry


def lru_scan(lr):
    b, t, cols = lr.shape
    tt = min(ROW_TILE, t)
    W = LRU_WIDTH
    return pl.pallas_call(
        _lru_scan_body,
        out_shape=jax.ShapeDtypeStruct((b, t, W), F32),
        grid=(b, t // tt),
        in_specs=[pl.BlockSpec((1, tt, cols), lambda i, j: (i, j, 0))],
        out_specs=pl.BlockSpec((1, tt, W), lambda i, j: (i, j, 0)),
        scratch_shapes=[pltpu.VMEM((SUBLANES, W), F32)],
        compiler_params=_params(("parallel", "arbitrary")),
        name="rglru_mixer",
    )(lr)


def _mlstm_body(p_ref, ng_ref, o_ref, c_ref, m_ref):
    tt = p_ref.shape[1]
    L = MLSTM_CHUNK
    W = MLSTM_WIDTH
    DH = MLSTM_DH
    H = MLSTM_HEADS
    n_chunks = tt // L
    nb = p_ref.shape[0]

    @pl.when(pl.program_id(1) == 0)
    def _():
        c_ref[...] = jnp.zeros_like(c_ref)
        m_ref[...] = jnp.zeros_like(m_ref)

    def chunk(c, carry):
        incl, _ = _tri_masks(L)
        tri01 = jnp.where(incl, 1.0, 0.0).astype(BF16)
        t0 = pl.multiple_of(c * L, L)
        rows = pl.ds(t0, L)
        nt = lambda y: lax.dot_general(y, tri01, (((1,), (1,)), ((), ())), preferred_element_type=F32)
        gates, lsg, g_t, b_rows = [], [], [], []
        for b in range(nb):
            gates.append(p_ref[b, rows, 4 * W:4 * W + LANES])
            lsg.append(_log_sigmoid(gates[b]))
            g_t.append(gates[b].T[0:SUBLANES, :])
            x1, x2, x3 = _split3(_log_sigmoid(g_t[b]))
            b_rows.append(nt(x1) + nt(x2) + nt(x3))
        ones_b = jnp.ones((L, LANES), BF16)
        U = [(b, h) for b in range(nb) for h in range(H)]
        nu = range(len(U))
        hs = [slice(h * DH, (h + 1) * DH) for h in range(H)]
        col = lambda j, h: slice(j * W + h * DH, j * W + (h + 1) * DH)
        qh = [p_ref[b, rows, col(0, h)] for b, h in U]
        kh = [p_ref[b, rows, col(1, h)] for b, h in U]
        v_aug = [jnp.concatenate([p_ref[b, rows, col(2, h)].astype(BF16), ones_b], axis=1) for b, h in U]
        cm = [c_ref[i] for i in nu]
        m = [m_ref[i, 0:1, :] for i in nu]
        i_rep = [jnp.broadcast_to(gates[b][:, h:h + 1], (L, LANES)) for b, h in U]
        lf_rep = [jnp.broadcast_to(lsg[b][:, H + h:H + h + 1], (L, LANES)) for b, h in U]
        b_rep = [_mm_exact_lhs01(tri01, lf_rep[i]) for i in nu]
        qk = [_mm_nt(qh[i], kh[i]) for i in nu]
        b_last = [b_rep[i][L - 1:L, :] for i in nu]
        log_e = [b_last[i] - b_rep[i] + i_rep[i] for i in nu]
        m_end = [jnp.max(log_e[i], axis=0, keepdims=True) for i in nu]
        m_new = [jnp.maximum(b_last[i] + m[i], m_end[i]) for i in nu]
        kin = [kh[i] * (jnp.exp(m_end[i] - m_new[i]) * jnp.exp(log_e[i] - m_end[i])) for i in nu]
        c_upd = [_mm_tn(kin[i], v_aug[i]) for i in nu]
        run = [i_rep[i] - b_rep[i] for i in nu]
        d = 1
        while d < L:
            run = [jnp.maximum(run[i], _shift_rows(run[i], d, -jnp.inf)) for i in nu]
            d *= 2
        m_t = [jnp.maximum(b_rep[i] + m[i], b_rep[i] + run[i]) for i in nu]
        log_d = [jnp.where(incl, b_rep[i][:, 0:L] - b_rows[b][H + h:H + h + 1, :] + g_t[b][h:h + 1, :], -jnp.inf)
                 for i, (b, h) in enumerate(U)]
        w_loc = [jnp.exp(log_d[i] - m_t[i][:, 0:L]) * qk[i] for i in nu]
        lhs = [jnp.concatenate([(jnp.exp(b_rep[i] + m[i] - m_t[i]) * qh[i]).astype(BF16), w_loc[i].astype(BF16)],
                               axis=1) for i in nu]
        nd = [jnp.dot(lhs[i], jnp.concatenate([cm[i].astype(BF16), v_aug[i]], axis=0),
                      preferred_element_type=F32) for i in nu]
        for i, (b, h) in enumerate(U):
            c_state = jnp.exp(b_last[i] + m[i] - m_new[i])
            c_ref[i] = jnp.concatenate([c_state, c_state], axis=1) * cm[i] + c_upd[i]
            m_ref[i] = jnp.broadcast_to(m_new[i], (SUBLANES, LANES))
            hv = nd[i][:, 0:DH] / jnp.maximum(jnp.abs(nd[i][:, DH:2 * DH]), jnp.exp(-m_t[i]))
            hv = hv * lax.rsqrt(jnp.mean(hv * hv, axis=-1, keepdims=True) + NORM_EPS)
            o_ref[b, rows, hs[h]] = p_ref[b, rows, col(3, h)] * hv * ng_ref[:, hs[h]]
        return carry

    lax.fori_loop(0, n_chunks, chunk, 0)


def _block_diag(w):
    *lead, nb, d, e = w.shape
    rows = jnp.tile(w.reshape(*lead, nb * d, e), (1,) * len(lead) + (1, nb))
    same_block = (jnp.arange(nb * d)[:, None] // d) == (jnp.arange(nb * e)[None, :] // e)
    return jnp.where(same_block, rows, 0.0)


def mlstm_mixer(md, norm_g):
    b, t, cols = md.shape
    tt = min(MIX_TILE, t)
    nb = min(MLSTM_BATCH_PER_STEP, b)
    W = MLSTM_WIDTH
    return pl.pallas_call(
        _mlstm_body,
        out_shape=jax.ShapeDtypeStruct((b, t, W), F32),
        grid=(b // nb, t // tt),
        in_specs=[pl.BlockSpec((nb, tt, cols), lambda i, j: (i, j, 0)), _const_spec((1, W))],
        out_specs=pl.BlockSpec((nb, tt, W), lambda i, j: (i, j, 0)),
        scratch_shapes=[pltpu.VMEM((nb * MLSTM_HEADS, MLSTM_DH, 2 * MLSTM_DH), F32),
                        pltpu.VMEM((nb * MLSTM_HEADS, SUBLANES, LANES), F32)],
        compiler_params=_params(("parallel", "arbitrary")),
        name="mlstm_mixer",
    )(md, norm_g.reshape(1, W))


def _mix_xattn_body(h_ref, ya_ref, yb_ref, wm_ref, g_ref, wq_ref, k_ref, v_ref, wo_ref, o_ref):
    tm = h_ref.shape[1]
    ng = XATTN_ROW_GROUPS
    rg = [slice(i * (tm // ng), (i + 1) * (tm // ng)) for i in range(ng)]
    G = range(ng)
    HD = range(XA_HEADS)
    hs = [slice(hd * XA_DH, (hd + 1) * XA_DH) for hd in HD]
    mixed = [jnp.concatenate([ya_ref[0, rg[i], :], yb_ref[0, rg[i], :]], axis=1).astype(BF16) for i in G]
    h = [h_ref[0, rg[i], :] + jnp.dot(mixed[i], wm_ref[...], preferred_element_type=F32) for i in G]
    xn = [_rmsnorm(h[i], g_ref[...]).astype(BF16) for i in G]
    q = [jnp.dot(xn[i], wq_ref[...], preferred_element_type=F32) for i in G]
    s = [[_mm_nt(q[i][:, hs[hd]], k_ref[0, :, hs[hd]]) * XA_DH ** -0.5 for hd in HD] for i in G]
    e = [[jnp.exp(s[i][hd] - jnp.max(s[i][hd], axis=-1, keepdims=True)) for hd in HD] for i in G]
    p = [[e[i][hd] / jnp.sum(e[i][hd], axis=-1, keepdims=True) for hd in HD] for i in G]
    pv = [[_mm(p[i][hd], v_ref[0, :, hs[hd]]) for hd in HD] for i in G]
    for i in G:
        o = jnp.concatenate(pv[i], axis=1).astype(BF16)
        o_ref[0, rg[i], :] = h[i] + jnp.dot(o, wo_ref[...], preferred_element_type=F32)


def mix_xattn(h, ya, yb, w_mix, g, wq, kv, wo):
    b, t, d = h.shape
    tm = min(XATTN_TILE, t)
    tok = lambda w: pl.BlockSpec((1, tm, w), lambda i, j: (i, j, 0))
    mem_k = pl.BlockSpec((1, MEM_LEN, d), lambda i, j: (i, 0, 0))
    mem_v = pl.BlockSpec((1, MEM_LEN, d), lambda i, j: (i, 0, 1))
    return pl.pallas_call(
        _mix_xattn_body,
        out_shape=jax.ShapeDtypeStruct((b, t, d), F32),
        grid=(b, t // tm),
        in_specs=[tok(d), tok(ya.shape[-1]), tok(yb.shape[-1]), _const_spec(w_mix.shape), _const_spec((1, d)),
                  _const_spec(wq.shape), mem_k, mem_v, _const_spec(wo.shape)],
        out_specs=tok(d),
        compiler_params=_params(("parallel", "parallel")),
        name="mix_xattn",
    )(h, ya, yb, w_mix, g.reshape(1, d), wq, kv, kv, wo)


def _ffn_body(final_norm, h_ref, g_ref, wu_ref, wg_ref, cw_ref, cb_ref, wd_ref, fg_ref, o_ref, uprev_ref):
    tm = h_ref.shape[1]

    @pl.when(pl.program_id(1) == 0)
    def _():
        uprev_ref[...] = jnp.zeros_like(uprev_ref)

    h = h_ref[0]
    xn = _rmsnorm(h, g_ref[...]).astype(BF16)
    acc = h
    start = 0
    for width in FFN_COL_GROUPS:
        cs = slice(start, start + width)
        start += width
        u = jnp.dot(xn, wu_ref[:, cs], preferred_element_type=F32)
        gt = jnp.dot(xn, wg_ref[:, cs], preferred_element_type=F32)
        uu = jnp.concatenate([uprev_ref[:, cs], u], axis=0)
        uprev_ref[:, cs] = u[tm - SUBLANES:tm, :]
        c = cb_ref[:, cs] + cw_ref[2:3, cs] * u
        c = c + cw_ref[1:2, cs] * pltpu.roll(uu, 1, 0)[SUBLANES:SUBLANES + tm, :]
        c = c + cw_ref[0:1, cs] * pltpu.roll(uu, 2, 0)[SUBLANES:SUBLANES + tm, :]
        act = (_silu(c) * gt).astype(BF16)
        acc = acc + jnp.dot(act, wd_ref[cs, :], preferred_element_type=F32)
    if final_norm:
        acc = _rmsnorm(acc, fg_ref[...])
    o_ref[0] = acc


def ffn(h, g, wu, wg, conv_w, conv_b, wd, final_g, final_norm):
    b, t, d = h.shape
    tm = min(ROW_TILE, t)
    tok = pl.BlockSpec((1, tm, d), lambda i, j: (i, j, 0))
    consts = [g.reshape(1, d), wu, wg, conv_w, conv_b, wd, final_g.reshape(1, d)]
    return pl.pallas_call(
        functools.partial(_ffn_body, final_norm),
        out_shape=jax.ShapeDtypeStruct((b, t, d), F32),
        grid=(b, t // tm),
        in_specs=[tok] + [_const_spec(x.shape) for x in consts],
        out_specs=tok,
        scratch_shapes=[pltpu.VMEM((SUBLANES, D_FF_PAD), F32)],
        compiler_params=_params(("parallel", "arbitrary")),
        name="ffn",
    )(h, *consts)


def _pad_cols(w, n):
    return jnp.pad(w, ((0, 0), (0, n - w.shape[1])))


def _rwkv_perm(x):
    W = RWKV_WIDTH
    o_w, o_k, o_v = W, W + RWKV_LORA_W, 2 * W + RWKV_LORA_W
    o_a = 3 * W + RWKV_LORA_W
    o_g = o_a + RWKV_LORA_A
    return jnp.concatenate([x[..., :W], x[..., o_k:o_k + W], x[..., o_v:o_v + W], x[..., o_w:o_w + RWKV_LORA_W],
                            x[..., o_a:o_a + RWKV_LORA_A], x[..., o_g:]], axis=-1)


def kernel(x, mem, mem_norm_g, norm_mix_g, ab_w_in, gla_w_alpha2, gla_b_alpha, gla_norm_g, rwkv_mu, rwkv_w0, rwkv_w2, rwkv_a0, rwkv_a2, rwkv_g2, rwkv_k_k, rwkv_k_a, rwkv_r_k, rwkv_ln_g, rwkv_ln_b, cd_w_in, lru_conv_w, lru_conv_b, lru_gate_w, lru_gate_b, lru_lambda, mlstm_conv_w, mlstm_conv_b, mlstm_qkv_w, mlstm_b_if, mlstm_norm_g, w_mix_out, norm_xattn_g, xattn_wq, xattn_wkv, xattn_wo, norm_ffn_g, ffn_w_up, ffn_conv_w, ffn_conv_b, ffn_w_down, final_norm_g):
    b, t, d = x.shape
    depth = norm_mix_g.shape[0]
    n = b * t
    h = x
    mem2d = mem.reshape(b * MEM_LEN, d)
    qkv_tiles = mlstm_qkv_tiles(mlstm_qkv_w)
    wkv = xattn_wkv.astype(BF16)
    kvs = norm_matmul(mem2d, mem_norm_g, [wkv[layer] for layer in range(depth)], out_dtype=BF16, name="proj_kv")
    for layer in range(depth):
        j = layer // 2
        h2d = h.reshape(n, d)
        if layer % 2 == 0:
            w = ab_w_in[j]
            wa = _pad_cols(w[:, :GLA_COLS], GLA_COLS_PAD).astype(BF16)
            wb = _rwkv_perm(w[:, GLA_COLS:]).astype(BF16)
            pa, sb, sf, gl = proj_ab(h, norm_mix_g[layer], wa, wb, _rwkv_perm(rwkv_mu[j]), rwkv_w0[j], rwkv_w2[j],
                                     rwkv_a0[j], rwkv_a2[j], rwkv_g2[j], rwkv_k_k[j], rwkv_k_a[j], rwkv_r_k[j])
            ya = gla_mixer(pa, gla_w_alpha2[j], gla_b_alpha[j], gla_norm_g[j])
            yb = rwkv_recurrence(sb, sf, gl, rwkv_ln_g[j], rwkv_ln_b[j])
        else:
            w = cd_w_in[j]
            wc = w[:, :2 * LRU_WIDTH].astype(BF16)
            wd_ = _pad_cols(w[:, 2 * LRU_WIDTH:], MLSTM_COLS_PAD).astype(BF16)
            lr, md = proj_cd(h, norm_mix_g[layer], wc, wd_, lru_conv_w[j], lru_conv_b[j], lru_gate_w[j],
                             lru_gate_b[j], lru_lambda[j], mlstm_conv_w[j], mlstm_conv_b[j], qkv_tiles[j],
                             mlstm_b_if[j])
            ya = lru_scan(lr)
            yb = mlstm_mixer(md, mlstm_norm_g[j])
        kv = kvs[layer].reshape(b, MEM_LEN, 2 * d)
        h = mix_xattn(h, ya, yb, w_mix_out[layer].astype(BF16), norm_xattn_g[layer], xattn_wq[layer].astype(BF16),
                      kv, xattn_wo[layer].astype(BF16))
        wup = ffn_w_up[layer]
        wu = _pad_cols(wup[:, :D_FF], D_FF_PAD).astype(BF16)
        wg = _pad_cols(wup[:, D_FF:], D_FF_PAD).astype(BF16)
        cw = _pad_cols(jnp.pad(ffn_conv_w[layer], ((0, SUBLANES - FFN_CONV), (0, 0))), D_FF_PAD)
        cb = _pad_cols(ffn_conv_b[layer].reshape(1, D_FF), D_FF_PAD)
        wdn = jnp.pad(ffn_w_down[layer], ((0, D_FF_PAD - D_FF), (0, 0))).astype(BF16)
        h = ffn(h, norm_ffn_g[layer], wu, wg, cw, cb, wdn, final_norm_g, layer == depth - 1)
    return h
```

```python
import functools

import jax
import jax.numpy as jnp
from jax import lax
from jax.experimental import pallas as pl
from jax.experimental.pallas import tpu as pltpu

F32 = jnp.float32
BF16 = jnp.bfloat16

NORM_EPS = 1e-6
LANES = 128
SUBLANES = 8
VMEM_LIMIT_BYTES = 56 * 1024 * 1024

GLA_HEADS, GLA_DK, GLA_DV, GLA_RANK, GLA_TAU, GLA_CHUNK = 4, 64, 128, 16, 16.0, 64
GLA_COLS = 2 * GLA_HEADS * GLA_DK + 2 * GLA_HEADS * GLA_DV + GLA_RANK
GLA_COLS_PAD = 13 * LANES

RWKV_HEADS, RWKV_N, RWKV_WIDTH = 8, 64, 512
RWKV_LORA_W, RWKV_LORA_A = 64, 64
RWKV_CHUNK = 64
RWKV_DECAY_SCALE = 0.6065306597126334
RWKV_GN_EPS = RWKV_N * 1e-5
RWKV_L2_EPS = 1e-12

LRU_WIDTH, LRU_BLOCKS, LRU_BLOCK, LRU_C, LRU_CONV = 512, 4, 128, 8.0, 4
MLSTM_HEADS, MLSTM_DH, MLSTM_WIDTH, MLSTM_CONV, MLSTM_CHUNK = 4, 128, 512, 4, 64
MLSTM_COLS_PAD = 2 * MLSTM_WIDTH + LANES
MLSTM_IN_COLS = 4 * MLSTM_WIDTH + LANES

XA_HEADS, XA_DH, MEM_LEN = 4, 256, 256
D_FF, FFN_CONV = 2752, 3
D_FF_PAD = 22 * LANES
MXU_K_TILE = 256
FFN_COL_GROUPS = (6 * MXU_K_TILE, 5 * MXU_K_TILE)

MIX_TILE = 256
GLA_BATCH_PER_STEP = 4
MLSTM_BATCH_PER_STEP = 4
RWKV_BATCH_PER_STEP = 4
ROW_TILE = 512
XATTN_TILE = 1024
XATTN_ROW_GROUPS = 2
PROJ_ROW_GROUPS = 2


def _mm(a, b):
    return jnp.dot(a.astype(BF16), b.astype(BF16), preferred_element_type=F32)


def _mm_nt(a, b):
    return lax.dot_general(a.astype(BF16), b.astype(BF16), (((1,), (1,)), ((), ())), preferred_element_type=F32)


def _mm_tn(a, b):
    return lax.dot_general(a.astype(BF16), b.astype(BF16), (((0,), (0,)), ((), ())), preferred_element_type=F32)


def _split3(x):
    x1 = x.astype(BF16)
    r1 = x - x1.astype(F32)
    x2 = r1.astype(BF16)
    x3 = (r1 - x2.astype(F32)).astype(BF16)
    return x1, x2, x3


def _mm_exact_lhs01(m01, x):
    x1, x2, x3 = _split3(x)
    d = lambda y: jnp.dot(m01, y, preferred_element_type=F32)
    return d(x1) + d(x2) + d(x3)


def _rmsnorm(x, g):
    return x * lax.rsqrt(jnp.mean(x * x, axis=-1, keepdims=True) + NORM_EPS) * g


def _log_sigmoid(x):
    return jnp.minimum(x, 0.0) - jnp.log1p(jnp.exp(-jnp.abs(x)))


def _softplus(x):
    return jnp.maximum(x, 0.0) + jnp.log1p(jnp.exp(-jnp.abs(x)))


def _silu(x):
    return x * jax.nn.sigmoid(x)


def _tri_masks(n):
    r = lax.broadcasted_iota(jnp.int32, (n, n), 0)
    c = lax.broadcasted_iota(jnp.int32, (n, n), 1)
    return r >= c, r > c


def _shift_rows(x, s, fill):
    if s == 0:
        return x
    rolled = pltpu.roll(x, s, 0)
    row = lax.broadcasted_iota(jnp.int32, x.shape, 0)
    return jnp.where(row >= s, rolled, fill)


def _causal_conv(x, buf_ref, w, b, width):
    t = x.shape[0]
    buf_ref[SUBLANES:SUBLANES + t, :] = x
    y = b + w[width - 1:width, :] * x
    for j in range(width - 1):
        s = width - 1 - j
        y = y + w[j:j + 1, :] * buf_ref[SUBLANES - s:SUBLANES - s + t, :]
    buf_ref[0:SUBLANES, :] = x[t - SUBLANES:t, :]
    return y


def _const_spec(shape):
    nd = len(shape)
    return pl.BlockSpec(shape, lambda *_: (0,) * nd, pipeline_mode=pl.Buffered(1))


def _params(sem):
    return pltpu.CompilerParams(dimension_semantics=sem, vmem_limit_bytes=VMEM_LIMIT_BYTES)


def _norm_matmul_body(n_out, x_ref, g_ref, *refs):
    xn = _rmsnorm(x_ref[...], g_ref[...]).astype(BF16)
    for w_ref, o_ref in zip(refs[:n_out], refs[n_out:]):
        o_ref[...] = jnp.dot(xn, w_ref[...], preferred_element_type=F32).astype(o_ref.dtype)


def norm_matmul(x2d, g, ws, out_dtype=F32, name="norm_matmul"):
    n, d = x2d.shape
    tm = min(ROW_TILE, n)
    assert n % tm == 0
    return pl.pallas_call(
        functools.partial(_norm_matmul_body, len(ws)),
        out_shape=[jax.ShapeDtypeStruct((n, w.shape[1]), out_dtype) for w in ws],
        grid=(n // tm,),
        in_specs=[pl.BlockSpec((tm, d), lambda i: (i, 0)), _const_spec((1, d))]
        + [_const_spec(w.shape) for w in ws],
        out_specs=[pl.BlockSpec((tm, w.shape[1]), lambda i: (i, 0)) for w in ws],
        compiler_params=_params(("parallel",)),
        name=name,
    )(x2d, g.reshape(1, d), *ws)


def _gla_body(p_ref, wa_ref, ba_ref, ng_ref, o_ref, st_ref):
    L = GLA_CHUNK
    nb = p_ref.shape[0]
    n_chunks = p_ref.shape[1] // L
    M = nb * L
    hk = GLA_HEADS * GLA_DK
    hv = GLA_HEADS * GLA_DV

    @pl.when(pl.program_id(1) == 0)
    def _():
        st_ref[...] = jnp.zeros_like(st_ref)

    def chunk(c, carry):
        incl, _ = _tri_masks(L)
        ri = lax.broadcasted_iota(jnp.int32, (M, M), 0)
        ci = lax.broadcasted_iota(jnp.int32, (M, M), 1)
        same_seq = jnp.bitwise_and(ri, -L) == jnp.bitwise_and(ci, -L)
        tri01 = jnp.where(same_seq & (ri >= ci), 1.0, 0.0).astype(BF16)
        t0 = pl.multiple_of(c * L, L)
        rows = pl.ds(t0, L)
        cols = lambda lo, hi: jnp.concatenate([p_ref[b, rows, lo:hi] for b in range(nb)], axis=0)
        q = cols(0, hk)
        k = cols(hk, 2 * hk)
        v = cols(2 * hk, 2 * hk + hv)
        gt = cols(2 * hk + hv, 2 * hk + 2 * hv)
        a_lr = cols(2 * hk + 2 * hv, GLA_COLS_PAD)
        la = _log_sigmoid(_mm(a_lr, wa_ref[...]) + ba_ref[...]) * (1.0 / GLA_TAU)
        g = _mm_exact_lhs01(tri01, la)
        g_last = jnp.concatenate([jnp.broadcast_to(g[(b + 1) * L - 1:(b + 1) * L, :], (L, hk)) for b in range(nb)],
                                 axis=0)
        q_dec = (q * GLA_DK ** -0.5) * jnp.exp(g)
        k_inv = k * jnp.exp(-g)
        k_end = k * jnp.exp(g_last - g)
        sd = [jnp.exp(g[(b + 1) * L - 1:(b + 1) * L, :]) for b in range(nb)]
        U = [(b, h) for b in range(nb) for h in range(GLA_HEADS)]
        nu = range(len(U))
        rb = [slice(b * L, (b + 1) * L) for b in range(nb)]
        ks = [slice(h * GLA_DK, (h + 1) * GLA_DK) for h in range(GLA_HEADS)]
        vs = [slice(h * GLA_DV, (h + 1) * GLA_DV) for h in range(GLA_HEADS)]
        qd = [q_dec[rb[b], ks[h]] for b, h in U]
        vh = [v[rb[b], vs[h]] for b, h in U]
        st = [st_ref[i] for i in nu]
        sc = [jnp.where(incl, _mm_nt(qd[i], k_inv[rb[b], ks[h]]), 0.0) for i, (b, h) in enumerate(U)]
        o_state = [_mm_nt(qd[i], st[i]) for i in nu]
        upd = [_mm_tn(vh[i], k_end[rb[b], ks[h]]) for i, (b, h) in enumerate(U)]
        o_loc = [_mm(sc[i], vh[i]) for i in nu]
        for i, (b, h) in enumerate(U):
            st_ref[i] = st[i] * sd[b][:, ks[h]] + upd[i]
            o = o_loc[i] + o_state[i]
            o = o * lax.rsqrt(jnp.mean(o * o, axis=-1, keepdims=True) + NORM_EPS)
            o_ref[b, rows, vs[h]] = o * ng_ref[:, vs[h]] * _silu(gt[rb[b], vs[h]])
        return carry

    lax.fori_loop(0, n_chunks, chunk, 0)


def gla_mixer(pa, w_alpha2, b_alpha, norm_g):
    b, t, _ = pa.shape
    tt = min(MIX_TILE, t)
    nb = min(GLA_BATCH_PER_STEP, b)
    hk, hv = GLA_HEADS * GLA_DK, GLA_HEADS * GLA_DV
    wa = jnp.zeros((LANES, hk), BF16).at[:GLA_RANK].set(w_alpha2.astype(BF16))
    return pl.pallas_call(
        _gla_body,
        out_shape=jax.ShapeDtypeStruct((b, t, hv), F32),
        grid=(b // nb, t // tt),
        in_specs=[pl.BlockSpec((nb, tt, GLA_COLS_PAD), lambda i, j: (i, j, 0)),
                  _const_spec((LANES, hk)), _const_spec((1, hk)), _const_spec((1, hv))],
        out_specs=pl.BlockSpec((nb, tt, hv), lambda i, j: (i, j, 0)),
        scratch_shapes=[pltpu.VMEM((nb * GLA_HEADS, GLA_DV, GLA_DK), F32)],
        compiler_params=_params(("parallel", "arbitrary")),
        name="gla_mixer",
    )(pa, wa, b_alpha.reshape(1, hk), norm_g.reshape(1, hv))


_RWKV_SB = ("a_dec", "r_dec", "b_inv", "k_inv", "b_end", "k_end", "v")


def _seg_sum(x, seg):
    hw = seg.shape[0]
    x1 = x.astype(BF16)
    x2 = (x - x1.astype(F32)).astype(BF16)
    parts = [piece[:, s * hw:(s + 1) * hw] for piece in (x1, x2) for s in range(2)]
    out = jnp.dot(jnp.concatenate(parts, axis=0), seg, preferred_element_type=F32)
    m = x.shape[0]
    halves = [out[s * m:(s + 1) * m] + out[(2 + s) * m:(3 + s) * m] for s in range(2)]
    return jnp.concatenate(halves, axis=1)


def _proj_ab_body(h_ref, g_ref, wa_ref, wb_ref, mu_ref, w0_ref, w2_ref, a0_ref, a2_ref, g2_ref, kk_ref, ka_ref,
                  rk_ref, seg_ref, pa_ref, sb_ref, sf_ref, gl_ref, prev_ref):
    L = RWKV_CHUNK
    W = RWKV_WIDTH
    tm = h_ref.shape[1]
    n_blocks = tm // L

    @pl.when(pl.program_id(1) == 0)
    def _():
        prev_ref[...] = jnp.zeros_like(prev_ref)

    ng = PROJ_ROW_GROUPS
    gm = tm // ng
    G = range(ng)
    rg = [slice(i * gm, (i + 1) * gm) for i in G]
    xn = [_rmsnorm(h_ref[0, rg[i], :], g_ref[...]).astype(BF16) for i in G]
    p = [jnp.dot(xn[i], wb_ref[...], preferred_element_type=F32) for i in G]

    n_a = wa_ref.shape[1]
    a_cols = [(0, 4 * LANES), (4 * LANES, 8 * LANES), (8 * LANES, n_a)]
    pending = [(i, c) for c in a_cols for i in G]

    def project_a(count):
        for _ in range(count):
            if pending:
                i, (lo_c, hi_c) = pending.pop(0)
                pa_ref[0, rg[i], lo_c:hi_c] = jnp.dot(xn[i], wa_ref[:, lo_c:hi_c], preferred_element_type=F32)

    first = lax.broadcasted_iota(jnp.int32, (SUBLANES, p[0].shape[1]), 0) == 0
    before = [prev_ref[0:1, :]] + [p[i][gm - 1:gm, :] for i in range(ng - 1)]
    rolled = [pltpu.roll(p[i], 1, 0) for i in G]
    sh = [jnp.concatenate([jnp.where(first, before[i], rolled[i][0:SUBLANES, :]), rolled[i][SUBLANES:gm, :]],
                          axis=0) for i in G]
    prev_ref[0:1, :] = p[ng - 1][gm - 1:gm, :]
    pf = [p[i] + (sh[i] - p[i]) * mu_ref[...] for i in G]
    r = [pf[i][:, 0:W] for i in G]
    k = [pf[i][:, W:2 * W] for i in G]
    v = [pf[i][:, 2 * W:3 * W] for i in G]
    wa = [pf[i][:, 3 * W:3 * W + LANES] for i in G]
    g_lr = [pf[i][:, 3 * W + LANES:3 * W + 2 * LANES] for i in G]
    log_w = [-RWKV_DECAY_SCALE * jax.nn.sigmoid(w0_ref[...] + _mm(jnp.tanh(wa[i]), w2_ref[...])) for i in G]
    a = [jax.nn.sigmoid(a0_ref[...] + _mm(wa[i], a2_ref[...])) for i in G]
    gate = [_mm(jax.nn.sigmoid(g_lr[i]), g2_ref[...]) for i in G]
    project_a(2)
    kk = [k[i] * kk_ref[...] for i in G]
    kk = [kk[i] * lax.rsqrt(_seg_sum(kk[i] * kk[i], seg_ref[...]) + RWKV_L2_EPS) for i in G]
    project_a(2)
    k = [k[i] * (1.0 + (a[i] - 1.0) * ka_ref[...]) for i in G]
    b_vec = [kk[i] * a[i] for i in G]
    bonus_sum = [_seg_sum(r[i] * k[i] * rk_ref[...], seg_ref[...]) for i in G]
    project_a(2)
    row_in_block = jnp.bitwise_and(lax.broadcasted_iota(jnp.int32, (gm, W), 0), L - 1)
    nbk = gm // L
    for i in G:
        g = log_w[i]
        d = 1
        while d < L:
            g = g + jnp.where(row_in_block >= d, pltpu.roll(g, d, 0), 0.0)
            d *= 2
        g_last = jnp.concatenate([jnp.broadcast_to(g[(j + 1) * L - 1:(j + 1) * L, :], (L, W)) for j in range(nbk)],
                                 axis=0)
        e_neg = jnp.exp(-g)
        e_end = jnp.exp(g_last - g)
        vals = dict(a_dec=-kk[i] * jnp.exp(g - log_w[i]), r_dec=r[i] * jnp.exp(g), b_inv=b_vec[i] * e_neg,
                    k_inv=k[i] * e_neg, b_end=b_vec[i] * e_end, k_end=k[i] * e_end, v=v[i])
        for j, name in enumerate(_RWKV_SB):
            sb_ref[0, rg[i], j * W:(j + 1) * W] = vals[name].astype(BF16)
        sf_ref[0, rg[i], 0:W] = gate[i]
        sf_ref[0, rg[i], W:2 * W] = bonus_sum[i] * v[i]
        for j in range(nbk):
            row0 = (i * nbk + j) * SUBLANES
            gl_ref[0, row0:row0 + SUBLANES, :] = g[(j + 1) * L - SUBLANES:(j + 1) * L, :]
        project_a(1)
    project_a(len(pending))


def proj_ab(h, g, wa, wb, mu, w0, w2, a0, a2, g2, k_k, k_a, r_k):
    b, t, d = h.shape
    tm = min(ROW_TILE, t)
    W = RWKV_WIDTH
    row = lambda x: x.reshape(1, -1)
    w2p = jnp.zeros((LANES, W), BF16).at[:RWKV_LORA_W].set(w2.astype(BF16))
    a2p = jnp.zeros((LANES, W), BF16).at[RWKV_LORA_W:].set(a2.astype(BF16))
    head_of = jnp.arange(W // 2) // RWKV_N
    seg = (head_of[:, None] == head_of[None, :]).astype(BF16)
    consts = [row(g), wa, wb, row(mu), row(w0), w2p, row(a0), a2p, g2.astype(BF16), row(k_k), row(k_a), row(r_k),
              seg]
    nsb = len(_RWKV_SB) * W
    tok = lambda w: pl.BlockSpec((1, tm, w), lambda i, j: (i, j, 0))
    return pl.pallas_call(
        _proj_ab_body,
        out_shape=[jax.ShapeDtypeStruct((b, t, wa.shape[1]), F32), jax.ShapeDtypeStruct((b, t, nsb), BF16),
                   jax.ShapeDtypeStruct((b, t, 2 * W), F32),
                   jax.ShapeDtypeStruct((b, t // RWKV_CHUNK * SUBLANES, W), F32)],
        grid=(b, t // tm),
        in_specs=[tok(d)] + [_const_spec(x.shape) for x in consts],
        out_specs=[tok(wa.shape[1]), tok(nsb), tok(2 * W),
                   pl.BlockSpec((1, tm // RWKV_CHUNK * SUBLANES, W), lambda i, j: (i, j, 0))],
        scratch_shapes=[pltpu.VMEM((SUBLANES, wb.shape[1]), F32)],
        compiler_params=_params(("parallel", "arbitrary")),
        name="proj_ab",
    )(h, *consts)


def _rwkv_rec_body(sb_ref, sf_ref, gl_ref, lng_ref, lnb_ref, seg_ref, tri_ref, quad_ref, o_ref, sp_ref):
    L = RWKV_CHUNK
    W = RWKV_WIDTH
    N = RWKV_N
    nb = sb_ref.shape[0]
    n_chunks = sb_ref.shape[1] // L
    n_pairs = RWKV_HEADS // 2
    idx = {name: j for j, name in enumerate(_RWKV_SB)}

    @pl.when(pl.program_id(1) == 0)
    def _():
        sp_ref[...] = jnp.zeros_like(sp_ref)

    def chunk(c, carry):
        lane = lax.broadcasted_iota(jnp.int32, (L, LANES), 1)
        lo = lane < N
        hi = lane >= N
        nat = (lo, hi)
        strict_f = tri_ref[0]
        incl_f = tri_ref[1]
        eye2 = tri_ref[2]
        diag_f = quad_ref[0]
        anti_f = quad_ref[1]
        zb = jnp.zeros((L, LANES), BF16)
        sel = lambda m, x: jnp.where(m, x, jnp.zeros_like(x))
        swap = lambda x: pltpu.roll(x, N, 1)
        cat0 = lambda xs: jnp.concatenate(xs, axis=0)
        cat1 = lambda xs: jnp.concatenate(xs, axis=1)
        dot = lambda x, y: jnp.dot(x, y, preferred_element_type=F32)
        nt = lambda x, y: lax.dot_general(x, y, (((1,), (1,)), ((), ())), preferred_element_type=F32)
        rows = pl.ds(pl.multiple_of(c * L, L), L)
        U = [(b, p) for b in range(nb) for p in range(n_pairs)]
        nu = range(len(U))

        def blk(name, u):
            col = idx[name] * W + u[1] * LANES
            return sb_ref[u[0], rows, col:col + LANES]

        vn = lambda i, h: sel(nat[h], blk("v", U[i]))
        ad = [blk("a_dec", u) for u in U]
        rd = [blk("r_dec", u) for u in U]
        b_inv = [blk("b_inv", u) for u in U]
        k_inv = [blk("k_inv", u) for u in U]
        pe = [nt(cat0([sel(lo, ad[i]), sel(lo, rd[i])]), cat0([b_inv[i], k_inv[i]])) for i in nu]
        po = [nt(cat0([sel(hi, ad[i]), sel(hi, rd[i])]), cat0([k_inv[i], b_inv[i]])) for i in nu]
        aa = [[pe[i][0:L, :] * strict_f, po[i][0:L, :] * strict_f] for i in nu]
        ar = [[(pe[i][L:2 * L, :] * incl_f).astype(BF16), (po[i][L:2 * L, :] * incl_f).astype(BF16)] for i in nu]
        t0 = [dot(jnp.where(lo, aa[i][1], aa[i][0]).astype(BF16),
                  cat0([cat1([vn(i, 1), zb]), cat1([zb, vn(i, 0)])])) for i in nu]
        ad_sw = [swap(x) for x in ad]
        x0 = [[jnp.where(hi, ad_sw[i], t0[i][:, LANES:2 * LANES].astype(BF16)),
               jnp.where(lo, ad_sw[i], t0[i][:, 0:LANES].astype(BF16))] for i in nu]
        q0 = [jnp.where(lo, aa[i][0], eye2) for i in nu]
        q1 = [jnp.where(lo, eye2, aa[i][1]) for i in nu]
        for _ in range(6):
            out = [dot(jnp.where(lo, q0[i], q1[i]).astype(BF16),
                       cat0([cat1([q0[i].astype(BF16), zb]), cat1([zb, q1[i].astype(BF16)])])) for i in nu]
            q0 = [out[i][:, 0:LANES] + sel(hi, q0[i]) for i in nu]
            q1 = [out[i][:, LANES:2 * LANES] + sel(lo, q1[i]) for i in nu]
        tx = [dot(jnp.where(lo, q1[i], q0[i]).astype(BF16),
                  cat0([cat1([x0[i][1], zb]), cat1([zb, x0[i][0]])])) for i in nu]
        xb = [[tx[i][:, LANES:2 * LANES].astype(BF16), tx[i][:, 0:LANES].astype(BF16)] for i in nu]
        rd_sw = [swap(blk("r_dec", u)).astype(F32) for u in U]
        rmy = [[sel(hi, rd_sw[i]) + dot(ar[i][0], cat0([xb[i][0], vn(i, 0)])),
                sel(lo, rd_sw[i]) + dot(ar[i][1], cat0([vn(i, 1), xb[i][1]]))] for i in nu]
        be_sw = [swap(blk("b_end", u)) for u in U]
        ke_sw = [swap(blk("k_end", u)) for u in U]
        gg = [lax.dot_general(cat0([xb[i][0], xb[i][1], vn(i, 0), vn(i, 1)]),
                              cat0([sel(hi, be_sw[i]), sel(lo, be_sw[i]), sel(hi, ke_sw[i]), sel(lo, ke_sw[i])]),
                              (((0,), (0,)), ((), ())), preferred_element_type=F32) for i in nu]
        sp = [sp_ref[i] for i in nu]
        ys = [_mm_nt(jnp.where(lo, rmy[i][1], rmy[i][0]), sp[i]) + jnp.where(lo, rmy[i][0], rmy[i][1]) for i in nu]
        g_rows = pl.ds(pl.multiple_of(c * SUBLANES, SUBLANES), SUBLANES)
        for i, u in enumerate(U):
            sd_sw = jnp.exp(swap(gl_ref[u[0], g_rows, u[1] * LANES:(u[1] + 1) * LANES])[SUBLANES - 1:SUBLANES, :])
            sp_ref[i] = sp[i] * sd_sw + _mm(sp[i], gg[i] * diag_f) + gg[i] * anti_f
        y = cat0([cat1(ys[b * n_pairs:(b + 1) * n_pairs]) for b in range(nb)])
        mean = _seg_sum(y, seg_ref[...]) * (1.0 / N)
        dy = y - mean
        var = _seg_sum(dy * dy, seg_ref[...]) * (1.0 / N)
        y = dy * lax.rsqrt(var + RWKV_GN_EPS) * lng_ref[...] + lnb_ref[...]
        for b in range(nb):
            yb = y[b * L:(b + 1) * L, :]
            o_ref[b, rows, :] = (yb + sf_ref[b, rows, W:2 * W]) * sf_ref[b, rows, 0:W]
        return carry

    lax.fori_loop(0, n_chunks, chunk, 0)


def rwkv_recurrence(sb, sf, gl, ln_g, ln_b):
    b, t, _ = sb.shape
    tt = min(MIX_TILE, t)
    nb = min(RWKV_BATCH_PER_STEP, b)
    W = RWKV_WIDTH
    row = lambda x: x.reshape(1, -1)
    head_of = jnp.arange(W // 2) // RWKV_N
    seg = (head_of[:, None] == head_of[None, :]).astype(BF16)
    t_idx = jnp.arange(RWKV_CHUNK)[:, None]
    s_idx = jnp.arange(LANES)[None, :] % RWKV_N
    tri = jnp.stack([t_idx > s_idx, t_idx >= s_idx, t_idx == s_idx]).astype(F32)
    half = jnp.arange(LANES) // RWKV_N
    same_half = half[:, None] == half[None, :]
    quad = jnp.stack([same_half, ~same_half]).astype(F32)
    consts = [row(ln_g), row(ln_b), seg, tri, quad]
    tok = lambda w: pl.BlockSpec((nb, tt, w), lambda i, j: (i, j, 0))
    return pl.pallas_call(
        _rwkv_rec_body,
        out_shape=jax.ShapeDtypeStruct((b, t, W), F32),
        grid=(b // nb, t // tt),
        in_specs=[tok(sb.shape[2]), tok(sf.shape[2]),
                  pl.BlockSpec((nb, tt // RWKV_CHUNK * SUBLANES, W), lambda i, j: (i, j, 0))]
        + [_const_spec(x.shape) for x in consts],
        out_specs=tok(W),
        scratch_shapes=[pltpu.VMEM((nb * RWKV_HEADS // 2, LANES, LANES), F32)],
        compiler_params=_params(("parallel", "arbitrary")),
        name="rwkv_mixer",
    )(sb, sf, gl, *consts)


def _proj_cd_body(h_ref, g_ref, wc_ref, wd_ref, lcw_ref, lcb_ref, gw_ref, gb_ref, lam_ref, mcw_ref, mcb_ref,
                  wq_ref, wk_ref, wv_ref, bif_ref, lr_ref, md_ref, lwin_ref, mwin_ref):
    tm = h_ref.shape[1]
    W = LRU_WIDTH

    @pl.when(pl.program_id(1) == 0)
    def _():
        lwin_ref[0:SUBLANES, :] = jnp.zeros((SUBLANES, W), F32)
        mwin_ref[0:SUBLANES, :] = jnp.zeros((SUBLANES, W), F32)

    ng = PROJ_ROW_GROUPS
    gm = tm // ng
    G = range(ng)
    rg = [slice(i * gm, (i + 1) * gm) for i in G]
    xn = [_rmsnorm(h_ref[0, rg[i], :], g_ref[...]).astype(BF16) for i in G]
    pc = [jnp.dot(xn[i], wc_ref[...], preferred_element_type=F32) for i in G]
    pd = [jnp.dot(xn[i], wd_ref[...], preferred_element_type=F32) for i in G]
    half = W // 2
    for i in G:
        x = pc[i][:, 0:W]
        xc = _causal_conv(x, lwin_ref, lcw_ref[...], lcb_ref[...], LRU_CONV)
        xcb = xc.astype(BF16)
        pre = []
        for gi in range(2):
            pre.append(jnp.concatenate(
                [jnp.dot(xcb[:, n * LRU_BLOCK:(n + 1) * LRU_BLOCK], gw_ref[gi, n], preferred_element_type=F32)
                 for n in range(LRU_BLOCKS)], axis=1) + gb_ref[gi:gi + 1, :])
        r_gate = jax.nn.sigmoid(pre[0])
        i_gate = jax.nn.sigmoid(pre[1])
        log_a = -LRU_C * r_gate * _softplus(-lam_ref[...])
        a = jnp.exp(log_a)
        w2 = -jnp.tanh(log_a) * (a * a + 1.0)
        lr_ref[0, rg[i], 0:W] = a
        lr_ref[0, rg[i], W:2 * W] = jnp.where(w2 > 0.0, w2 * lax.rsqrt(w2), 0.0) * (i_gate * xc)
        lr_ref[0, rg[i], 2 * W:3 * W] = jax.nn.gelu(pc[i][:, W:2 * W])
        mx = pd[i][:, 0:W]
        mxc = _silu(_causal_conv(mx, mwin_ref, mcw_ref[...], mcb_ref[...], MLSTM_CONV)).astype(BF16)
        mxb = mx.astype(BF16)
        for s in range(2):
            cs = slice(s * half, (s + 1) * half)
            md_ref[0, rg[i], s * half:(s + 1) * half] = jnp.dot(mxc[:, cs], wq_ref[s], preferred_element_type=F32)
            md_ref[0, rg[i], W + s * half:W + (s + 1) * half] = (
                jnp.dot(mxc[:, cs], wk_ref[s], preferred_element_type=F32) * MLSTM_DH ** -0.5)
            md_ref[0, rg[i], 2 * W + s * half:2 * W + (s + 1) * half] = jnp.dot(
                mxb[:, cs], wv_ref[s], preferred_element_type=F32)
        md_ref[0, rg[i], 3 * W:4 * W] = jax.nn.sigmoid(pd[i][:, W:2 * W])
        md_ref[0, rg[i], 4 * W:4 * W + LANES] = pd[i][:, 2 * W:2 * W + LANES] + bif_ref[...]


def mlstm_qkv_tiles(qkv_w):
    *lead, three, nblk, d, e = qkv_w.shape
    return _block_diag(qkv_w.reshape(*lead, three, 2, nblk // 2, d, e)).astype(BF16)


def proj_cd(h, g, wc, wd, lru_conv_w, lru_conv_b, gate_w, gate_b, lam, m_conv_w, m_conv_b, qkv_tiles, b_if):
    b, t, d = h.shape
    tm = min(ROW_TILE, t)
    W = LRU_WIDTH
    row = lambda x: x.reshape(1, -1)
    pad_taps = lambda w: jnp.zeros((SUBLANES, W), F32).at[:w.shape[0]].set(w)
    bif = jnp.zeros((1, LANES), F32).at[0, :2 * MLSTM_HEADS].set(b_if)
    consts = [row(g), wc, wd, pad_taps(lru_conv_w), row(lru_conv_b), gate_w.astype(BF16), gate_b, row(lam),
              pad_taps(m_conv_w), row(m_conv_b)] + [qkv_tiles[i] for i in range(3)] + [bif]
    gm = tm // PROJ_ROW_GROUPS
    tok = lambda w: pl.BlockSpec((1, tm, w), lambda i, j: (i, j, 0))
    return pl.pallas_call(
        _proj_cd_body,
        out_shape=[jax.ShapeDtypeStruct((b, t, 3 * W), F32), jax.ShapeDtypeStruct((b, t, MLSTM_IN_COLS), F32)],
        grid=(b, t // tm),
        in_specs=[tok(d)] + [_const_spec(x.shape) for x in consts],
        out_specs=[tok(3 * W), tok(MLSTM_IN_COLS)],
        scratch_shapes=[pltpu.VMEM((SUBLANES + gm, W), F32), pltpu.VMEM((SUBLANES + gm, W), F32)],
        compiler_params=_params(("parallel", "arbitrary")),
        name="proj_cd",
    )(h, *consts)


def _lru_scan_body(p_ref, o_ref, hprev_ref):
    tt = p_ref.shape[1]
    W = LRU_WIDTH

    @pl.when(pl.program_id(1) == 0)
    def _():
        hprev_ref[...] = jnp.zeros_like(hprev_ref)

    a = p_ref[0, :, 0:W]
    u = p_ref[0, :, W:2 * W]
    row_in_group = jnp.bitwise_and(lax.broadcasted_iota(jnp.int32, (tt, W), 0), SUBLANES - 1)
    d = 1
    while d < SUBLANES:
        keep = row_in_group >= d
        u = u + a * jnp.where(keep, pltpu.roll(u, d, 0), 0.0)
        a = a * jnp.where(keep, pltpu.roll(a, d, 0), 1.0)
        d *= 2
    carry = hprev_ref[0:1, :]
    for g in range(tt // SUBLANES):
        rs = slice(g * SUBLANES, (g + 1) * SUBLANES)
        hg = u[rs, :] + a[rs, :] * carry
        carry = hg[SUBLANES - 1:SUBLANES, :]
        o_ref[0, rs, :] = hg * p_ref[0, rs, 2 * W:3 * W]
    hprev_ref[0:1, :] = carry


def lru_scan(lr):
    b, t, cols = lr.shape
    tt = min(ROW_TILE, t)
    W = LRU_WIDTH
    return pl.pallas_call(
        _lru_scan_body,
        out_shape=jax.ShapeDtypeStruct((b, t, W), F32),
        grid=(b, t // tt),
        in_specs=[pl.BlockSpec((1, tt, cols), lambda i, j: (i, j, 0))],
        out_specs=pl.BlockSpec((1, tt, W), lambda i, j: (i, j, 0)),
        scratch_shapes=[pltpu.VMEM((SUBLANES, W), F32)],
        compiler_params=_params(("parallel", "arbitrary")),
        name="rglru_mixer",
    )(lr)


def _mlstm_body(p_ref, ng_ref, o_ref, c_ref, m_ref):
    tt = p_ref.shape[1]
    L = MLSTM_CHUNK
    W = MLSTM_WIDTH
    DH = MLSTM_DH
    H = MLSTM_HEADS
    n_chunks = tt // L
    nb = p_ref.shape[0]

    @pl.when(pl.program_id(1) == 0)
    def _():
        c_ref[...] = jnp.zeros_like(c_ref)
        m_ref[...] = jnp.zeros_like(m_ref)

    def chunk(c, carry):
        incl, _ = _tri_masks(L)
        tri01 = jnp.where(incl, 1.0, 0.0).astype(BF16)
        t0 = pl.multiple_of(c * L, L)
        rows = pl.ds(t0, L)
        nt = lambda y: lax.dot_general(y, tri01, (((1,), (1,)), ((), ())), preferred_element_type=F32)
        gates, lsg, g_t, b_rows = [], [], [], []
        for b in range(nb):
            gates.append(p_ref[b, rows, 4 * W:4 * W + LANES])
            lsg.append(_log_sigmoid(gates[b]))
            g_t.append(gates[b].T[0:SUBLANES, :])
            x1, x2, x3 = _split3(_log_sigmoid(g_t[b]))
            b_rows.append(nt(x1) + nt(x2) + nt(x3))
        ones_b = jnp.ones((L, LANES), BF16)
        U = [(b, h) for b in range(nb) for h in range(H)]
        nu = range(len(U))
        hs = [slice(h * DH, (h + 1) * DH) for h in range(H)]
        col = lambda j, h: slice(j * W + h * DH, j * W + (h + 1) * DH)
        qh = [p_ref[b, rows, col(0, h)] for b, h in U]
        kh = [p_ref[b, rows, col(1, h)] for b, h in U]
        v_aug = [jnp.concatenate([p_ref[b, rows, col(2, h)].astype(BF16), ones_b], axis=1) for b, h in U]
        cm = [c_ref[i] for i in nu]
        m = [m_ref[i, 0:1, :] for i in nu]
        i_rep = [jnp.broadcast_to(gates[b][:, h:h + 1], (L, LANES)) for b, h in U]
        lf_rep = [jnp.broadcast_to(lsg[b][:, H + h:H + h + 1], (L, LANES)) for b, h in U]
        b_rep = [_mm_exact_lhs01(tri01, lf_rep[i]) for i in nu]
        qk = [_mm_nt(qh[i], kh[i]) for i in nu]
        b_last = [b_rep[i][L - 1:L, :] for i in nu]
        log_e = [b_last[i] - b_rep[i] + i_rep[i] for i in nu]
        m_end = [jnp.max(log_e[i], axis=0, keepdims=True) for i in nu]
        m_new = [jnp.maximum(b_last[i] + m[i], m_end[i]) for i in nu]
        kin = [kh[i] * (jnp.exp(m_end[i] - m_new[i]) * jnp.exp(log_e[i] - m_end[i])) for i in nu]
        c_upd = [_mm_tn(kin[i], v_aug[i]) for i in nu]
        run = [i_rep[i] - b_rep[i] for i in nu]
        d = 1
        while d < L:
            run = [jnp.maximum(run[i], _shift_rows(run[i], d, -jnp.inf)) for i in nu]
            d *= 2
        m_t = [jnp.maximum(b_rep[i] + m[i], b_rep[i] + run[i]) for i in nu]
        log_d = [jnp.where(incl, b_rep[i][:, 0:L] - b_rows[b][H + h:H + h + 1, :] + g_t[b][h:h + 1, :], -jnp.inf)
                 for i, (b, h) in enumerate(U)]
        w_loc = [jnp.exp(log_d[i] - m_t[i][:, 0:L]) * qk[i] for i in nu]
        lhs = [jnp.concatenate([(jnp.exp(b_rep[i] + m[i] - m_t[i]) * qh[i]).astype(BF16), w_loc[i].astype(BF16)],
                               axis=1) for i in nu]
        nd = [jnp.dot(lhs[i], jnp.concatenate([cm[i].astype(BF16), v_aug[i]], axis=0),
                      preferred_element_type=F32) for i in nu]
        for i, (b, h) in enumerate(U):
            c_state = jnp.exp(b_last[i] + m[i] - m_new[i])
            c_ref[i] = jnp.concatenate([c_state, c_state], axis=1) * cm[i] + c_upd[i]
            m_ref[i] = jnp.broadcast_to(m_new[i], (SUBLANES, LANES))
            hv = nd[i][:, 0:DH] / jnp.maximum(jnp.abs(nd[i][:, DH:2 * DH]), jnp.exp(-m_t[i]))
            hv = hv * lax.rsqrt(jnp.mean(hv * hv, axis=-1, keepdims=True) + NORM_EPS)
            o_ref[b, rows, hs[h]] = p_ref[b, rows, col(3, h)] * hv * ng_ref[:, hs[h]]
        return carry

    lax.fori_loop(0, n_chunks, chunk, 0)


def _block_diag(w):
    *lead, nb, d, e = w.shape
    rows = jnp.tile(w.reshape(*lead, nb * d, e), (1,) * len(lead) + (1, nb))
    same_block = (jnp.arange(nb * d)[:, None] // d) == (jnp.arange(nb * e)[None, :] // e)
    return jnp.where(same_block, rows, 0.0)


def mlstm_mixer(md, norm_g):
    b, t, cols = md.shape
    tt = min(MIX_TILE, t)
    nb = min(MLSTM_BATCH_PER_STEP, b)
    W = MLSTM_WIDTH
    return pl.pallas_call(
        _mlstm_body,
        out_shape=jax.ShapeDtypeStruct((b, t, W), F32),
        grid=(b // nb, t // tt),
        in_specs=[pl.BlockSpec((nb, tt, cols), lambda i, j: (i, j, 0)), _const_spec((1, W))],
        out_specs=pl.BlockSpec((nb, tt, W), lambda i, j: (i, j, 0)),
        scratch_shapes=[pltpu.VMEM((nb * MLSTM_HEADS, MLSTM_DH, 2 * MLSTM_DH), F32),
                        pltpu.VMEM((nb * MLSTM_HEADS, SUBLANES, LANES), F32)],
        compiler_params=_params(("parallel", "arbitrary")),
        name="mlstm_mixer",
    )(md, norm_g.reshape(1, W))


def _mix_xattn_body(h_ref, ya_ref, yb_ref, wm_ref, g_ref, wq_ref, k_ref, v_ref, wo_ref, o_ref):
    tm = h_ref.shape[1]
    ng = XATTN_ROW_GROUPS
    rg = [slice(i * (tm // ng), (i + 1) * (tm // ng)) for i in range(ng)]
    G = range(ng)
    HD = range(XA_HEADS)
    hs = [slice(hd * XA_DH, (hd + 1) * XA_DH) for hd in HD]
    mixed = [jnp.concatenate([ya_ref[0, rg[i], :], yb_ref[0, rg[i], :]], axis=1).astype(BF16) for i in G]
    h = [h_ref[0, rg[i], :] + jnp.dot(mixed[i], wm_ref[...], preferred_element_type=F32) for i in G]
    xn = [_rmsnorm(h[i], g_ref[...]).astype(BF16) for i in G]
    q = [jnp.dot(xn[i], wq_ref[...], preferred_element_type=F32) for i in G]
    s = [[_mm_nt(q[i][:, hs[hd]], k_ref[0, :, hs[hd]]) * XA_DH ** -0.5 for hd in HD] for i in G]
    e = [[jnp.exp(s[i][hd] - jnp.max(s[i][hd], axis=-1, keepdims=True)) for hd in HD] for i in G]
    p = [[e[i][hd] / jnp.sum(e[i][hd], axis=-1, keepdims=True) for hd in HD] for i in G]
    pv = [[_mm(p[i][hd], v_ref[0, :, hs[hd]]) for hd in HD] for i in G]
    for i in G:
        o = jnp.concatenate(pv[i], axis=1).astype(BF16)
        o_ref[0, rg[i], :] = h[i] + jnp.dot(o, wo_ref[...], preferred_element_type=F32)


def mix_xattn(h, ya, yb, w_mix, g, wq, kv, wo):
    b, t, d = h.shape
    tm = min(XATTN_TILE, t)
    tok = lambda w: pl.BlockSpec((1, tm, w), lambda i, j: (i, j, 0))
    mem_k = pl.BlockSpec((1, MEM_LEN, d), lambda i, j: (i, 0, 0))
    mem_v = pl.BlockSpec((1, MEM_LEN, d), lambda i, j: (i, 0, 1))
    return pl.pallas_call(
        _mix_xattn_body,
        out_shape=jax.ShapeDtypeStruct((b, t, d), F32),
        grid=(b, t // tm),
        in_specs=[tok(d), tok(ya.shape[-1]), tok(yb.shape[-1]), _const_spec(w_mix.shape), _const_spec((1, d)),
                  _const_spec(wq.shape), mem_k, mem_v, _const_spec(wo.shape)],
        out_specs=tok(d),
        compiler_params=_params(("parallel", "parallel")),
        name="mix_xattn",
    )(h, ya, yb, w_mix, g.reshape(1, d), wq, kv, kv, wo)


def _ffn_body(final_norm, h_ref, g_ref, wu_ref, wg_ref, cw_ref, cb_ref, wd_ref, fg_ref, o_ref, uprev_ref):
    tm = h_ref.shape[1]

    @pl.when(pl.program_id(1) == 0)
    def _():
        uprev_ref[...] = jnp.zeros_like(uprev_ref)

    h = h_ref[0]
    xn = _rmsnorm(h, g_ref[...]).astype(BF16)
    acc = h
    start = 0
    for width in FFN_COL_GROUPS:
        cs = slice(start, start + width)
        start += width
        u = jnp.dot(xn, wu_ref[:, cs], preferred_element_type=F32)
        gt = jnp.dot(xn, wg_ref[:, cs], preferred_element_type=F32)
        uu = jnp.concatenate([uprev_ref[:, cs], u], axis=0)
        uprev_ref[:, cs] = u[tm - SUBLANES:tm, :]
        c = cb_ref[:, cs] + cw_ref[2:3, cs] * u
        c = c + cw_ref[1:2, cs] * pltpu.roll(uu, 1, 0)[SUBLANES:SUBLANES + tm, :]
        c = c + cw_ref[0:1, cs] * pltpu.roll(uu, 2, 0)[SUBLANES:SUBLANES + tm, :]
        act = (_silu(c) * gt).astype(BF16)
        acc = acc + jnp.dot(act, wd_ref[cs, :], preferred_element_type=F32)
    if final_norm:
        acc = _rmsnorm(acc, fg_ref[...])
    o_ref[0] = acc


def ffn(h, g, wu, wg, conv_w, conv_b, wd, final_g, final_norm):
    b, t, d = h.shape
    tm = min(ROW_TILE, t)
    tok = pl.BlockSpec((1, tm, d), lambda i, j: (i, j, 0))
    consts = [g.reshape(1, d), wu, wg, conv_w, conv_b, wd, final_g.reshape(1, d)]
    return pl.pallas_call(
        functools.partial(_ffn_body, final_norm),
        out_shape=jax.ShapeDtypeStruct((b, t, d), F32),
        grid=(b, t // tm),
        in_specs=[tok] + [_const_spec(x.shape) for x in consts],
        out_specs=tok,
        scratch_shapes=[pltpu.VMEM((SUBLANES, D_FF_PAD), F32)],
        compiler_params=_params(("parallel", "arbitrary")),
        name="ffn",
    )(h, *consts)


def _pad_cols(w, n):
    return jnp.pad(w, ((0, 0), (0, n - w.shape[1])))


def _rwkv_perm(x):
    W = RWKV_WIDTH
    o_w, o_k, o_v = W, W + RWKV_LORA_W, 2 * W + RWKV_LORA_W
    o_a = 3 * W + RWKV_LORA_W
    o_g = o_a + RWKV_LORA_A
    return jnp.concatenate([x[..., :W], x[..., o_k:o_k + W], x[..., o_v:o_v + W], x[..., o_w:o_w + RWKV_LORA_W],
                            x[..., o_a:o_a + RWKV_LORA_A], x[..., o_g:]], axis=-1)


def kernel(x, mem, mem_norm_g, norm_mix_g, ab_w_in, gla_w_alpha2, gla_b_alpha, gla_norm_g, rwkv_mu, rwkv_w0, rwkv_w2, rwkv_a0, rwkv_a2, rwkv_g2, rwkv_k_k, rwkv_k_a, rwkv_r_k, rwkv_ln_g, rwkv_ln_b, cd_w_in, lru_conv_w, lru_conv_b, lru_gate_w, lru_gate_b, lru_lambda, mlstm_conv_w, mlstm_conv_b, mlstm_qkv_w, mlstm_b_if, mlstm_norm_g, w_mix_out, norm_xattn_g, xattn_wq, xattn_wkv, xattn_wo, norm_ffn_g, ffn_w_up, ffn_conv_w, ffn_conv_b, ffn_w_down, final_norm_g):
    b, t, d = x.shape
    depth = norm_mix_g.shape[0]
    n = b * t
    h = x
    mem2d = mem.reshape(b * MEM_LEN, d)
    qkv_tiles = mlstm_qkv_tiles(mlstm_qkv_w)
    wkv = xattn_wkv.astype(BF16)
    kvs = norm_matmul(mem2d, mem_norm_g, [wkv[layer] for layer in range(depth)], out_dtype=BF16, name="proj_kv")
    for layer in range(depth):
        j = layer // 2
        h2d = h.reshape(n, d)
        if layer % 2 == 0:
            w = ab_w_in[j]
            wa = _pad_cols(w[:, :GLA_COLS], GLA_COLS_PAD).astype(BF16)
            wb = _rwkv_perm(w[:, GLA_COLS:]).astype(BF16)
            pa, sb, sf, gl = proj_ab(h, norm_mix_g[layer], wa, wb, _rwkv_perm(rwkv_mu[j]), rwkv_w0[j], rwkv_w2[j],
                                     rwkv_a0[j], rwkv_a2[j], rwkv_g2[j], rwkv_k_k[j], rwkv_k_a[j], rwkv_r_k[j])
            ya = gla_mixer(pa, gla_w_alpha2[j], gla_b_alpha[j], gla_norm_g[j])
            yb = rwkv_recurrence(sb, sf, gl, rwkv_ln_g[j], rwkv_ln_b[j])
        else:
            w = cd_w_in[j]
            wc = w[:, :2 * LRU_WIDTH].astype(BF16)
            wd_ = _pad_cols(w[:, 2 * LRU_WIDTH:], MLSTM_COLS_PAD).astype(BF16)
            lr, md = proj_cd(h, norm_mix_g[layer], wc, wd_, lru_conv_w[j], lru_conv_b[j], lru_gate_w[j],
                             lru_gate_b[j], lru_lambda[j], mlstm_conv_w[j], mlstm_conv_b[j], qkv_tiles[j],
                             mlstm_b_if[j])
            ya = lru_scan(lr)
            yb = mlstm_mixer(md, mlstm_norm_g[j])
        kv = kvs[layer].reshape(b, MEM_LEN, 2 * d)
        h = mix_xattn(h, ya, yb, w_mix_out[layer].astype(BF16), norm_xattn_g[layer], xattn_wq[layer].astype(BF16),
                      kv, xattn_wo[layer].astype(BF16))
        wup = ffn_w_up[layer]
        wu = _pad_cols(wup[:, :D_FF], D_FF_PAD).astype(BF16)
        wg = _pad_cols(wup[:, D_FF:], D_FF_PAD).astype(BF16)
        cw = _pad_cols(jnp.pad(ffn_conv_w[layer], ((0, SUBLANES - FFN_CONV), (0, 0))), D_FF_PAD)
        cb = _pad_cols(ffn_conv_b[layer].reshape(1, D_FF), D_FF_PAD)
        wdn = jnp.pad(ffn_w_down[layer], ((0, D_FF_PAD - D_FF), (0, 0))).astype(BF16)
        h = ffn(h, norm_ffn_g[layer], wu, wg, cw, cb, wdn, final_norm_g, layer == depth - 1)
    return h
```
